```python
import jax, jax.numpy as jnp
from jax import lax
import numpy as np

D_MODEL = 1024
BATCH = 2
SEQ = 8192
DEPTH = 2

N_META = 16
MIX_WIDTH = D_MODEL
POOL_WIDTH = MIX_WIDTH // 2
POOL_GROUPS = 4
POOL_GROUP_DIM = POOL_WIDTH // POOL_GROUPS
POOL_WINDOWS = (2, 4, 8, 16)
GLA_WIDTH = MIX_WIDTH - POOL_WIDTH
GLA_HEADS = 4
GLA_KEY_WIDTH = GLA_WIDTH // 2
GLA_DK = GLA_KEY_WIDTH // GLA_HEADS
GLA_DV = GLA_WIDTH // GLA_HEADS
GLA_GATE_RANK = 16
GLA_GATE_TEMP = 16.0
GLA_CHUNK = 64
GLA_PAD = GLA_CHUNK - N_META
D_FF = 2816
IN_PROJ_COLS = POOL_WIDTH + 2 * GLA_KEY_WIDTH + 2 * GLA_WIDTH + GLA_GATE_RANK
DEEPNORM_ALPHA = (2.0 * DEPTH) ** 0.25
DEEPNORM_BETA = (8.0 * DEPTH) ** -0.25
LN_EPS = 1e-5
RMS_EPS = 1e-6

kernel_name = "hymba_pool_gla_macaron_deepnorm"


def layer_norm(x, g, b):
    xf = x.astype(jnp.float32)
    mu = jnp.mean(xf, axis=-1, keepdims=True)
    var = jnp.mean(jnp.square(xf - mu), axis=-1, keepdims=True)
    y = (xf - mu) * lax.rsqrt(var + LN_EPS)
    return (y * g.astype(jnp.float32) + b.astype(jnp.float32)).astype(x.dtype)


def swiglu(x, w_gate, w_up, w_down):
    return (jax.nn.silu(x @ w_gate) * (x @ w_up)) @ w_down


def pool_mixer(u, w_pool, pool_scale):
    B_, T, _ = u.shape
    uf = u.astype(jnp.float32)
    cs = jnp.cumsum(uf, axis=1)
    t = jnp.arange(T)
    outs = []
    for gi, w in enumerate(POOL_WINDOWS):
        sl = slice(gi * POOL_GROUP_DIM, (gi + 1) * POOL_GROUP_DIM)
        c = cs[..., sl]
        shifted = jnp.pad(c, ((0, 0), (w, 0), (0, 0)))[:, :T]
        cnt = jnp.minimum(t + 1, w).astype(jnp.float32)
        mean = (c - shifted) / cnt[None, :, None]
        outs.append(mean - uf[..., sl])
    p = jnp.stack(outs, axis=2).astype(u.dtype)
    p = jnp.einsum('btgc,gcd->btgd', p, w_pool).reshape(B_, T, POOL_WIDTH)
    return p * pool_scale


def gla_chunked(q, k, v, log_a):
    B_, H, Tp, dk = q.shape
    dv = v.shape[-1]
    n = Tp // GLA_CHUNK

    def chunks(z):
        return jnp.moveaxis(z.reshape(B_, H, n, GLA_CHUNK, z.shape[-1]), 2, 0)

    mask = jnp.tril(jnp.ones((GLA_CHUNK, GLA_CHUNK), dtype=bool))[:, :, None]

    def step(S, xs):
        qc, kc, vc, gc = xs
        b = jnp.cumsum(gc, axis=2)
        o_inter = jnp.einsum('bhid,bhde->bhie', qc * jnp.exp(b), S)
        diff = b[:, :, :, None, :] - b[:, :, None, :, :]
        decay = jnp.where(mask, jnp.exp(jnp.where(mask, diff, 0.0)), 0.0)
        A = jnp.einsum('bhid,bhjd,bhijd->bhij', qc, kc, decay)
        o_intra = jnp.einsum('bhij,bhje->bhie', A, vc)
        b_last = b[:, :, -1:, :]
        S_new = S * jnp.exp(b_last[:, :, 0, :])[..., None] + jnp.einsum(
            'bhjd,bhje->bhde', kc * jnp.exp(b_last - b), vc)
        return S_new, o_inter + o_intra

    S0 = jnp.zeros((B_, H, dk, dv), jnp.float32)
    _, o = lax.scan(step, S0, (chunks(q), chunks(k), chunks(v), chunks(log_a)))
    return jnp.moveaxis(o, 0, 2).reshape(B_, H, Tp, dv)


def gla_mixer(q, k, v, r, g_lr, w_gate_up, b_gate, gla_norm_g):
    B_, T, _ = q.shape
    log_a = jax.nn.log_sigmoid((g_lr @ w_gate_up + b_gate).astype(jnp.float32)) / GLA_GATE_TEMP

    def heads(z, d):
        z = z.astype(jnp.float32).reshape(B_, T, GLA_HEADS, d).transpose(0, 2, 1, 3)
        return jnp.pad(z, ((0, 0), (0, 0), (GLA_PAD, 0), (0, 0)))

    qh = heads(q, GLA_DK) * (GLA_DK ** -0.5)
    kh = heads(k, GLA_DK)
    vh = heads(v, GLA_DV)
    gh = heads(log_a, GLA_DK)
    o = gla_chunked(qh, kh, vh, gh)[:, :, GLA_PAD:]
    o = o.transpose(0, 2, 1, 3)
    o = o * lax.rsqrt(jnp.mean(jnp.square(o), axis=-1, keepdims=True) + RMS_EPS)
    o = o.reshape(B_, T, GLA_WIDTH) * gla_norm_g.astype(jnp.float32)
    return (o * jax.nn.silu(r.astype(jnp.float32))).astype(q.dtype)


def setup_inputs(seed: int = 0) -> dict:
    key = jax.random.key(seed)
    ks = jax.random.split(key, 24)
    f32 = jnp.float32

    def nrm(k, shape, scale):
        return jax.random.normal(k, shape, f32) * scale

    L = DEPTH
    return {
        "x": nrm(ks[0], (BATCH, SEQ, D_MODEL), 1.0),
        "meta_tokens": nrm(ks[1], (N_META, D_MODEL), 1.0),
        "ffn1_w_gate": nrm(ks[2], (L, D_MODEL, D_FF), D_MODEL ** -0.5),
        "ffn1_w_up": nrm(ks[3], (L, D_MODEL, D_FF), D_MODEL ** -0.5),
        "ffn1_w_down": nrm(ks[4], (L, D_FF, D_MODEL), DEEPNORM_BETA * D_FF ** -0.5),
        "ln1_g": 1.0 + nrm(ks[5], (L, D_MODEL), 0.02),
        "ln1_b": nrm(ks[6], (L, D_MODEL), 0.02),
        "w_in": nrm(ks[7], (L, D_MODEL, IN_PROJ_COLS), D_MODEL ** -0.5),
        "w_gate_up": nrm(ks[8], (L, GLA_GATE_RANK, GLA_KEY_WIDTH), GLA_GATE_RANK ** -0.5),
        "b_gate": nrm(ks[9], (L, GLA_KEY_WIDTH), 0.1),
        "w_pool": nrm(ks[10], (L, POOL_GROUPS, POOL_GROUP_DIM, POOL_GROUP_DIM), POOL_GROUP_DIM ** -0.5),
        "pool_scale": 1.0 + nrm(ks[11], (L, POOL_WIDTH), 0.1),
        "gla_norm_g": 1.0 + nrm(ks[12], (L, GLA_WIDTH), 0.02),
        "w_out": nrm(ks[13], (L, MIX_WIDTH, D_MODEL), DEEPNORM_BETA * MIX_WIDTH ** -0.5),
        "ln2_g": 1.0 + nrm(ks[14], (L, D_MODEL), 0.02),
        "ln2_b": nrm(ks[15], (L, D_MODEL), 0.02),
        "ffn2_w_gate": nrm(ks[16], (L, D_MODEL, D_FF), D_MODEL ** -0.5),
        "ffn2_w_up": nrm(ks[17], (L, D_MODEL, D_FF), D_MODEL ** -0.5),
        "ffn2_w_down": nrm(ks[18], (L, D_FF, D_MODEL), DEEPNORM_BETA * D_FF ** -0.5),
        "ln3_g": 1.0 + nrm(ks[19], (L, D_MODEL), 0.02),
        "ln3_b": nrm(ks[20], (L, D_MODEL), 0.02),
    }


def reference(x, meta_tokens, ffn1_w_gate, ffn1_w_up, ffn1_w_down, ln1_g, ln1_b,
              w_in, w_gate_up, b_gate, w_pool, pool_scale, gla_norm_g, w_out,
              ln2_g, ln2_b, ffn2_w_gate, ffn2_w_up, ffn2_w_down, ln3_g, ln3_b):
    B_ = x.shape[0]
    meta = jnp.broadcast_to(meta_tokens[None].astype(x.dtype), (B_, N_META, D_MODEL))
    h = jnp.concatenate([meta, x], axis=1)
    a = DEEPNORM_ALPHA
    s0 = POOL_WIDTH
    s1 = s0 + GLA_KEY_WIDTH
    s2 = s1 + GLA_KEY_WIDTH
    s3 = s2 + GLA_WIDTH
    s4 = s3 + GLA_WIDTH
    for l in range(DEPTH):
        h = layer_norm(a * h + 0.5 * swiglu(h, ffn1_w_gate[l], ffn1_w_up[l], ffn1_w_down[l]),
                       ln1_g[l], ln1_b[l])
        z = h @ w_in[l]
        u_pool = z[..., :s0]
        q, k, v, r, g_lr = z[..., s0:s1], z[..., s1:s2], z[..., s2:s3], z[..., s3:s4], z[..., s4:]
        y_pool = pool_mixer(u_pool, w_pool[l], pool_scale[l])
        y_gla = gla_mixer(q, k, v, r, g_lr, w_gate_up[l], b_gate[l], gla_norm_g[l])
        y = jnp.concatenate([y_pool, y_gla], axis=-1) @ w_out[l]
        h = layer_norm(a * h + y, ln2_g[l], ln2_b[l])
        h = layer_norm(a * h + 0.5 * swiglu(h, ffn2_w_gate[l], ffn2_w_up[l], ffn2_w_down[l]),
                       ln3_g[l], ln3_b[l])
    return h[:, N_META:]
```

```python
import functools

import numpy as np
import jax
import jax.numpy as jnp
from jax import lax
from jax.experimental import pallas as pl
from jax.experimental.pallas import tpu as pltpu

D_MODEL = 1024
N_META = 16
POOL_WIDTH = 512
POOL_GROUPS = 4
POOL_GROUP_DIM = 128
POOL_WINDOWS = (2, 4, 8, 16)
GLA_WIDTH = 512
GLA_HEADS = 4
GLA_KEY_WIDTH = 256
GLA_DK = 64
GLA_DV = 128
GLA_GATE_RANK = 16
GLA_GATE_TEMP = 16.0
GLA_CHUNK = 64
D_FF = 2816
LN_EPS = 1e-5
RMS_EPS = 1e-6

BATCH = 2
SC_ROWS = BATCH * GLA_CHUNK
SUB = 16
N_SUB = GLA_CHUNK // SUB
LANES = 128
FF_SPLIT = 2
VMEM_LIMIT_BYTES = 60 * 1024 * 1024

F32 = jnp.float32
BF16 = jnp.bfloat16


def _dot(a, b):
    return jnp.dot(a, b, preferred_element_type=F32)


def _dot_nt(a, b):
    return lax.dot_general(a, b, (((1,), (1,)), ((), ())), preferred_element_type=F32)


def _layer_norm(z, g, b):
    mu = jnp.mean(z, axis=-1, keepdims=True)
    zc = z - mu
    var = jnp.mean(zc * zc, axis=-1, keepdims=True)
    return zc * lax.rsqrt(var + LN_EPS) * g + b


def _silu(x):
    return x * jax.nn.sigmoid(x)


def _log_sigmoid(x):
    return jnp.minimum(x, 0.0) - jnp.log1p(jnp.exp(-jnp.abs(x)))


def _swiglu_half_step(x, wg_ref, wu_ref, wd_ref, alpha):
    xb = x.astype(BF16)
    fc = D_FF // FF_SPLIT
    y = None
    for c in range(FF_SPLIT):
        cols = slice(c * fc, (c + 1) * fc)
        g = _dot(xb, wg_ref[:, cols])
        u = _dot(xb, wu_ref[:, cols])
        act = (_silu(g) * u).astype(BF16)
        part = _dot(act, wd_ref[cols, :])
        y = part if y is None else y + part
    return alpha * x + 0.5 * y


def _ffn_inproj_kernel(alpha, tile_rows,
                       x_ref, wg_ref, wu_ref, wd_ref, lng_ref, lnb_ref,
                       wtm_ref, wgu_ref, bg_ref, wcm_ref, wgt_ref, wgut_ref, bgc_ref,
                       h_ref, u_ref, k_ref, la_ref, qt_ref, kt_ref, lat_ref, vt_ref, rt_ref):
    x = x_ref[...]
    h = _layer_norm(_swiglu_half_step(x, wg_ref, wu_ref, wd_ref, alpha), lng_ref[...], lnb_ref[...])
    h_ref[...] = h
    hb = h.astype(BF16)

    ztm = _dot(hb, wtm_ref[...])
    u_ref[...] = ztm[:, :POOL_WIDTH]
    k_ref[...] = ztm[:, POOL_WIDTH:POOL_WIDTH + GLA_KEY_WIDTH]
    glr = ztm[:, POOL_WIDTH + GLA_KEY_WIDTH:].astype(BF16)
    la_ref[...] = _log_sigmoid(_dot(glr, wgu_ref[...]) + bg_ref[...]) * (1.0 / GLA_GATE_TEMP)

    zt = _dot_nt(wcm_ref[...], hb)
    kw = GLA_KEY_WIDTH
    qt_ref[...] = zt[:kw]
    kt_ref[...] = zt[kw:2 * kw]
    vt_ref[...] = zt[2 * kw:2 * kw + GLA_WIDTH]
    rt_ref[...] = _silu(zt[2 * kw + GLA_WIDTH:])
    gt = _dot_nt(wgt_ref[...], hb).astype(BF16)
    lat_ref[...] = _log_sigmoid(_dot(wgut_ref[...], gt) + bgc_ref[...]) * (1.0 / GLA_GATE_TEMP)

    @pl.when(pl.program_id(0) == pl.num_programs(0) - 1)
    def _():
        base = tile_rows - SC_ROWS
        pad = GLA_CHUNK - N_META
        rows = lax.broadcasted_iota(jnp.int32, (SC_ROWS, 1), 0)
        keep_r = (rows % GLA_CHUNK) >= pad
        lanes = lax.broadcasted_iota(jnp.int32, (1, SC_ROWS), 1)
        keep_l = (lanes % GLA_CHUNK) >= pad
        for ref in (u_ref, k_ref, la_ref):
            ref[base:, :] = jnp.where(keep_r, ref[base:, :], 0.0)
        for ref in (qt_ref, kt_ref, lat_ref, vt_ref):
            ref[:, base:] = jnp.where(keep_l, ref[:, base:], 0.0)


def _split3(x):
    hi = x.astype(BF16)
    r1 = x - hi.astype(F32)
    mid = r1.astype(BF16)
    lo = (r1 - mid.astype(F32)).astype(BF16)
    return hi, mid, lo


def _mixer_kernel(u_ref, k_ref, la_ref, qt_ref, kt_ref, lat_ref, vt_ref, rt_ref,
                  lbd_ref, ucat_ref, slot_ref, dmask_ref, eye_ref, gn_ref, cnt_ref,
                  wpool_ref, pscale_ref,
                  y_ref,
                  st_ref, ext_ref, b_scr, sd_scr):
    step = pl.program_id(0)
    C = GLA_CHUNK

    @pl.when(step == 0)
    def _():
        st_ref[...] = jnp.zeros(st_ref.shape, F32)
        ext_ref[:, 0:SUB, :] = jnp.zeros((BATCH, SUB, POOL_WIDTH), F32)

    u = u_ref[...]
    for b in range(BATCH):
        ext_ref[b, SUB:SUB + C, :] = u[C * b:C * (b + 1), :]
    cnt = cnt_ref[...]
    parts = []
    for g, w in enumerate(POOL_WINDOWS):
        cols = slice(POOL_GROUP_DIM * g, POOL_GROUP_DIM * (g + 1))
        sums = []
        for b in range(BATCH):
            s = ext_ref[b, SUB:SUB + C, cols]
            for back in range(1, w):
                s = s + ext_ref[b, SUB - back:SUB + C - back, cols]
            sums.append(s)
        s = jnp.concatenate(sums, axis=0)
        p = s / cnt[:, cols] - u[:, cols]
        parts.append(_dot(p.astype(BF16), wpool_ref[g]))
    y_pool = jnp.concatenate(parts, axis=1) * pscale_ref[...]
    for b in range(BATCH):
        ext_ref[b, 0:SUB, :] = ext_ref[b, C:C + SUB, :]

    la_parts = _split3(la_ref[...])
    lbd = lbd_ref[...]
    b_tm = _dot(lbd, la_parts[0]) + _dot(lbd, la_parts[1]) + _dot(lbd, la_parts[2])
    lat_parts = _split3(lat_ref[...])
    ucat = ucat_ref[...]
    bcat = _dot(lat_parts[0], ucat) + _dot(lat_parts[1], ucat) + _dot(lat_parts[2], ucat)
    bT = bcat[:, :SC_ROWS]
    cprevT = bcat[:, SC_ROWS:]

    b_scr[...] = b_tm
    ends = [[b_scr[pl.ds(C * b + SUB * j + SUB - 1, 1), :] for j in range(N_SUB)] for b in range(BATCH)]

    def per_block(fn):
        return jnp.concatenate(
            [jnp.broadcast_to(fn(b, j), (SUB, GLA_KEY_WIDTH)) for b in range(BATCH) for j in range(N_SUB)], axis=0)

    e_own = per_block(lambda b, j: ends[b][j])
    e_last = per_block(lambda b, j: ends[b][N_SUB - 1])
    f2 = per_block(lambda b, j: jnp.exp(ends[b][min(j + 1, N_SUB - 1)] - ends[b][j]))
    f3 = per_block(lambda b, j: jnp.exp(ends[b][min(j + 2, N_SUB - 1)] - ends[b][j]))
    k_tm = k_ref[...]
    khat = k_tm * jnp.exp(e_own - b_tm)
    ktil = k_tm * jnp.exp(e_last - b_tm)
    kslots = jnp.concatenate([khat, khat * f2, khat * f3], axis=0).astype(BF16)

    qT = qt_ref[...]
    kT = kt_ref[...]
    qtT = qT * jnp.exp(bT - cprevT)
    qeT = qT * jnp.exp(bT)
    row_head = lax.broadcasted_iota(jnp.int32, (GLA_KEY_WIDTH, SC_ROWS), 0) // GLA_DK
    qbd = jnp.concatenate([jnp.where(row_head == h, qtT, 0.0) for h in range(GLA_HEADS)], axis=1).astype(BF16)

    r_all = _dot(kslots, qbd)
    slot = slot_ref[...]
    at_off = jnp.where(slot == 1, r_all[0:SC_ROWS],
                       jnp.where(slot == 2, r_all[SC_ROWS:2 * SC_ROWS],
                                 jnp.where(slot == 3, r_all[2 * SC_ROWS:], 0.0)))

    for dist in range(SUB):
        if dist == 0:
            qs, bs = qT, bT
        else:
            qs = pltpu.roll(qT, SC_ROWS - dist, axis=1)
            bs = pltpu.roll(bT, SC_ROWS - dist, axis=1)
        prod = qs * kT * jnp.exp(jnp.minimum(bs - bT, 0.0))
        for h in range(GLA_HEADS):
            sd_scr[pl.ds(SUB * h + dist, 1), :] = jnp.sum(prod[GLA_DK * h:GLA_DK * (h + 1)], axis=0, keepdims=True)
    sd = jnp.concatenate([sd_scr[...], jnp.zeros((SC_ROWS - GLA_HEADS * SUB, SC_ROWS), F32)], axis=0)
    sdt = sd.T

    lane = lax.broadcasted_iota(jnp.int32, (SC_ROWS, SC_ROWS), 1)
    rowi = lax.broadcasted_iota(jnp.int32, (SC_ROWS, SC_ROWS), 0)
    lane64 = lax.broadcasted_iota(jnp.int32, (C, SC_ROWS), 1)
    lane8 = lax.broadcasted_iota(jnp.int32, (8, SC_ROWS), 1)
    lane_lo = lane < C
    row_lo = rowi < C
    dmask = dmask_ref[...]
    eye = eye_ref[...]
    vT = vt_ref[...]
    rT = rt_ref[...]
    gn = gn_ref[...]
    y_heads = []
    for h in range(GLA_HEADS):
        xh = jnp.where((lane >= SUB * h) & (lane < SUB * (h + 1)), sdt, 0.0)
        skew = pltpu.roll(xh, (SC_ROWS - SUB * h) % SC_ROWS, axis=1, stride=1, stride_axis=0)
        at_h = jnp.where(dmask != 0, skew, 0.0) + at_off[:, SC_ROWS * h:SC_ROWS * (h + 1)]

        half = (h % 2) * C
        kt_cols = ktil[:, LANES * (h // 2):LANES * (h // 2 + 1)]
        kt_roll = pltpu.roll(kt_cols, C, axis=1)
        lo_src, hi_src = (kt_cols, kt_roll) if half == 0 else (kt_roll, kt_cols)
        kbd = jnp.where(row_lo & lane_lo, lo_src, jnp.where((~row_lo) & (~lane_lo), hi_src, 0.0))
        qe_h = qeT[GLA_DK * h:GLA_DK * (h + 1)]
        qebd = jnp.concatenate([jnp.where(lane64 < C, qe_h, 0.0), jnp.where(lane64 >= C, qe_h, 0.0)], axis=0)

        v_h = vT[GLA_DV * h:GLA_DV * (h + 1)].astype(BF16)
        res = _dot(v_h, jnp.concatenate([at_h, kbd], axis=1).astype(BF16))
        st = st_ref[h]
        o_t = res[:, :SC_ROWS] + _dot(st.astype(BF16), qebd.astype(BF16))

        e0 = jnp.broadcast_to(ends[0][N_SUB - 1][:, LANES * (h // 2):LANES * (h // 2 + 1)], (8, LANES))
        e1 = jnp.broadcast_to(ends[1][N_SUB - 1][:, LANES * (h // 2):LANES * (h // 2 + 1)], (8, LANES))
        if half == 0:
            e1 = pltpu.roll(e1, C, axis=1)
        else:
            e0 = pltpu.roll(e0, C, axis=1)
        dec = jnp.exp(jnp.where(lane8 < C, e0, e1))
        st_ref[h] = st * jnp.broadcast_to(dec[0:1], (GLA_DV, SC_ROWS)) + res[:, SC_ROWS:]

        ms = jnp.mean(o_t * o_t, axis=0, keepdims=True)
        gsl = slice(GLA_DV * h, GLA_DV * (h + 1))
        y_heads.append(o_t * lax.rsqrt(ms + RMS_EPS) * gn[gsl] * rT[gsl])
    y_t = jnp.concatenate(y_heads, axis=0).astype(BF16)
    y_gla = _dot_nt(eye, y_t)

    y_ref[...] = jnp.concatenate([y_pool, y_gla], axis=1).astype(BF16)


def _outproj_ffn_kernel(alpha, h_ref, y_ref, wo_ref, l2g_ref, l2b_ref,
                        wg_ref, wu_ref, wd_ref, l3g_ref, l3b_ref, o_ref):
    h = h_ref[...]
    h2 = _layer_norm(alpha * h + _dot(y_ref[...], wo_ref[...]), l2g_ref[...], l2b_ref[...])
    o_ref[...] = _layer_norm(_swiglu_half_step(h2, wg_ref, wu_ref, wd_ref, alpha), l3g_ref[...], l3b_ref[...])


def _resident(shape):
    nd = len(shape)
    return pl.BlockSpec(shape, lambda i: (0,) * nd, pipeline_mode=pl.Buffered(1))


def _mixer_constants():
    r = np.arange(SC_ROWS)
    b, t = r // GLA_CHUNK, r % GLA_CHUNK
    same_b = b[:, None] == b[None, :]
    lbd = same_b & (t[None, :] <= t[:, None])
    ubd = lbd.T
    blk = t // SUB
    uprev = same_b & (blk[:, None] < blk[None, :])
    ucat = np.concatenate([ubd, uprev], axis=1)
    dist = np.where(same_b, blk[None, :] - blk[:, None], 0)
    slot = np.where((dist >= 1) & (dist < N_SUB), dist, 0).astype(np.int32)
    slot = np.tile(slot, (1, GLA_HEADS))
    dmask = (same_b & (blk[:, None] == blk[None, :]) & (t[None, :] >= t[:, None])).astype(np.int32)
    eye = np.eye(SC_ROWS)
    w_lane = np.repeat(np.array(POOL_WINDOWS, np.float32), POOL_GROUP_DIM)[None, :]
    t_meta = np.maximum(t - (GLA_CHUNK - N_META), 0).astype(np.float32)[:, None]
    cnt = np.stack([np.broadcast_to(w_lane, (SC_ROWS, POOL_WIDTH)), np.minimum(t_meta + 1.0, w_lane)])
    return (jnp.asarray(lbd, BF16), jnp.asarray(ucat, BF16), jnp.asarray(slot), jnp.asarray(dmask),
            jnp.asarray(eye, BF16), jnp.asarray(cnt, F32))


def _tile_rows(n_sc):
    k = 3 if n_sc % 3 == 0 else 1
    return k * SC_ROWS


def kernel(x, meta_tokens, ffn1_w_gate, ffn1_w_up, ffn1_w_down, ln1_g, ln1_b, w_in, w_gate_up, b_gate, w_pool, pool_scale, gla_norm_g, w_out, ln2_g, ln2_b, ffn2_w_gate, ffn2_w_up, ffn2_w_down, ln3_g, ln3_b):
    batch, seq, d = x.shape
    depth = w_in.shape[0]
    assert batch == BATCH and d == D_MODEL and seq % GLA_CHUNK == 0
    n_ch = seq // GLA_CHUNK
    n_sc = n_ch + 1
    n_rows = n_sc * SC_ROWS
    tr = _tile_rows(n_sc)
    n_tiles = n_rows // tr
    alpha = (2.0 * depth) ** 0.25

    xi = x.reshape(batch, n_ch, GLA_CHUNK, d).transpose(1, 0, 2, 3).reshape(n_ch * SC_ROWS, d)
    meta_half = jnp.concatenate([jnp.zeros((GLA_CHUNK - N_META, d), x.dtype), meta_tokens.astype(x.dtype)], axis=0)
    h = jnp.concatenate([xi, meta_half, meta_half], axis=0)

    lbd, ucat, slot, dmask, eye, cnt = _mixer_constants()
    s0 = POOL_WIDTH
    s1 = s0 + GLA_KEY_WIDTH
    s2 = s1 + GLA_KEY_WIDTH
    s3 = s2 + GLA_WIDTH
    s4 = s3 + GLA_WIDTH
    cparams = pltpu.CompilerParams(dimension_semantics=("arbitrary",), vmem_limit_bytes=VMEM_LIMIT_BYTES)
    row_tile = lambda cols: pl.BlockSpec((tr, cols), lambda i: (i, 0))
    col_tile = lambda rows: pl.BlockSpec((rows, tr), lambda i: (0, i))
    sc_of = lambda i: lax.rem(i + n_ch, n_sc)
    sc_rows = lambda cols: pl.BlockSpec((SC_ROWS, cols), lambda i: (sc_of(i), 0))
    sc_cols = lambda rows: pl.BlockSpec((rows, SC_ROWS), lambda i: (0, sc_of(i)))

    for l in range(depth):
        wl = w_in[l]
        glr_pad = jnp.zeros((d, LANES - GLA_GATE_RANK), F32)
        wtm = jnp.concatenate([wl[:, :s0], wl[:, s1:s2], wl[:, s4:], glr_pad], axis=1).astype(BF16)
        wgu = jnp.concatenate([w_gate_up[l], jnp.zeros((LANES - GLA_GATE_RANK, GLA_KEY_WIDTH), F32)], axis=0).astype(BF16)
        wcm = jnp.concatenate([wl[:, s0:s1] * (GLA_DK ** -0.5), wl[:, s1:s2], wl[:, s2:s3], wl[:, s3:s4]], axis=1).T.astype(BF16)
        wgt = jnp.concatenate([wl[:, s4:], glr_pad], axis=1).T.astype(BF16)
        wgut = wgu.T
        bg = b_gate[l].reshape(1, GLA_KEY_WIDTH)
        bgc = b_gate[l].reshape(GLA_KEY_WIDTH, 1)
        wg1, wu1, wd1 = ffn1_w_gate[l].astype(BF16), ffn1_w_up[l].astype(BF16), ffn1_w_down[l].astype(BF16)

        outs = pl.pallas_call(
            functools.partial(_ffn_inproj_kernel, alpha, tr),
            grid=(n_tiles,),
            in_specs=[row_tile(d), _resident(wg1.shape), _resident(wu1.shape), _resident(wd1.shape),
                      _resident((1, d)), _resident((1, d)),
                      _resident(wtm.shape), _resident(wgu.shape), _resident(bg.shape),
                      _resident(wcm.shape), _resident(wgt.shape), _resident(wgut.shape), _resident(bgc.shape)],
            out_specs=[row_tile(d), row_tile(POOL_WIDTH), row_tile(GLA_KEY_WIDTH), row_tile(GLA_KEY_WIDTH),
                       col_tile(GLA_KEY_WIDTH), col_tile(GLA_KEY_WIDTH), col_tile(GLA_KEY_WIDTH),
                       col_tile(GLA_WIDTH), col_tile(GLA_WIDTH)],
            out_shape=[jax.ShapeDtypeStruct((n_rows, d), F32),
                       jax.ShapeDtypeStruct((n_rows, POOL_WIDTH), F32),
                       jax.ShapeDtypeStruct((n_rows, GLA_KEY_WIDTH), F32),
                       jax.ShapeDtypeStruct((n_rows, GLA_KEY_WIDTH), F32),
                       jax.ShapeDtypeStruct((GLA_KEY_WIDTH, n_rows), F32),
                       jax.ShapeDtypeStruct((GLA_KEY_WIDTH, n_rows), F32),
                       jax.ShapeDtypeStruct((GLA_KEY_WIDTH, n_rows), F32),
                       jax.ShapeDtypeStruct((GLA_WIDTH, n_rows), F32),
                       jax.ShapeDtypeStruct((GLA_WIDTH, n_rows), F32)],
            compiler_params=cparams,
            name=f"ffn_inproj_{l}",
        )(h, wg1, wu1, wd1, ln1_g[l].reshape(1, d), ln1_b[l].reshape(1, d),
          wtm, wgu, bg, wcm, wgt, wgut, bgc)
        h1, u_tm, k_tm, la_tm, q_t, k_t, la_t, v_t, r_t = outs

        gn = jnp.broadcast_to(gla_norm_g[l].reshape(GLA_WIDTH, 1), (GLA_WIDTH, SC_ROWS))
        y_cat = pl.pallas_call(
            _mixer_kernel,
            grid=(n_sc,),
            in_specs=[sc_rows(POOL_WIDTH), sc_rows(GLA_KEY_WIDTH), sc_rows(GLA_KEY_WIDTH),
                      sc_cols(GLA_KEY_WIDTH), sc_cols(GLA_KEY_WIDTH), sc_cols(GLA_KEY_WIDTH),
                      sc_cols(GLA_WIDTH), sc_cols(GLA_WIDTH),
                      _resident(lbd.shape), _resident(ucat.shape), _resident(slot.shape), _resident(dmask.shape),
                      _resident(eye.shape), _resident(gn.shape),
                      pl.BlockSpec((None, SC_ROWS, POOL_WIDTH), lambda i: (jnp.where(i == 0, 1, 0), 0, 0)),
                      _resident((POOL_GROUPS, POOL_GROUP_DIM, POOL_GROUP_DIM)), _resident((1, POOL_WIDTH))],
            out_specs=sc_rows(d),
            out_shape=jax.ShapeDtypeStruct((n_rows, d), BF16),
            scratch_shapes=[pltpu.VMEM((GLA_HEADS, GLA_DV, SC_ROWS), F32),
                            pltpu.VMEM((BATCH, SUB + GLA_CHUNK, POOL_WIDTH), F32),
                            pltpu.VMEM((SC_ROWS, GLA_KEY_WIDTH), F32),
                            pltpu.VMEM((GLA_HEADS * SUB, SC_ROWS), F32)],
            compiler_params=cparams,
            name=f"mixer_{l}",
        )(u_tm, k_tm, la_tm, q_t, k_t, la_t, v_t, r_t, lbd, ucat, slot, dmask, eye, gn, cnt,
          w_pool[l].astype(BF16), pool_scale[l].reshape(1, POOL_WIDTH))

        wo = w_out[l].astype(BF16)
        wg2, wu2, wd2 = ffn2_w_gate[l].astype(BF16), ffn2_w_up[l].astype(BF16), ffn2_w_down[l].astype(BF16)
        h = pl.pallas_call(
            functools.partial(_outproj_ffn_kernel, alpha),
            grid=(n_tiles,),
            in_specs=[row_tile(d), row_tile(d), _resident(wo.shape), _resident((1, d)), _resident((1, d)),
                      _resident(wg2.shape), _resident(wu2.shape), _resident(wd2.shape),
                      _resident((1, d)), _resident((1, d))],
            out_specs=row_tile(d),
            out_shape=jax.ShapeDtypeStruct((n_rows, d), F32),
            compiler_params=cparams,
            name=f"outproj_ffn_{l}",
        )(h1, y_cat, wo, ln2_g[l].reshape(1, d), ln2_b[l].reshape(1, d),
          wg2, wu2, wd2, ln3_g[l].reshape(1, d), ln3_b[l].reshape(1, d))

    out = h[:n_ch * SC_ROWS].reshape(n_ch, batch, GLA_CHUNK, d).transpose(1, 0, 2, 3)
    return out.reshape(batch, seq, d)
```

```python
import functools

import numpy as np
import jax
import jax.numpy as jnp
from jax import lax
from jax.experimental import pallas as pl
from jax.experimental.pallas import tpu as pltpu

D_MODEL = 1024
N_META = 16
POOL_WIDTH = 512
POOL_GROUPS = 4
POOL_GROUP_DIM = 128
POOL_WINDOWS = (2, 4, 8, 16)
GLA_WIDTH = 512
GLA_HEADS = 4
GLA_KEY_WIDTH = 256
GLA_DK = 64
GLA_DV = 128
GLA_GATE_RANK = 16
GLA_GATE_TEMP = 16.0
GLA_CHUNK = 64
D_FF = 2816
LN_EPS = 1e-5
RMS_EPS = 1e-6

BATCH = 2
SC_ROWS = BATCH * GLA_CHUNK
SUB = 16
N_SUB = GLA_CHUNK // SUB
LANES = 128
FF_SPLIT = 2
VMEM_LIMIT_BYTES = 60 * 1024 * 1024
FAST_PATH_MAX_DECAY = 40.0
FLAG_ROWS = 8

F32 = jnp.float32
BF16 = jnp.bfloat16


def _dot(a, b):
    return jnp.dot(a, b, preferred_element_type=F32)


def _dot_nt(a, b):
    return lax.dot_general(a, b, (((1,), (1,)), ((), ())), preferred_element_type=F32)


def _layer_norm(z, g, b):
    mu = jnp.mean(z, axis=-1, keepdims=True)
    zc = z - mu
    var = jnp.mean(zc * zc, axis=-1, keepdims=True)
    return zc * lax.rsqrt(var + LN_EPS) * g + b


def _silu(x):
    return x * jax.nn.sigmoid(x)


def _log_sigmoid(x):
    return jnp.minimum(x, 0.0) - jnp.log1p(jnp.exp(-jnp.abs(x)))


def _swiglu_half_step(x, wg_ref, wu_ref, wd_ref, alpha):
    xb = x.astype(BF16)
    fc = D_FF // FF_SPLIT
    y = None
    for c in range(FF_SPLIT):
        cols = slice(c * fc, (c + 1) * fc)
        g = _dot(xb, wg_ref[:, cols])
        u = _dot(xb, wu_ref[:, cols])
        act = (_silu(g) * u).astype(BF16)
        part = _dot(act, wd_ref[cols, :])
        y = part if y is None else y + part
    return alpha * x + 0.5 * y


def _ffn_inproj_kernel(alpha, tile_rows,
                       x_ref, wg_ref, wu_ref, wd_ref, lng_ref, lnb_ref,
                       wtm_ref, wgu_ref, bg_ref, wcm_ref, wgt_ref, wgut_ref, bgc_ref,
                       h_ref, u_ref, k_ref, la_ref, qt_ref, kt_ref, lat_ref, vt_ref, rt_ref, flag_ref):
    x = x_ref[...]
    h = _layer_norm(_swiglu_half_step(x, wg_ref, wu_ref, wd_ref, alpha), lng_ref[...], lnb_ref[...])
    h_ref[...] = h
    hb = h.astype(BF16)

    ztm = _dot(hb, wtm_ref[...])
    u_ref[...] = ztm[:, :POOL_WIDTH]
    k_ref[...] = ztm[:, POOL_WIDTH:POOL_WIDTH + GLA_KEY_WIDTH]
    glr = ztm[:, POOL_WIDTH + GLA_KEY_WIDTH:].astype(BF16)
    la_ref[...] = _log_sigmoid(_dot(glr, wgu_ref[...]) + bg_ref[...]) * (1.0 / GLA_GATE_TEMP)

    zt = _dot_nt(wcm_ref[...], hb)
    kw = GLA_KEY_WIDTH
    qt_ref[...] = zt[:kw]
    kt_ref[...] = zt[kw:2 * kw]
    vt_ref[...] = zt[2 * kw:2 * kw + GLA_WIDTH]
    rt_ref[...] = _silu(zt[2 * kw + GLA_WIDTH:])
    gt = _dot_nt(wgt_ref[...], hb).astype(BF16)
    lat_ref[...] = _log_sigmoid(_dot(wgut_ref[...], gt) + bgc_ref[...]) * (1.0 / GLA_GATE_TEMP)

    @pl.when(pl.program_id(0) == pl.num_programs(0) - 1)
    def _():
        base = tile_rows - SC_ROWS
        pad = GLA_CHUNK - N_META
        rows = lax.broadcasted_iota(jnp.int32, (SC_ROWS, 1), 0)
        keep_r = (rows % GLA_CHUNK) >= pad
        lanes = lax.broadcasted_iota(jnp.int32, (1, SC_ROWS), 1)
        keep_l = (lanes % GLA_CHUNK) >= pad
        for ref in (u_ref, k_ref, la_ref):
            ref[base:, :] = jnp.where(keep_r, ref[base:, :], 0.0)
        for ref in (qt_ref, kt_ref, lat_ref, vt_ref):
            ref[:, base:] = jnp.where(keep_l, ref[:, base:], 0.0)

    la = la_ref[...]
    frow = lax.broadcasted_iota(jnp.int32, (FLAG_ROWS, LANES), 0)
    flags = jnp.zeros((FLAG_ROWS, LANES), jnp.int32)
    for s in range(tile_rows // SC_ROWS):
        tot = jnp.minimum(jnp.sum(la[SC_ROWS * s:SC_ROWS * s + GLA_CHUNK], axis=0, keepdims=True),
                          jnp.sum(la[SC_ROWS * s + GLA_CHUNK:SC_ROWS * (s + 1)], axis=0, keepdims=True))
        slow = (jnp.min(tot, axis=1, keepdims=True) < -FAST_PATH_MAX_DECAY).astype(jnp.int32)
        flags = jnp.where(frow == s, slow, flags)
    flag_ref[...] = flags


def _split3(x):
    hi = x.astype(BF16)
    r1 = x - hi.astype(F32)
    mid = r1.astype(BF16)
    lo = (r1 - mid.astype(F32)).astype(BF16)
    return hi, mid, lo


def _mixer_kernel(n_ch, flags_ref,
                  u_ref, k_ref, la_ref, qt_ref, kt_ref, lat_ref, vt_ref, rt_ref,
                  lbd_ref, ubd_ref, uprev_ref, slot_ref, dmask_ref, causal_ref, eye_ref, gn_ref, cnt_ref,
                  wpool_ref, pscale_ref,
                  y_ref,
                  st_ref, ext_ref, b_scr, sd_scr, at_scr):
    step = pl.program_id(0)
    C = GLA_CHUNK
    slow = flags_ref[lax.rem(step + n_ch, n_ch + 1)]

    @pl.when(step == 0)
    def _():
        st_ref[...] = jnp.zeros(st_ref.shape, F32)
        ext_ref[:, 0:SUB, :] = jnp.zeros((BATCH, SUB, POOL_WIDTH), F32)

    u = u_ref[...]
    for b in range(BATCH):
        ext_ref[b, SUB:SUB + C, :] = u[C * b:C * (b + 1), :]
    cnt = cnt_ref[...]
    parts = []
    for g, w in enumerate(POOL_WINDOWS):
        cols = slice(POOL_GROUP_DIM * g, POOL_GROUP_DIM * (g + 1))
        sums = []
        for b in range(BATCH):
            s = ext_ref[b, SUB:SUB + C, cols]
            for back in range(1, w):
                s = s + ext_ref[b, SUB - back:SUB + C - back, cols]
            sums.append(s)
        s = jnp.concatenate(sums, axis=0)
        p = s / cnt[:, cols] - u[:, cols]
        parts.append(_dot(p.astype(BF16), wpool_ref[g]))
    y_pool = jnp.concatenate(parts, axis=1) * pscale_ref[...]
    for b in range(BATCH):
        ext_ref[b, 0:SUB, :] = ext_ref[b, C:C + SUB, :]

    la_parts = _split3(la_ref[...])
    lbd = lbd_ref[...]
    b_tm = _dot(lbd, la_parts[0]) + _dot(lbd, la_parts[1]) + _dot(lbd, la_parts[2])
    lat_parts = _split3(lat_ref[...])
    ubd = ubd_ref[...]
    bT = _dot(lat_parts[0], ubd) + _dot(lat_parts[1], ubd) + _dot(lat_parts[2], ubd)

    b_scr[...] = b_tm
    ends = [[b_scr[pl.ds(C * b + SUB * j + SUB - 1, 1), :] for j in range(N_SUB)] for b in range(BATCH)]

    def per_block(fn):
        return jnp.concatenate(
            [jnp.broadcast_to(fn(b, j), (SUB, GLA_KEY_WIDTH)) for b in range(BATCH) for j in range(N_SUB)], axis=0)

    k_tm = k_ref[...]
    qT = qt_ref[...]
    e_last = per_block(lambda b, j: ends[b][N_SUB - 1])
    ktil = k_tm * jnp.exp(e_last - b_tm)
    qeT = qT * jnp.exp(bT)
    row_head = lax.broadcasted_iota(jnp.int32, (GLA_KEY_WIDTH, SC_ROWS), 0) // GLA_DK

    def heads_on_lanes(xt):
        return jnp.concatenate([jnp.where(row_head == h, xt, 0.0) for h in range(GLA_HEADS)], axis=1).astype(BF16)

    lane = lax.broadcasted_iota(jnp.int32, (SC_ROWS, SC_ROWS), 1)

    @pl.when(slow == 0)
    def _():
        kneg = (k_tm * jnp.exp(-b_tm)).astype(BF16)
        at_scr[...] = jnp.where(causal_ref[...] != 0, _dot(kneg, heads_on_lanes(qeT)), 0.0)

    @pl.when(slow != 0)
    def _():
        lat_p = _split3(lat_ref[...])
        uprev = uprev_ref[...]
        cprevT = _dot(lat_p[0], uprev) + _dot(lat_p[1], uprev) + _dot(lat_p[2], uprev)
        e_own = per_block(lambda b, j: ends[b][j])
        f2 = per_block(lambda b, j: jnp.exp(ends[b][min(j + 1, N_SUB - 1)] - ends[b][j]))
        f3 = per_block(lambda b, j: jnp.exp(ends[b][min(j + 2, N_SUB - 1)] - ends[b][j]))
        khat = k_tm * jnp.exp(e_own - b_tm)
        kslots = jnp.concatenate([khat, khat * f2, khat * f3], axis=0).astype(BF16)
        r_all = _dot(kslots, heads_on_lanes(qT * jnp.exp(bT - cprevT)))
        slot = slot_ref[...]
        at_off = jnp.where(slot == 1, r_all[0:SC_ROWS],
                           jnp.where(slot == 2, r_all[SC_ROWS:2 * SC_ROWS],
                                     jnp.where(slot == 3, r_all[2 * SC_ROWS:], 0.0)))

        kT = kt_ref[...]
        for dist in range(SUB):
            if dist == 0:
                qs, bs = qT, bT
            else:
                qs = pltpu.roll(qT, SC_ROWS - dist, axis=1)
                bs = pltpu.roll(bT, SC_ROWS - dist, axis=1)
            prod = qs * kT * jnp.exp(jnp.minimum(bs - bT, 0.0))
            for h in range(GLA_HEADS):
                sd_scr[pl.ds(SUB * h + dist, 1), :] = jnp.sum(prod[GLA_DK * h:GLA_DK * (h + 1)], axis=0, keepdims=True)
        sd = jnp.concatenate([sd_scr[...], jnp.zeros((SC_ROWS - GLA_HEADS * SUB, SC_ROWS), F32)], axis=0)
        sdt = sd.T
        dmask = dmask_ref[...]
        for h in range(GLA_HEADS):
            xh = jnp.where((lane >= SUB * h) & (lane < SUB * (h + 1)), sdt, 0.0)
            skew = pltpu.roll(xh, (SC_ROWS - SUB * h) % SC_ROWS, axis=1, stride=1, stride_axis=0)
            at_scr[:, SC_ROWS * h:SC_ROWS * (h + 1)] = (
                jnp.where(dmask != 0, skew, 0.0) + at_off[:, SC_ROWS * h:SC_ROWS * (h + 1)])

    rowi = lax.broadcasted_iota(jnp.int32, (SC_ROWS, SC_ROWS), 0)
    lane64 = lax.broadcasted_iota(jnp.int32, (C, SC_ROWS), 1)
    lane8 = lax.broadcasted_iota(jnp.int32, (8, SC_ROWS), 1)
    lane_lo = lane < C
    row_lo = rowi < C
    eye = eye_ref[...]
    vT = vt_ref[...]
    rT = rt_ref[...]
    gn = gn_ref[...]
    y_heads = []
    for h in range(GLA_HEADS):
        at_h = at_scr[:, SC_ROWS * h:SC_ROWS * (h + 1)]

        half = (h % 2) * C
        kt_cols = ktil[:, LANES * (h // 2):LANES * (h // 2 + 1)]
        kt_roll = pltpu.roll(kt_cols, C, axis=1)
        lo_src, hi_src = (kt_cols, kt_roll) if half == 0 else (kt_roll, kt_cols)
        kbd = jnp.where(row_lo & lane_lo, lo_src, jnp.where((~row_lo) & (~lane_lo), hi_src, 0.0))
        qe_h = qeT[GLA_DK * h:GLA_DK * (h + 1)]
        qebd = jnp.concatenate([jnp.where(lane64 < C, qe_h, 0.0), jnp.where(lane64 >= C, qe_h, 0.0)], axis=0)

        v_h = vT[GLA_DV * h:GLA_DV * (h + 1)].astype(BF16)
        res = _dot(v_h, jnp.concatenate([at_h, kbd], axis=1).astype(BF16))
        st = st_ref[h]
        o_t = res[:, :SC_ROWS] + _dot(st.astype(BF16), qebd.astype(BF16))

        e0 = jnp.broadcast_to(ends[0][N_SUB - 1][:, LANES * (h // 2):LANES * (h // 2 + 1)], (8, LANES))
        e1 = jnp.broadcast_to(ends[1][N_SUB - 1][:, LANES * (h // 2):LANES * (h // 2 + 1)], (8, LANES))
        if half == 0:
            e1 = pltpu.roll(e1, C, axis=1)
        else:
            e0 = pltpu.roll(e0, C, axis=1)
        dec = jnp.exp(jnp.where(lane8 < C, e0, e1))
        st_ref[h] = st * jnp.broadcast_to(dec[0:1], (GLA_DV, SC_ROWS)) + res[:, SC_ROWS:]

        ms = jnp.mean(o_t * o_t, axis=0, keepdims=True)
        gsl = slice(GLA_DV * h, GLA_DV * (h + 1))
        y_heads.append(o_t * lax.rsqrt(ms + RMS_EPS) * gn[gsl] * rT[gsl])
    y_t = jnp.concatenate(y_heads, axis=0).astype(BF16)
    y_gla = _dot_nt(eye, y_t)

    y_ref[...] = jnp.concatenate([y_pool, y_gla], axis=1).astype(BF16)


def _outproj_ffn_kernel(alpha, h_ref, y_ref, wo_ref, l2g_ref, l2b_ref,
                        wg_ref, wu_ref, wd_ref, l3g_ref, l3b_ref, o_ref):
    h = h_ref[...]
    h2 = _layer_norm(alpha * h + _dot(y_ref[...], wo_ref[...]), l2g_ref[...], l2b_ref[...])
    o_ref[...] = _layer_norm(_swiglu_half_step(h2, wg_ref, wu_ref, wd_ref, alpha), l3g_ref[...], l3b_ref[...])


def _resident(shape):
    nd = len(shape)
    return pl.BlockSpec(shape, lambda i: (0,) * nd, pipeline_mode=pl.Buffered(1))


def _mixer_constants():
    r = np.arange(SC_ROWS)
    b, t = r // GLA_CHUNK, r % GLA_CHUNK
    same_b = b[:, None] == b[None, :]
    lbd = same_b & (t[None, :] <= t[:, None])
    ubd = lbd.T
    blk = t // SUB
    uprev = same_b & (blk[:, None] < blk[None, :])
    dist = np.where(same_b, blk[None, :] - blk[:, None], 0)
    slot = np.where((dist >= 1) & (dist < N_SUB), dist, 0).astype(np.int32)
    slot = np.tile(slot, (1, GLA_HEADS))
    dmask = (same_b & (blk[:, None] == blk[None, :]) & (t[None, :] >= t[:, None])).astype(np.int32)
    causal = np.tile((same_b & (t[None, :] >= t[:, None])).astype(np.int32), (1, GLA_HEADS))
    eye = np.eye(SC_ROWS)
    w_lane = np.repeat(np.array(POOL_WINDOWS, np.float32), POOL_GROUP_DIM)[None, :]
    t_meta = np.maximum(t - (GLA_CHUNK - N_META), 0).astype(np.float32)[:, None]
    cnt = np.stack([np.broadcast_to(w_lane, (SC_ROWS, POOL_WIDTH)), np.minimum(t_meta + 1.0, w_lane)])
    return (jnp.asarray(lbd, BF16), jnp.asarray(ubd, BF16), jnp.asarray(uprev, BF16), jnp.asarray(slot),
            jnp.asarray(dmask), jnp.asarray(causal), jnp.asarray(eye, BF16), jnp.asarray(cnt, F32))


def _tile_rows(n_sc):
    k = 3 if n_sc % 3 == 0 else 1
    return k * SC_ROWS


def kernel(x, meta_tokens, ffn1_w_gate, ffn1_w_up, ffn1_w_down, ln1_g, ln1_b, w_in, w_gate_up, b_gate, w_pool, pool_scale, gla_norm_g, w_out, ln2_g, ln2_b, ffn2_w_gate, ffn2_w_up, ffn2_w_down, ln3_g, ln3_b):
    batch, seq, d = x.shape
    depth = w_in.shape[0]
    assert batch == BATCH and d == D_MODEL and seq % GLA_CHUNK == 0
    n_ch = seq // GLA_CHUNK
    n_sc = n_ch + 1
    n_rows = n_sc * SC_ROWS
    tr = _tile_rows(n_sc)
    n_tiles = n_rows // tr
    alpha = (2.0 * depth) ** 0.25

    xi = x.reshape(batch, n_ch, GLA_CHUNK, d).transpose(1, 0, 2, 3).reshape(n_ch * SC_ROWS, d)
    meta_half = jnp.concatenate([jnp.zeros((GLA_CHUNK - N_META, d), x.dtype), meta_tokens.astype(x.dtype)], axis=0)
    h = jnp.concatenate([xi, meta_half, meta_half], axis=0)

    lbd, ubd, uprev, slot, dmask, causal, eye, cnt = _mixer_constants()
    s0 = POOL_WIDTH
    s1 = s0 + GLA_KEY_WIDTH
    s2 = s1 + GLA_KEY_WIDTH
    s3 = s2 + GLA_WIDTH
    s4 = s3 + GLA_WIDTH
    cparams = pltpu.CompilerParams(dimension_semantics=("arbitrary",), vmem_limit_bytes=VMEM_LIMIT_BYTES)
    row_tile = lambda cols: pl.BlockSpec((tr, cols), lambda i: (i, 0))
    col_tile = lambda rows: pl.BlockSpec((rows, tr), lambda i: (0, i))
    sc_of = lambda i: lax.rem(i + n_ch, n_sc)
    sc_rows = lambda cols: pl.BlockSpec((SC_ROWS, cols), lambda i, f: (sc_of(i), 0))
    sc_cols = lambda rows: pl.BlockSpec((rows, SC_ROWS), lambda i, f: (0, sc_of(i)))

    def mix_const(shape):
        nd = len(shape)
        return pl.BlockSpec(shape, lambda i, f: (0,) * nd, pipeline_mode=pl.Buffered(1))

    for l in range(depth):
        wl = w_in[l]
        glr_pad = jnp.zeros((d, LANES - GLA_GATE_RANK), F32)
        wtm = jnp.concatenate([wl[:, :s0], wl[:, s1:s2], wl[:, s4:], glr_pad], axis=1).astype(BF16)
        wgu = jnp.concatenate([w_gate_up[l], jnp.zeros((LANES - GLA_GATE_RANK, GLA_KEY_WIDTH), F32)], axis=0).astype(BF16)
        wcm = jnp.concatenate([wl[:, s0:s1] * (GLA_DK ** -0.5), wl[:, s1:s2], wl[:, s2:s3], wl[:, s3:s4]], axis=1).T.astype(BF16)
        wgt = jnp.concatenate([wl[:, s4:], glr_pad], axis=1).T.astype(BF16)
        wgut = wgu.T
        bg = b_gate[l].reshape(1, GLA_KEY_WIDTH)
        bgc = b_gate[l].reshape(GLA_KEY_WIDTH, 1)
        wg1, wu1, wd1 = ffn1_w_gate[l].astype(BF16), ffn1_w_up[l].astype(BF16), ffn1_w_down[l].astype(BF16)

        outs = pl.pallas_call(
            functools.partial(_ffn_inproj_kernel, alpha, tr),
            grid=(n_tiles,),
            in_specs=[row_tile(d), _resident(wg1.shape), _resident(wu1.shape), _resident(wd1.shape),
                      _resident((1, d)), _resident((1, d)),
                      _resident(wtm.shape), _resident(wgu.shape), _resident(bg.shape),
                      _resident(wcm.shape), _resident(wgt.shape), _resident(wgut.shape), _resident(bgc.shape)],
            out_specs=[row_tile(d), row_tile(POOL_WIDTH), row_tile(GLA_KEY_WIDTH), row_tile(GLA_KEY_WIDTH),
                       col_tile(GLA_KEY_WIDTH), col_tile(GLA_KEY_WIDTH), col_tile(GLA_KEY_WIDTH),
                       col_tile(GLA_WIDTH), col_tile(GLA_WIDTH),
                       pl.BlockSpec((FLAG_ROWS, LANES), lambda i: (i, 0))],
            out_shape=[jax.ShapeDtypeStruct((n_rows, d), F32),
                       jax.ShapeDtypeStruct((n_rows, POOL_WIDTH), F32),
                       jax.ShapeDtypeStruct((n_rows, GLA_KEY_WIDTH), F32),
                       jax.ShapeDtypeStruct((n_rows, GLA_KEY_WIDTH), F32),
                       jax.ShapeDtypeStruct((GLA_KEY_WIDTH, n_rows), F32),
                       jax.ShapeDtypeStruct((GLA_KEY_WIDTH, n_rows), F32),
                       jax.ShapeDtypeStruct((GLA_KEY_WIDTH, n_rows), F32),
                       jax.ShapeDtypeStruct((GLA_WIDTH, n_rows), F32),
                       jax.ShapeDtypeStruct((GLA_WIDTH, n_rows), F32),
                       jax.ShapeDtypeStruct((n_tiles * FLAG_ROWS, LANES), jnp.int32)],
            compiler_params=cparams,
            name=f"ffn_inproj_{l}",
        )(h, wg1, wu1, wd1, ln1_g[l].reshape(1, d), ln1_b[l].reshape(1, d),
          wtm, wgu, bg, wcm, wgt, wgut, bgc)
        h1, u_tm, k_tm, la_tm, q_t, k_t, la_t, v_t, r_t, tile_flags = outs
        sc_flags = tile_flags.reshape(n_tiles, FLAG_ROWS, LANES)[:, :tr // SC_ROWS, 0].reshape(n_sc)

        gn = jnp.broadcast_to(gla_norm_g[l].reshape(GLA_WIDTH, 1), (GLA_WIDTH, SC_ROWS))
        y_cat = pl.pallas_call(
            functools.partial(_mixer_kernel, n_ch),
            grid_spec=pltpu.PrefetchScalarGridSpec(
                num_scalar_prefetch=1,
                grid=(n_sc,),
                in_specs=[sc_rows(POOL_WIDTH), sc_rows(GLA_KEY_WIDTH), sc_rows(GLA_KEY_WIDTH),
                          sc_cols(GLA_KEY_WIDTH), sc_cols(GLA_KEY_WIDTH), sc_cols(GLA_KEY_WIDTH),
                          sc_cols(GLA_WIDTH), sc_cols(GLA_WIDTH),
                          mix_const(lbd.shape), mix_const(ubd.shape), mix_const(uprev.shape), mix_const(slot.shape),
                          mix_const(dmask.shape), mix_const(causal.shape), mix_const(eye.shape), mix_const(gn.shape),
                          pl.BlockSpec((None, SC_ROWS, POOL_WIDTH), lambda i, f: (jnp.where(i == 0, 1, 0), 0, 0)),
                          mix_const((POOL_GROUPS, POOL_GROUP_DIM, POOL_GROUP_DIM)), mix_const((1, POOL_WIDTH))],
                out_specs=sc_rows(d),
                scratch_shapes=[pltpu.VMEM((GLA_HEADS, GLA_DV, SC_ROWS), F32),
                                pltpu.VMEM((BATCH, SUB + GLA_CHUNK, POOL_WIDTH), F32),
                                pltpu.VMEM((SC_ROWS, GLA_KEY_WIDTH), F32),
                                pltpu.VMEM((GLA_HEADS * SUB, SC_ROWS), F32),
                                pltpu.VMEM((SC_ROWS, GLA_HEADS * SC_ROWS), F32)]),
            out_shape=jax.ShapeDtypeStruct((n_rows, d), BF16),
            compiler_params=cparams,
            name=f"mixer_{l}",
        )(sc_flags, u_tm, k_tm, la_tm, q_t, k_t, la_t, v_t, r_t, lbd, ubd, uprev, slot, dmask, causal, eye, gn, cnt,
          w_pool[l].astype(BF16), pool_scale[l].reshape(1, POOL_WIDTH))

        wo = w_out[l].astype(BF16)
        wg2, wu2, wd2 = ffn2_w_gate[l].astype(BF16), ffn2_w_up[l].astype(BF16), ffn2_w_down[l].astype(BF16)
        h = pl.pallas_call(
            functools.partial(_outproj_ffn_kernel, alpha),
            grid=(n_tiles,),
            in_specs=[row_tile(d), row_tile(d), _resident(wo.shape), _resident((1, d)), _resident((1, d)),
                      _resident(wg2.shape), _resident(wu2.shape), _resident(wd2.shape),
                      _resident((1, d)), _resident((1, d))],
            out_specs=row_tile(d),
            out_shape=jax.ShapeDtypeStruct((n_rows, d), F32),
            compiler_params=cparams,
            name=f"outproj_ffn_{l}",
        )(h1, y_cat, wo, ln2_g[l].reshape(1, d), ln2_b[l].reshape(1, d),
          wg2, wu2, wd2, ln3_g[l].reshape(1, d), ln3_b[l].reshape(1, d))

    out = h[:n_ch * SC_ROWS].reshape(n_ch, batch, GLA_CHUNK, d).transpose(1, 0, 2, 3)
    return out.reshape(batch, seq, d)
```

```python
import functools

import numpy as np
import jax
import jax.numpy as jnp
from jax import lax
from jax.experimental import pallas as pl
from jax.experimental.pallas import tpu as pltpu

D_MODEL = 1024
N_META = 16
POOL_WIDTH = 512
POOL_GROUPS = 4
POOL_GROUP_DIM = 128
POOL_WINDOWS = (2, 4, 8, 16)
GLA_WIDTH = 512
GLA_HEADS = 4
GLA_KEY_WIDTH = 256
GLA_DK = 64
GLA_DV = 128
GLA_GATE_RANK = 16
GLA_GATE_TEMP = 16.0
GLA_CHUNK = 64
D_FF = 2816
LN_EPS = 1e-5
RMS_EPS = 1e-6

BATCH = 2
SC_ROWS = BATCH * GLA_CHUNK
SUB = 16
N_SUB = GLA_CHUNK // SUB
LANES = 128
FF_SPLIT = 2
VMEM_LIMIT_BYTES = 60 * 1024 * 1024
FAST_PATH_MAX_DECAY = 40.0
FLAG_ROWS = 8

F32 = jnp.float32
BF16 = jnp.bfloat16


def _dot(a, b):
    return jnp.dot(a, b, preferred_element_type=F32)


def _dot_nt(a, b):
    return lax.dot_general(a, b, (((1,), (1,)), ((), ())), preferred_element_type=F32)


def _layer_norm(z, g, b):
    mu = jnp.mean(z, axis=-1, keepdims=True)
    zc = z - mu
    var = jnp.mean(zc * zc, axis=-1, keepdims=True)
    return zc * lax.rsqrt(var + LN_EPS) * g + b


def _silu(x):
    return x * jax.nn.sigmoid(x)


def _log_sigmoid(x):
    return jnp.minimum(x, 0.0) - jnp.log1p(jnp.exp(-jnp.abs(x)))


def _swiglu_half_step(x, wg_ref, wu_ref, wd_ref, alpha):
    xb = x.astype(BF16)
    fc = D_FF // FF_SPLIT
    y = None
    for c in range(FF_SPLIT):
        cols = slice(c * fc, (c + 1) * fc)
        g = _dot(xb, wg_ref[:, cols])
        u = _dot(xb, wu_ref[:, cols])
        act = (_silu(g) * u).astype(BF16)
        part = _dot(act, wd_ref[cols, :])
        y = part if y is None else y + part
    return alpha * x + 0.5 * y


def _ffn_inproj_rows(alpha, x, is_meta, wg_ref, wu_ref, wd_ref, lng_ref, lnb_ref,
                     wtm_ref, wgu_ref, bg_ref, wcm_ref, wgt_ref, wgut_ref, bgc_ref,
                     h_ref, u_ref, k_ref, la_ref, qt_ref, kt_ref, lat_ref, vt_ref, rt_ref, flag_ref):
    rows = x.shape[0]
    h = _layer_norm(_swiglu_half_step(x, wg_ref, wu_ref, wd_ref, alpha), lng_ref[...], lnb_ref[...])
    h_ref[0:rows, :] = h
    hb = h.astype(BF16)

    if is_meta:
        pad = GLA_CHUNK - N_META
        keep_r = (lax.broadcasted_iota(jnp.int32, (rows, 1), 0) % GLA_CHUNK) >= pad
        keep_l = (lax.broadcasted_iota(jnp.int32, (1, rows), 1) % GLA_CHUNK) >= pad
        mask_r = lambda v: jnp.where(keep_r, v, 0.0)
        mask_l = lambda v: jnp.where(keep_l, v, 0.0)
    else:
        mask_r = mask_l = lambda v: v

    ztm = _dot(hb, wtm_ref[...])
    u_ref[0:rows, :] = mask_r(ztm[:, :POOL_WIDTH])
    k_ref[0:rows, :] = mask_r(ztm[:, POOL_WIDTH:POOL_WIDTH + GLA_KEY_WIDTH])
    glr = ztm[:, POOL_WIDTH + GLA_KEY_WIDTH:].astype(BF16)
    la = mask_r(_log_sigmoid(_dot(glr, wgu_ref[...]) + bg_ref[...]) * (1.0 / GLA_GATE_TEMP))
    la_ref[0:rows, :] = la

    zt = _dot_nt(wcm_ref[...], hb)
    kw = GLA_KEY_WIDTH
    qt_ref[:, 0:rows] = mask_l(zt[:kw])
    kt_ref[:, 0:rows] = mask_l(zt[kw:2 * kw])
    vt_ref[:, 0:rows] = mask_l(zt[2 * kw:2 * kw + GLA_WIDTH]).astype(BF16)
    rt_ref[:, 0:rows] = _silu(zt[2 * kw + GLA_WIDTH:])
    gt = _dot_nt(wgt_ref[...], hb).astype(BF16)
    lat_ref[:, 0:rows] = mask_l(_log_sigmoid(_dot(wgut_ref[...], gt) + bgc_ref[...]) * (1.0 / GLA_GATE_TEMP))

    frow = lax.broadcasted_iota(jnp.int32, (FLAG_ROWS, LANES), 0)
    flags = jnp.zeros((FLAG_ROWS, LANES), jnp.int32)
    for s in range(rows // SC_ROWS):
        tot = jnp.minimum(jnp.sum(la[SC_ROWS * s:SC_ROWS * s + GLA_CHUNK], axis=0, keepdims=True),
                          jnp.sum(la[SC_ROWS * s + GLA_CHUNK:SC_ROWS * (s + 1)], axis=0, keepdims=True))
        slow = (jnp.min(tot, axis=1, keepdims=True) < -FAST_PATH_MAX_DECAY).astype(jnp.int32)
        flags = jnp.where(frow == s, slow, flags)
    flag_ref[...] = flags


def _ffn_inproj_kernel(alpha, from_x, x_ref, *refs):
    if from_x:
        meta_ref, *refs = refs
    last = pl.num_programs(0) - 1

    @pl.when(pl.program_id(0) < last)
    def _():
        if from_x:
            x = jnp.concatenate([x_ref[b, s] for s in range(x_ref.shape[1]) for b in range(BATCH)], axis=0)
        else:
            x = x_ref[...]
        _ffn_inproj_rows(alpha, x, False, *refs)

    @pl.when(pl.program_id(0) == last)
    def _():
        x = meta_ref[...] if from_x else x_ref[0:SC_ROWS, :]
        _ffn_inproj_rows(alpha, x, True, *refs)


def _split3(x):
    hi = x.astype(BF16)
    r1 = x - hi.astype(F32)
    mid = r1.astype(BF16)
    lo = (r1 - mid.astype(F32)).astype(BF16)
    return hi, mid, lo


def _mixer_kernel(n_ch, flags_ref,
                  u_ref, k_ref, la_ref, qt_ref, kt_ref, lat_ref, vt_ref, rt_ref,
                  lbd_ref, ubd_ref, uprev_ref, slot_ref, dmask_ref, causal_ref, eye_ref, gn_ref, cnt_ref,
                  wpool_ref, pscale_ref,
                  y_ref,
                  st_ref, ext_ref, b_scr, sd_scr, at_scr):
    step = pl.program_id(0)
    C = GLA_CHUNK
    slow = flags_ref[lax.rem(step + n_ch, n_ch + 1)]

    @pl.when(step == 0)
    def _():
        st_ref[...] = jnp.zeros(st_ref.shape, F32)
        ext_ref[:, 0:SUB, :] = jnp.zeros((BATCH, SUB, POOL_WIDTH), F32)

    u = u_ref[...]
    for b in range(BATCH):
        ext_ref[b, SUB:SUB + C, :] = u[C * b:C * (b + 1), :]
    cnt = cnt_ref[...]
    parts = []
    for g, w in enumerate(POOL_WINDOWS):
        cols = slice(POOL_GROUP_DIM * g, POOL_GROUP_DIM * (g + 1))
        sums = []
        for b in range(BATCH):
            s = ext_ref[b, SUB:SUB + C, cols]
            for back in range(1, w):
                s = s + ext_ref[b, SUB - back:SUB + C - back, cols]
            sums.append(s)
        s = jnp.concatenate(sums, axis=0)
        p = s / cnt[:, cols] - u[:, cols]
        parts.append(_dot(p.astype(BF16), wpool_ref[g]))
    y_pool = jnp.concatenate(parts, axis=1) * pscale_ref[...]
    for b in range(BATCH):
        ext_ref[b, 0:SUB, :] = ext_ref[b, C:C + SUB, :]

    la_parts = _split3(la_ref[...])
    lbd = lbd_ref[...]
    b_tm = _dot(lbd, la_parts[0]) + _dot(lbd, la_parts[1]) + _dot(lbd, la_parts[2])
    lat_parts = _split3(lat_ref[...])
    ubd = ubd_ref[...]
    bT = _dot(lat_parts[0], ubd) + _dot(lat_parts[1], ubd) + _dot(lat_parts[2], ubd)

    b_scr[...] = b_tm
    ends = [[b_scr[pl.ds(C * b + SUB * j + SUB - 1, 1), :] for j in range(N_SUB)] for b in range(BATCH)]

    def per_block(fn):
        return jnp.concatenate(
            [jnp.broadcast_to(fn(b, j), (SUB, GLA_KEY_WIDTH)) for b in range(BATCH) for j in range(N_SUB)], axis=0)

    k_tm = k_ref[...]
    qT = qt_ref[...]
    e_last = per_block(lambda b, j: ends[b][N_SUB - 1])
    ktil = k_tm * jnp.exp(e_last - b_tm)
    qeT = qT * jnp.exp(bT)
    row_head = lax.broadcasted_iota(jnp.int32, (GLA_KEY_WIDTH, SC_ROWS), 0) // GLA_DK

    def heads_on_lanes(xt):
        return jnp.concatenate([jnp.where(row_head == h, xt, 0.0) for h in range(GLA_HEADS)], axis=1).astype(BF16)

    lane = lax.broadcasted_iota(jnp.int32, (SC_ROWS, SC_ROWS), 1)

    @pl.when(slow == 0)
    def _():
        kneg = (k_tm * jnp.exp(-b_tm)).astype(BF16)
        at_scr[...] = jnp.where(causal_ref[...] != 0, _dot(kneg, heads_on_lanes(qeT)), 0.0)

    @pl.when(slow != 0)
    def _():
        lat_p = _split3(lat_ref[...])
        uprev = uprev_ref[...]
        cprevT = _dot(lat_p[0], uprev) + _dot(lat_p[1], uprev) + _dot(lat_p[2], uprev)
        e_own = per_block(lambda b, j: ends[b][j])
        f2 = per_block(lambda b, j: jnp.exp(ends[b][min(j + 1, N_SUB - 1)] - ends[b][j]))
        f3 = per_block(lambda b, j: jnp.exp(ends[b][min(j + 2, N_SUB - 1)] - ends[b][j]))
        khat = k_tm * jnp.exp(e_own - b_tm)
        kslots = jnp.concatenate([khat, khat * f2, khat * f3], axis=0).astype(BF16)
        r_all = _dot(kslots, heads_on_lanes(qT * jnp.exp(bT - cprevT)))
        slot = slot_ref[...]
        at_off = jnp.where(slot == 1, r_all[0:SC_ROWS],
                           jnp.where(slot == 2, r_all[SC_ROWS:2 * SC_ROWS],
                                     jnp.where(slot == 3, r_all[2 * SC_ROWS:], 0.0)))

        kT = kt_ref[...]
        for dist in range(SUB):
            if dist == 0:
                qs, bs = qT, bT
            else:
                qs = pltpu.roll(qT, SC_ROWS - dist, axis=1)
                bs = pltpu.roll(bT, SC_ROWS - dist, axis=1)
            prod = qs * kT * jnp.exp(jnp.minimum(bs - bT, 0.0))
            for h in range(GLA_HEADS):
                sd_scr[pl.ds(SUB * h + dist, 1), :] = jnp.sum(prod[GLA_DK * h:GLA_DK * (h + 1)], axis=0, keepdims=True)
        sd = jnp.concatenate([sd_scr[...], jnp.zeros((SC_ROWS - GLA_HEADS * SUB, SC_ROWS), F32)], axis=0)
        sdt = sd.T
        dmask = dmask_ref[...]
        for h in range(GLA_HEADS):
            xh = jnp.where((lane >= SUB * h) & (lane < SUB * (h + 1)), sdt, 0.0)
            skew = pltpu.roll(xh, (SC_ROWS - SUB * h) % SC_ROWS, axis=1, stride=1, stride_axis=0)
            at_scr[:, SC_ROWS * h:SC_ROWS * (h + 1)] = (
                jnp.where(dmask != 0, skew, 0.0) + at_off[:, SC_ROWS * h:SC_ROWS * (h + 1)])

    rowi = lax.broadcasted_iota(jnp.int32, (SC_ROWS, SC_ROWS), 0)
    lane64 = lax.broadcasted_iota(jnp.int32, (C, SC_ROWS), 1)
    lane8 = lax.broadcasted_iota(jnp.int32, (8, SC_ROWS), 1)
    lane_lo = lane < C
    row_lo = rowi < C
    eye = eye_ref[...]
    vT = vt_ref[...]
    rT = rt_ref[...]
    gn = gn_ref[...]
    y_heads = []
    for h in range(GLA_HEADS):
        at_h = at_scr[:, SC_ROWS * h:SC_ROWS * (h + 1)]

        half = (h % 2) * C
        kt_cols = ktil[:, LANES * (h // 2):LANES * (h // 2 + 1)]
        kt_roll = pltpu.roll(kt_cols, C, axis=1)
        lo_src, hi_src = (kt_cols, kt_roll) if half == 0 else (kt_roll, kt_cols)
        kbd = jnp.where(row_lo & lane_lo, lo_src, jnp.where((~row_lo) & (~lane_lo), hi_src, 0.0))
        qe_h = qeT[GLA_DK * h:GLA_DK * (h + 1)]
        qebd = jnp.concatenate([jnp.where(lane64 < C, qe_h, 0.0), jnp.where(lane64 >= C, qe_h, 0.0)], axis=0)

        v_h = vT[GLA_DV * h:GLA_DV * (h + 1)].astype(BF16)
        res = _dot(v_h, jnp.concatenate([at_h, kbd], axis=1).astype(BF16))
        st = st_ref[h]
        o_t = res[:, :SC_ROWS] + _dot(st.astype(BF16), qebd.astype(BF16))

        e0 = jnp.broadcast_to(ends[0][N_SUB - 1][:, LANES * (h // 2):LANES * (h // 2 + 1)], (8, LANES))
        e1 = jnp.broadcast_to(ends[1][N_SUB - 1][:, LANES * (h // 2):LANES * (h // 2 + 1)], (8, LANES))
        if half == 0:
            e1 = pltpu.roll(e1, C, axis=1)
        else:
            e0 = pltpu.roll(e0, C, axis=1)
        dec = jnp.exp(jnp.where(lane8 < C, e0, e1))
        st_ref[h] = st * jnp.broadcast_to(dec[0:1], (GLA_DV, SC_ROWS)) + res[:, SC_ROWS:]

        ms = jnp.mean(o_t * o_t, axis=0, keepdims=True)
        gsl = slice(GLA_DV * h, GLA_DV * (h + 1))
        y_heads.append(o_t * lax.rsqrt(ms + RMS_EPS) * gn[gsl] * rT[gsl])
    y_t = jnp.concatenate(y_heads, axis=0).astype(BF16)
    y_gla = _dot_nt(eye, y_t)

    y_ref[...] = jnp.concatenate([y_pool, y_gla], axis=1).astype(BF16)


def _outproj_ffn_rows(alpha, h, y, wo_ref, l2g_ref, l2b_ref, wg_ref, wu_ref, wd_ref, l3g_ref, l3b_ref):
    h2 = _layer_norm(alpha * h + _dot(y, wo_ref[...]), l2g_ref[...], l2b_ref[...])
    return _layer_norm(_swiglu_half_step(h2, wg_ref, wu_ref, wd_ref, alpha), l3g_ref[...], l3b_ref[...])


def _outproj_ffn_kernel(alpha, to_out, h_ref, y_ref, *refs):
    *w_refs, o_ref = refs
    if to_out:
        o = _outproj_ffn_rows(alpha, h_ref[...], y_ref[...], *w_refs)
        for s in range(o_ref.shape[1]):
            for b in range(BATCH):
                r0 = (s * BATCH + b) * GLA_CHUNK
                o_ref[b, s] = o[r0:r0 + GLA_CHUNK]
        return
    last = pl.num_programs(0) - 1

    @pl.when(pl.program_id(0) < last)
    def _():
        o_ref[...] = _outproj_ffn_rows(alpha, h_ref[...], y_ref[...], *w_refs)

    @pl.when(pl.program_id(0) == last)
    def _():
        o_ref[0:SC_ROWS, :] = _outproj_ffn_rows(alpha, h_ref[0:SC_ROWS, :], y_ref[0:SC_ROWS, :], *w_refs)


def _resident(shape):
    nd = len(shape)
    return pl.BlockSpec(shape, lambda i: (0,) * nd, pipeline_mode=pl.Buffered(1))


def _mixer_constants():
    r = np.arange(SC_ROWS)
    b, t = r // GLA_CHUNK, r % GLA_CHUNK
    same_b = b[:, None] == b[None, :]
    lbd = same_b & (t[None, :] <= t[:, None])
    ubd = lbd.T
    blk = t // SUB
    uprev = same_b & (blk[:, None] < blk[None, :])
    dist = np.where(same_b, blk[None, :] - blk[:, None], 0)
    slot = np.where((dist >= 1) & (dist < N_SUB), dist, 0).astype(np.int32)
    slot = np.tile(slot, (1, GLA_HEADS))
    dmask = (same_b & (blk[:, None] == blk[None, :]) & (t[None, :] >= t[:, None])).astype(np.int32)
    causal = np.tile((same_b & (t[None, :] >= t[:, None])).astype(np.int32), (1, GLA_HEADS))
    eye = np.eye(SC_ROWS)
    w_lane = np.repeat(np.array(POOL_WINDOWS, np.float32), POOL_GROUP_DIM)[None, :]
    t_meta = np.maximum(t - (GLA_CHUNK - N_META), 0).astype(np.float32)[:, None]
    cnt = np.stack([np.broadcast_to(w_lane, (SC_ROWS, POOL_WIDTH)), np.minimum(t_meta + 1.0, w_lane)])
    return (jnp.asarray(lbd, BF16), jnp.asarray(ubd, BF16), jnp.asarray(uprev, BF16), jnp.asarray(slot),
            jnp.asarray(dmask), jnp.asarray(causal), jnp.asarray(eye, BF16), jnp.asarray(cnt, F32))


def _tile_chunks(n_ch):
    return next(k for k in (4, 2, 1) if n_ch % k == 0)


def kernel(x, meta_tokens, ffn1_w_gate, ffn1_w_up, ffn1_w_down, ln1_g, ln1_b, w_in, w_gate_up, b_gate, w_pool, pool_scale, gla_norm_g, w_out, ln2_g, ln2_b, ffn2_w_gate, ffn2_w_up, ffn2_w_down, ln3_g, ln3_b):
    batch, seq, d = x.shape
    depth = w_in.shape[0]
    assert batch == BATCH and d == D_MODEL and seq % GLA_CHUNK == 0
    n_ch = seq // GLA_CHUNK
    n_sc = n_ch + 1
    n_rows = n_sc * SC_ROWS
    tc = _tile_chunks(n_ch)
    tr = tc * SC_ROWS
    n_main = n_ch // tc
    alpha = (2.0 * depth) ** 0.25

    x4 = x.reshape(batch, n_ch, GLA_CHUNK, d)
    meta_half = jnp.concatenate([jnp.zeros((GLA_CHUNK - N_META, d), x.dtype), meta_tokens.astype(x.dtype)], axis=0)
    meta_sc = jnp.concatenate([meta_half, meta_half], axis=0)
    h = None

    lbd, ubd, uprev, slot, dmask, causal, eye, cnt = _mixer_constants()
    s0 = POOL_WIDTH
    s1 = s0 + GLA_KEY_WIDTH
    s2 = s1 + GLA_KEY_WIDTH
    s3 = s2 + GLA_WIDTH
    s4 = s3 + GLA_WIDTH
    cparams = pltpu.CompilerParams(dimension_semantics=("arbitrary",), vmem_limit_bytes=VMEM_LIMIT_BYTES)
    row_tile = lambda cols: pl.BlockSpec((tr, cols), lambda i: (i, 0))
    col_tile = lambda rows: pl.BlockSpec((rows, tr), lambda i: (0, i))
    x4_tile = pl.BlockSpec((batch, tc, GLA_CHUNK, d), lambda i: (0, jnp.minimum(i, n_main - 1), 0, 0))
    sc_of = lambda i: lax.rem(i + n_ch, n_sc)
    sc_rows = lambda cols: pl.BlockSpec((SC_ROWS, cols), lambda i, f: (sc_of(i), 0))
    sc_cols = lambda rows: pl.BlockSpec((rows, SC_ROWS), lambda i, f: (0, sc_of(i)))

    def mix_const(shape):
        nd = len(shape)
        return pl.BlockSpec(shape, lambda i, f: (0,) * nd, pipeline_mode=pl.Buffered(1))

    for l in range(depth):
        wl = w_in[l]
        glr_pad = jnp.zeros((d, LANES - GLA_GATE_RANK), F32)
        wtm = jnp.concatenate([wl[:, :s0], wl[:, s1:s2], wl[:, s4:], glr_pad], axis=1).astype(BF16)
        wgu = jnp.concatenate([w_gate_up[l], jnp.zeros((LANES - GLA_GATE_RANK, GLA_KEY_WIDTH), F32)], axis=0).astype(BF16)
        wcm = jnp.concatenate([wl[:, s0:s1] * (GLA_DK ** -0.5), wl[:, s1:s2], wl[:, s2:s3], wl[:, s3:s4]], axis=1).T.astype(BF16)
        wgt = jnp.concatenate([wl[:, s4:], glr_pad], axis=1).T.astype(BF16)
        wgut = wgu.T
        bg = b_gate[l].reshape(1, GLA_KEY_WIDTH)
        bgc = b_gate[l].reshape(GLA_KEY_WIDTH, 1)
        wg1, wu1, wd1 = ffn1_w_gate[l].astype(BF16), ffn1_w_up[l].astype(BF16), ffn1_w_down[l].astype(BF16)

        from_x = l == 0
        acts = (x4, meta_sc) if from_x else (h,)
        act_specs = [x4_tile, _resident(meta_sc.shape)] if from_x else [row_tile(d)]
        outs = pl.pallas_call(
            functools.partial(_ffn_inproj_kernel, alpha, from_x),
            grid=(n_main + 1,),
            in_specs=act_specs + [_resident(wg1.shape), _resident(wu1.shape), _resident(wd1.shape),
                      _resident((1, d)), _resident((1, d)),
                      _resident(wtm.shape), _resident(wgu.shape), _resident(bg.shape),
                      _resident(wcm.shape), _resident(wgt.shape), _resident(wgut.shape), _resident(bgc.shape)],
            out_specs=[row_tile(d), row_tile(POOL_WIDTH), row_tile(GLA_KEY_WIDTH), row_tile(GLA_KEY_WIDTH),
                       col_tile(GLA_KEY_WIDTH), col_tile(GLA_KEY_WIDTH), col_tile(GLA_KEY_WIDTH),
                       col_tile(GLA_WIDTH), col_tile(GLA_WIDTH),
                       pl.BlockSpec((FLAG_ROWS, LANES), lambda i: (i, 0))],
            out_shape=[jax.ShapeDtypeStruct((n_rows, d), F32),
                       jax.ShapeDtypeStruct((n_rows, POOL_WIDTH), F32),
                       jax.ShapeDtypeStruct((n_rows, GLA_KEY_WIDTH), F32),
                       jax.ShapeDtypeStruct((n_rows, GLA_KEY_WIDTH), F32),
                       jax.ShapeDtypeStruct((GLA_KEY_WIDTH, n_rows), F32),
                       jax.ShapeDtypeStruct((GLA_KEY_WIDTH, n_rows), F32),
                       jax.ShapeDtypeStruct((GLA_KEY_WIDTH, n_rows), F32),
                       jax.ShapeDtypeStruct((GLA_WIDTH, n_rows), BF16),
                       jax.ShapeDtypeStruct((GLA_WIDTH, n_rows), F32),
                       jax.ShapeDtypeStruct(((n_main + 1) * FLAG_ROWS, LANES), jnp.int32)],
            compiler_params=cparams,
            name=f"ffn_inproj_{l}",
        )(*acts, wg1, wu1, wd1, ln1_g[l].reshape(1, d), ln1_b[l].reshape(1, d),
          wtm, wgu, bg, wcm, wgt, wgut, bgc)
        h1, u_tm, k_tm, la_tm, q_t, k_t, la_t, v_t, r_t, tile_flags = outs
        sc_flags = tile_flags.reshape(n_main + 1, FLAG_ROWS, LANES)[:, :tc, 0].reshape(-1)[:n_sc]

        gn = jnp.broadcast_to(gla_norm_g[l].reshape(GLA_WIDTH, 1), (GLA_WIDTH, SC_ROWS))
        y_cat = pl.pallas_call(
            functools.partial(_mixer_kernel, n_ch),
            grid_spec=pltpu.PrefetchScalarGridSpec(
                num_scalar_prefetch=1,
                grid=(n_sc,),
                in_specs=[sc_rows(POOL_WIDTH), sc_rows(GLA_KEY_WIDTH), sc_rows(GLA_KEY_WIDTH),
                          sc_cols(GLA_KEY_WIDTH), sc_cols(GLA_KEY_WIDTH), sc_cols(GLA_KEY_WIDTH),
                          sc_cols(GLA_WIDTH), sc_cols(GLA_WIDTH),
                          mix_const(lbd.shape), mix_const(ubd.shape), mix_const(uprev.shape), mix_const(slot.shape),
                          mix_const(dmask.shape), mix_const(causal.shape), mix_const(eye.shape), mix_const(gn.shape),
                          pl.BlockSpec((None, SC_ROWS, POOL_WIDTH), lambda i, f: (jnp.where(i == 0, 1, 0), 0, 0)),
                          mix_const((POOL_GROUPS, POOL_GROUP_DIM, POOL_GROUP_DIM)), mix_const((1, POOL_WIDTH))],
                out_specs=sc_rows(d),
                scratch_shapes=[pltpu.VMEM((GLA_HEADS, GLA_DV, SC_ROWS), F32),
                                pltpu.VMEM((BATCH, SUB + GLA_CHUNK, POOL_WIDTH), F32),
                                pltpu.VMEM((SC_ROWS, GLA_KEY_WIDTH), F32),
                                pltpu.VMEM((GLA_HEADS * SUB, SC_ROWS), F32),
                                pltpu.VMEM((SC_ROWS, GLA_HEADS * SC_ROWS), F32)]),
            out_shape=jax.ShapeDtypeStruct((n_rows, d), BF16),
            compiler_params=cparams,
            name=f"mixer_{l}",
        )(sc_flags, u_tm, k_tm, la_tm, q_t, k_t, la_t, v_t, r_t, lbd, ubd, uprev, slot, dmask, causal, eye, gn, cnt,
          w_pool[l].astype(BF16), pool_scale[l].reshape(1, POOL_WIDTH))

        wo = w_out[l].astype(BF16)
        wg2, wu2, wd2 = ffn2_w_gate[l].astype(BF16), ffn2_w_up[l].astype(BF16), ffn2_w_down[l].astype(BF16)
        to_out = l == depth - 1
        h = pl.pallas_call(
            functools.partial(_outproj_ffn_kernel, alpha, to_out),
            grid=(n_main if to_out else n_main + 1,),
            in_specs=[row_tile(d), row_tile(d), _resident(wo.shape), _resident((1, d)), _resident((1, d)),
                      _resident(wg2.shape), _resident(wu2.shape), _resident(wd2.shape),
                      _resident((1, d)), _resident((1, d))],
            out_specs=pl.BlockSpec((batch, tc, GLA_CHUNK, d), lambda i: (0, i, 0, 0)) if to_out else row_tile(d),
            out_shape=jax.ShapeDtypeStruct(x4.shape if to_out else (n_rows, d), F32),
            compiler_params=cparams,
            name=f"outproj_ffn_{l}",
        )(h1, y_cat, wo, ln2_g[l].reshape(1, d), ln2_b[l].reshape(1, d),
          wg2, wu2, wd2, ln3_g[l].reshape(1, d), ln3_b[l].reshape(1, d))

    return h.reshape(batch, seq, d)
```

```python
import functools

import numpy as np
import jax
import jax.numpy as jnp
from jax import lax
from jax.experimental import pallas as pl
from jax.experimental.pallas import tpu as pltpu

D_MODEL = 1024
N_META = 16
POOL_WIDTH = 512
POOL_GROUPS = 4
POOL_GROUP_DIM = 128
POOL_WINDOWS = (2, 4, 8, 16)
GLA_WIDTH = 512
GLA_HEADS = 4
GLA_KEY_WIDTH = 256
GLA_DK = 64
GLA_DV = 128
GLA_GATE_RANK = 16
GLA_GATE_TEMP = 16.0
GLA_CHUNK = 64
D_FF = 2816
LN_EPS = 1e-5
RMS_EPS = 1e-6

BATCH = 2
SC_ROWS = BATCH * GLA_CHUNK
SUB = 16
N_SUB = GLA_CHUNK // SUB
LANES = 128
FF_SPLIT = 2
VMEM_LIMIT_BYTES = 60 * 1024 * 1024
FAST_PATH_MAX_DECAY = 40.0
FLAG_ROWS = 8

F32 = jnp.float32
BF16 = jnp.bfloat16


def _dot(a, b):
    return jnp.dot(a, b, preferred_element_type=F32)


def _dot_nt(a, b):
    return lax.dot_general(a, b, (((1,), (1,)), ((), ())), preferred_element_type=F32)


def _layer_norm(z, g, b):
    mu = jnp.mean(z, axis=-1, keepdims=True)
    zc = z - mu
    var = jnp.mean(zc * zc, axis=-1, keepdims=True)
    return zc * lax.rsqrt(var + LN_EPS) * g + b


def _silu(x):
    return x * jax.nn.sigmoid(x)


def _log_sigmoid(x):
    return jnp.minimum(x, 0.0) - jnp.log1p(jnp.exp(-jnp.abs(x)))


def _swiglu_half_step(x, wg_ref, wu_ref, wd_ref, alpha):
    xb = x.astype(BF16)
    fc = D_FF // FF_SPLIT
    y = None
    for c in range(FF_SPLIT):
        cols = slice(c * fc, (c + 1) * fc)
        g = _dot(xb, wg_ref[:, cols])
        u = _dot(xb, wu_ref[:, cols])
        act = (_silu(g) * u).astype(BF16)
        part = _dot(act, wd_ref[cols, :])
        y = part if y is None else y + part
    return alpha * x + 0.5 * y


def _ffn_inproj_rows(alpha, x, is_meta, wg_ref, wu_ref, wd_ref, lng_ref, lnb_ref,
                     wtm_ref, wgu_ref, bg_ref, wcm_ref, wgt_ref, wgut_ref, bgc_ref,
                     h_ref, u_ref, k_ref, la_ref, qt_ref, kt_ref, lat_ref, vt_ref, rt_ref, flag_ref):
    rows = x.shape[0]
    h = _layer_norm(_swiglu_half_step(x, wg_ref, wu_ref, wd_ref, alpha), lng_ref[...], lnb_ref[...])
    h_ref[0:rows, :] = h
    hb = h.astype(BF16)

    if is_meta:
        pad = GLA_CHUNK - N_META
        keep_r = (lax.broadcasted_iota(jnp.int32, (rows, 1), 0) % GLA_CHUNK) >= pad
        keep_l = (lax.broadcasted_iota(jnp.int32, (1, rows), 1) % GLA_CHUNK) >= pad
        mask_r = lambda v: jnp.where(keep_r, v, 0.0)
        mask_l = lambda v: jnp.where(keep_l, v, 0.0)
    else:
        mask_r = mask_l = lambda v: v

    ztm = _dot(hb, wtm_ref[...])
    u_ref[0:rows, :] = mask_r(ztm[:, :POOL_WIDTH])
    k_ref[0:rows, :] = mask_r(ztm[:, POOL_WIDTH:POOL_WIDTH + GLA_KEY_WIDTH])
    glr = ztm[:, POOL_WIDTH + GLA_KEY_WIDTH:].astype(BF16)
    la = mask_r(_log_sigmoid(_dot(glr, wgu_ref[...]) + bg_ref[...]) * (1.0 / GLA_GATE_TEMP))
    la_ref[0:rows, :] = la

    zt = _dot_nt(wcm_ref[...], hb)
    kw = GLA_KEY_WIDTH
    qt_ref[:, 0:rows] = mask_l(zt[:kw])
    kt_ref[:, 0:rows] = mask_l(zt[kw:2 * kw])
    vt_ref[:, 0:rows] = mask_l(zt[2 * kw:2 * kw + GLA_WIDTH]).astype(BF16)
    rt_ref[:, 0:rows] = _silu(zt[2 * kw + GLA_WIDTH:])
    gt = _dot_nt(wgt_ref[...], hb).astype(BF16)
    lat_ref[:, 0:rows] = mask_l(_log_sigmoid(_dot(wgut_ref[...], gt) + bgc_ref[...]) * (1.0 / GLA_GATE_TEMP))

    frow = lax.broadcasted_iota(jnp.int32, (FLAG_ROWS, LANES), 0)
    flags = jnp.zeros((FLAG_ROWS, LANES), jnp.int32)
    for s in range(rows // SC_ROWS):
        tot = jnp.minimum(jnp.sum(la[SC_ROWS * s:SC_ROWS * s + GLA_CHUNK], axis=0, keepdims=True),
                          jnp.sum(la[SC_ROWS * s + GLA_CHUNK:SC_ROWS * (s + 1)], axis=0, keepdims=True))
        slow = (jnp.min(tot, axis=1, keepdims=True) < -FAST_PATH_MAX_DECAY).astype(jnp.int32)
        flags = jnp.where(frow == s, slow, flags)
    flag_ref[...] = flags


def _ffn_inproj_kernel(alpha, from_x, x_ref, *refs):
    if from_x:
        meta_ref, *refs = refs
    last = pl.num_programs(0) - 1

    @pl.when(pl.program_id(0) < last)
    def _():
        if from_x:
            x = jnp.concatenate([x_ref[b, s] for s in range(x_ref.shape[1]) for b in range(BATCH)], axis=0)
        else:
            x = x_ref[...]
        _ffn_inproj_rows(alpha, x, False, *refs)

    @pl.when(pl.program_id(0) == last)
    def _():
        x = meta_ref[...] if from_x else x_ref[0:SC_ROWS, :]
        _ffn_inproj_rows(alpha, x, True, *refs)


def _split3(x):
    hi = x.astype(BF16)
    r1 = x - hi.astype(F32)
    mid = r1.astype(BF16)
    lo = (r1 - mid.astype(F32)).astype(BF16)
    return hi, mid, lo


def _dot3_left(c01, x):
    parts = _split3(x)
    return _dot(c01, parts[0]) + _dot(c01, parts[1]) + _dot(c01, parts[2])


def _dot3_right(x, c01):
    parts = _split3(x)
    return _dot(parts[0], c01) + _dot(parts[1], c01) + _dot(parts[2], c01)


def _mix_chunks(n, robust, in_refs, const_refs, y_ref, st_ref, ext_ref, b_scr, sd_scr, at_scr):
    u_ref, k_ref, la_ref, qt_ref, kt_ref, lat_ref, vt_ref, rt_ref = in_refs
    (lbd_ref, ubd_ref, uprev_ref, slot_ref, dmask_ref, causal_ref, eye_ref, gn_ref, cnt_ref,
     wpool_ref, pscale_ref) = const_refs
    C = GLA_CHUNK
    lane = lax.broadcasted_iota(jnp.int32, (SC_ROWS, SC_ROWS), 1)
    rowi = lax.broadcasted_iota(jnp.int32, (SC_ROWS, SC_ROWS), 0)
    lane64 = lax.broadcasted_iota(jnp.int32, (C, SC_ROWS), 1)
    lane8 = lax.broadcasted_iota(jnp.int32, (8, SC_ROWS), 1)
    lane_lo = lane < C
    row_lo = rowi < C
    row_head = lax.broadcasted_iota(jnp.int32, (GLA_KEY_WIDTH, SC_ROWS), 0) // GLA_DK

    def heads_on_lanes(xt):
        return jnp.concatenate([jnp.where(row_head == h, xt, 0.0) for h in range(GLA_HEADS)], axis=1).astype(BF16)

    for c in range(n):
        for b in range(BATCH):
            ext_ref[b, SUB + C * c:SUB + C * (c + 1), :] = u_ref[SC_ROWS * c + C * b:SC_ROWS * c + C * (b + 1), :]
    states = [st_ref[h] for h in range(GLA_HEADS)]
    cnt = cnt_ref[...]
    lbd = lbd_ref[...]
    ubd = ubd_ref[...]
    eye = eye_ref[...]
    gn = gn_ref[...]

    for c in range(n):
        rows = slice(SC_ROWS * c, SC_ROWS * (c + 1))

        u = u_ref[rows, :]
        parts = []
        for g, w in enumerate(POOL_WINDOWS):
            cols = slice(POOL_GROUP_DIM * g, POOL_GROUP_DIM * (g + 1))
            sums = []
            for b in range(BATCH):
                r0 = SUB + C * c
                s = ext_ref[b, r0:r0 + C, cols]
                for back in range(1, w):
                    s = s + ext_ref[b, r0 - back:r0 + C - back, cols]
                sums.append(s)
            s = jnp.concatenate(sums, axis=0)
            p = s / cnt[:, cols] - u[:, cols]
            parts.append(_dot(p.astype(BF16), wpool_ref[g]))
        y_pool = jnp.concatenate(parts, axis=1) * pscale_ref[...]

        b_tm = _dot3_left(lbd, la_ref[rows, :])
        bT = _dot3_right(lat_ref[:, rows], ubd)
        b_scr[c] = b_tm
        ends = [[b_scr[c, pl.ds(C * b + SUB * j + SUB - 1, 1), :] for j in range(N_SUB)] for b in range(BATCH)]

        def per_block(fn):
            return jnp.concatenate(
                [jnp.broadcast_to(fn(b, j), (SUB, GLA_KEY_WIDTH)) for b in range(BATCH) for j in range(N_SUB)], axis=0)

        k_tm = k_ref[rows, :]
        qT = qt_ref[:, rows]
        e_last = per_block(lambda b, j: ends[b][N_SUB - 1])
        ktil = k_tm * jnp.exp(e_last - b_tm)
        qeT = qT * jnp.exp(bT)

        if not robust:
            kneg = (k_tm * jnp.exp(-b_tm)).astype(BF16)
            at_all = jnp.where(causal_ref[...] != 0, _dot(kneg, heads_on_lanes(qeT)), 0.0)
        else:
            cprevT = _dot3_right(lat_ref[:, rows], uprev_ref[...])
            e_own = per_block(lambda b, j: ends[b][j])
            f2 = per_block(lambda b, j: jnp.exp(ends[b][min(j + 1, N_SUB - 1)] - ends[b][j]))
            f3 = per_block(lambda b, j: jnp.exp(ends[b][min(j + 2, N_SUB - 1)] - ends[b][j]))
            khat = k_tm * jnp.exp(e_own - b_tm)
            kslots = jnp.concatenate([khat, khat * f2, khat * f3], axis=0).astype(BF16)
            r_all = _dot(kslots, heads_on_lanes(qT * jnp.exp(bT - cprevT)))
            slot = slot_ref[...]
            at_off = jnp.where(slot == 1, r_all[0:SC_ROWS],
                               jnp.where(slot == 2, r_all[SC_ROWS:2 * SC_ROWS],
                                         jnp.where(slot == 3, r_all[2 * SC_ROWS:], 0.0)))
            kT = kt_ref[:, rows]
            for dist in range(SUB):
                if dist == 0:
                    qs, bs = qT, bT
                else:
                    qs = pltpu.roll(qT, SC_ROWS - dist, axis=1)
                    bs = pltpu.roll(bT, SC_ROWS - dist, axis=1)
                prod = qs * kT * jnp.exp(jnp.minimum(bs - bT, 0.0))
                for h in range(GLA_HEADS):
                    sd_scr[pl.ds(SUB * h + dist, 1), :] = jnp.sum(
                        prod[GLA_DK * h:GLA_DK * (h + 1)], axis=0, keepdims=True)
            sd = jnp.concatenate([sd_scr[...], jnp.zeros((SC_ROWS - GLA_HEADS * SUB, SC_ROWS), F32)], axis=0)
            sdt = sd.T
            dmask = dmask_ref[...]
            for h in range(GLA_HEADS):
                xh = jnp.where((lane >= SUB * h) & (lane < SUB * (h + 1)), sdt, 0.0)
                skew = pltpu.roll(xh, (SC_ROWS - SUB * h) % SC_ROWS, axis=1, stride=1, stride_axis=0)
                at_scr[:, SC_ROWS * h:SC_ROWS * (h + 1)] = (
                    jnp.where(dmask != 0, skew, 0.0) + at_off[:, SC_ROWS * h:SC_ROWS * (h + 1)])
            at_all = at_scr[...]

        vT = vt_ref[:, rows]
        rT = rt_ref[:, rows]
        y_heads = []
        for h in range(GLA_HEADS):
            at_h = at_all[:, SC_ROWS * h:SC_ROWS * (h + 1)]

            half = (h % 2) * C
            kt_cols = ktil[:, LANES * (h // 2):LANES * (h // 2 + 1)]
            kt_roll = pltpu.roll(kt_cols, C, axis=1)
            lo_src, hi_src = (kt_cols, kt_roll) if half == 0 else (kt_roll, kt_cols)
            kbd = jnp.where(row_lo & lane_lo, lo_src, jnp.where((~row_lo) & (~lane_lo), hi_src, 0.0))
            qe_h = qeT[GLA_DK * h:GLA_DK * (h + 1)]
            qebd = jnp.concatenate([jnp.where(lane64 < C, qe_h, 0.0), jnp.where(lane64 >= C, qe_h, 0.0)], axis=0)

            v_h = vT[GLA_DV * h:GLA_DV * (h + 1)]
            res = _dot(v_h, jnp.concatenate([at_h, kbd], axis=1).astype(BF16))
            st = states[h]
            o_t = res[:, :SC_ROWS] + _dot(st.astype(BF16), qebd.astype(BF16))

            e0 = jnp.broadcast_to(ends[0][N_SUB - 1][:, LANES * (h // 2):LANES * (h // 2 + 1)], (8, LANES))
            e1 = jnp.broadcast_to(ends[1][N_SUB - 1][:, LANES * (h // 2):LANES * (h // 2 + 1)], (8, LANES))
            if half == 0:
                e1 = pltpu.roll(e1, C, axis=1)
            else:
                e0 = pltpu.roll(e0, C, axis=1)
            dec = jnp.exp(jnp.where(lane8 < C, e0, e1))
            states[h] = st * jnp.broadcast_to(dec[0:1], (GLA_DV, SC_ROWS)) + res[:, SC_ROWS:]

            ms = jnp.mean(o_t * o_t, axis=0, keepdims=True)
            gsl = slice(GLA_DV * h, GLA_DV * (h + 1))
            y_heads.append(o_t * lax.rsqrt(ms + RMS_EPS) * gn[gsl] * rT[gsl])
        y_t = jnp.concatenate(y_heads, axis=0).astype(BF16)
        y_gla = _dot_nt(eye, y_t)
        y_ref[rows, :] = jnp.concatenate([y_pool, y_gla], axis=1).astype(BF16)

    for h in range(GLA_HEADS):
        st_ref[h] = states[h]
    for b in range(BATCH):
        ext_ref[b, 0:SUB, :] = ext_ref[b, C * n:C * n + SUB, :]


def _mixer_kernel(tc, flags_ref, *refs):
    in_refs, const_refs, y_ref, scratch = refs[:8], refs[8:19], refs[19], refs[20:]
    st_ref, ext_ref = scratch[0], scratch[1]
    step = pl.program_id(0)

    @pl.when(step == 0)
    def _():
        st_ref[...] = jnp.zeros(st_ref.shape, F32)
        ext_ref[:, 0:SUB, :] = jnp.zeros((BATCH, SUB, POOL_WIDTH), F32)
        _mix_chunks(1, True, in_refs, const_refs, y_ref, *scratch)

    @pl.when(step > 0)
    def _():
        base = (step - 1) * tc
        slow = flags_ref[base]
        for c in range(1, tc):
            slow = jnp.maximum(slow, flags_ref[base + c])

        @pl.when(slow == 0)
        def _():
            _mix_chunks(tc, False, in_refs, const_refs, y_ref, *scratch)

        @pl.when(slow != 0)
        def _():
            _mix_chunks(tc, True, in_refs, const_refs, y_ref, *scratch)


def _outproj_ffn_rows(alpha, h, y, wo_ref, l2g_ref, l2b_ref, wg_ref, wu_ref, wd_ref, l3g_ref, l3b_ref):
    h2 = _layer_norm(alpha * h + _dot(y, wo_ref[...]), l2g_ref[...], l2b_ref[...])
    return _layer_norm(_swiglu_half_step(h2, wg_ref, wu_ref, wd_ref, alpha), l3g_ref[...], l3b_ref[...])


def _outproj_ffn_kernel(alpha, to_out, h_ref, y_ref, *refs):
    *w_refs, o_ref = refs
    if to_out:
        o = _outproj_ffn_rows(alpha, h_ref[...], y_ref[...], *w_refs)
        for s in range(o_ref.shape[1]):
            for b in range(BATCH):
                r0 = (s * BATCH + b) * GLA_CHUNK
                o_ref[b, s] = o[r0:r0 + GLA_CHUNK]
        return
    last = pl.num_programs(0) - 1

    @pl.when(pl.program_id(0) < last)
    def _():
        o_ref[...] = _outproj_ffn_rows(alpha, h_ref[...], y_ref[...], *w_refs)

    @pl.when(pl.program_id(0) == last)
    def _():
        o_ref[0:SC_ROWS, :] = _outproj_ffn_rows(alpha, h_ref[0:SC_ROWS, :], y_ref[0:SC_ROWS, :], *w_refs)


def _resident(shape):
    nd = len(shape)
    return pl.BlockSpec(shape, lambda *_: (0,) * nd, pipeline_mode=pl.Buffered(1))


def _layer_resident(shape, layer):
    nd = len(shape) - 1
    return pl.BlockSpec((None,) + tuple(shape[1:]), lambda *_: (layer,) + (0,) * nd, pipeline_mode=pl.Buffered(1))


def _mixer_constants():
    r = np.arange(SC_ROWS)
    b, t = r // GLA_CHUNK, r % GLA_CHUNK
    same_b = b[:, None] == b[None, :]
    lbd = same_b & (t[None, :] <= t[:, None])
    ubd = lbd.T
    blk = t // SUB
    uprev = same_b & (blk[:, None] < blk[None, :])
    dist = np.where(same_b, blk[None, :] - blk[:, None], 0)
    slot = np.where((dist >= 1) & (dist < N_SUB), dist, 0).astype(np.int32)
    slot = np.tile(slot, (1, GLA_HEADS))
    dmask = (same_b & (blk[:, None] == blk[None, :]) & (t[None, :] >= t[:, None])).astype(np.int32)
    causal = np.tile((same_b & (t[None, :] >= t[:, None])).astype(np.int32), (1, GLA_HEADS))
    eye = np.eye(SC_ROWS)
    w_lane = np.repeat(np.array(POOL_WINDOWS, np.float32), POOL_GROUP_DIM)[None, :]
    t_meta = np.maximum(t - (GLA_CHUNK - N_META), 0).astype(np.float32)[:, None]
    cnt = np.stack([np.broadcast_to(w_lane, (SC_ROWS, POOL_WIDTH)), np.minimum(t_meta + 1.0, w_lane)])
    return (jnp.asarray(lbd, BF16), jnp.asarray(ubd, BF16), jnp.asarray(uprev, BF16), jnp.asarray(slot),
            jnp.asarray(dmask), jnp.asarray(causal), jnp.asarray(eye, BF16), jnp.asarray(cnt, F32))


def _tile_chunks(n_ch):
    return next(k for k in (4, 2, 1) if n_ch % k == 0)


def kernel(x, meta_tokens, ffn1_w_gate, ffn1_w_up, ffn1_w_down, ln1_g, ln1_b, w_in, w_gate_up, b_gate, w_pool, pool_scale, gla_norm_g, w_out, ln2_g, ln2_b, ffn2_w_gate, ffn2_w_up, ffn2_w_down, ln3_g, ln3_b):
    batch, seq, d = x.shape
    depth = w_in.shape[0]
    assert batch == BATCH and d == D_MODEL and seq % GLA_CHUNK == 0
    n_ch = seq // GLA_CHUNK
    n_sc = n_ch + 1
    n_rows = n_sc * SC_ROWS
    tc = _tile_chunks(n_ch)
    tr = tc * SC_ROWS
    n_main = n_ch // tc
    alpha = (2.0 * depth) ** 0.25

    x4 = x.reshape(batch, n_ch, GLA_CHUNK, d)
    meta_half = jnp.concatenate([jnp.zeros((GLA_CHUNK - N_META, d), x.dtype), meta_tokens.astype(x.dtype)], axis=0)
    meta_sc = jnp.concatenate([meta_half, meta_half], axis=0)
    h = None

    lbd, ubd, uprev, slot, dmask, causal, eye, cnt = _mixer_constants()
    s0 = POOL_WIDTH
    s1 = s0 + GLA_KEY_WIDTH
    s2 = s1 + GLA_KEY_WIDTH
    s3 = s2 + GLA_WIDTH
    s4 = s3 + GLA_WIDTH
    cparams = pltpu.CompilerParams(dimension_semantics=("arbitrary",), vmem_limit_bytes=VMEM_LIMIT_BYTES)
    row_tile = lambda cols: pl.BlockSpec((tr, cols), lambda i: (i, 0))
    col_tile = lambda rows: pl.BlockSpec((rows, tr), lambda i: (0, i))
    x4_tile = pl.BlockSpec((batch, tc, GLA_CHUNK, d), lambda i: (0, jnp.minimum(i, n_main - 1), 0, 0))
    tile_of = lambda i: lax.rem(i + n_main, n_main + 1)
    mix_rows = lambda cols: pl.BlockSpec((tr, cols), lambda i, f: (tile_of(i), 0))
    mix_cols = lambda rows: pl.BlockSpec((rows, tr), lambda i, f: (0, tile_of(i)))

    bf = lambda w: w.astype(BF16)
    w1g, w1u, w1d = bf(ffn1_w_gate), bf(ffn1_w_up), bf(ffn1_w_down)
    w2g, w2u, w2d = bf(ffn2_w_gate), bf(ffn2_w_up), bf(ffn2_w_down)
    wo_all, wpool_all = bf(w_out), bf(w_pool)
    row3 = lambda p: p.reshape(depth, 1, p.shape[-1])
    l1g, l1b, l2g, l2b, l3g, l3b = (row3(p) for p in (ln1_g, ln1_b, ln2_g, ln2_b, ln3_g, ln3_b))
    pscale_all = row3(pool_scale)

    for l in range(depth):
        wl = w_in[l]
        glr_pad = jnp.zeros((d, LANES - GLA_GATE_RANK), F32)
        wtm = jnp.concatenate([wl[:, :s0], wl[:, s1:s2], wl[:, s4:], glr_pad], axis=1).astype(BF16)
        wgu = jnp.concatenate([w_gate_up[l], jnp.zeros((LANES - GLA_GATE_RANK, GLA_KEY_WIDTH), F32)], axis=0).astype(BF16)
        wcm = jnp.concatenate([wl[:, s0:s1] * (GLA_DK ** -0.5), wl[:, s1:s2], wl[:, s2:s3], wl[:, s3:s4]], axis=1).T.astype(BF16)
        wgt = jnp.concatenate([wl[:, s4:], glr_pad], axis=1).T.astype(BF16)
        wgut = wgu.T
        bg = b_gate[l].reshape(1, GLA_KEY_WIDTH)
        bgc = b_gate[l].reshape(GLA_KEY_WIDTH, 1)

        from_x = l == 0
        acts = (x4, meta_sc) if from_x else (h,)
        act_specs = [x4_tile, _resident(meta_sc.shape)] if from_x else [row_tile(d)]
        outs = pl.pallas_call(
            functools.partial(_ffn_inproj_kernel, alpha, from_x),
            grid=(n_main + 1,),
            in_specs=act_specs + [_layer_resident(w1g.shape, l), _layer_resident(w1u.shape, l),
                                  _layer_resident(w1d.shape, l), _layer_resident(l1g.shape, l),
                                  _layer_resident(l1b.shape, l),
                                  _resident(wtm.shape), _resident(wgu.shape), _resident(bg.shape),
                                  _resident(wcm.shape), _resident(wgt.shape), _resident(wgut.shape),
                                  _resident(bgc.shape)],
            out_specs=[row_tile(d), row_tile(POOL_WIDTH), row_tile(GLA_KEY_WIDTH), row_tile(GLA_KEY_WIDTH),
                       col_tile(GLA_KEY_WIDTH), col_tile(GLA_KEY_WIDTH), col_tile(GLA_KEY_WIDTH),
                       col_tile(GLA_WIDTH), col_tile(GLA_WIDTH),
                       pl.BlockSpec((FLAG_ROWS, LANES), lambda i: (i, 0))],
            out_shape=[jax.ShapeDtypeStruct((n_rows, d), F32),
                       jax.ShapeDtypeStruct((n_rows, POOL_WIDTH), F32),
                       jax.ShapeDtypeStruct((n_rows, GLA_KEY_WIDTH), F32),
                       jax.ShapeDtypeStruct((n_rows, GLA_KEY_WIDTH), F32),
                       jax.ShapeDtypeStruct((GLA_KEY_WIDTH, n_rows), F32),
                       jax.ShapeDtypeStruct((GLA_KEY_WIDTH, n_rows), F32),
                       jax.ShapeDtypeStruct((GLA_KEY_WIDTH, n_rows), F32),
                       jax.ShapeDtypeStruct((GLA_WIDTH, n_rows), BF16),
                       jax.ShapeDtypeStruct((GLA_WIDTH, n_rows), F32),
                       jax.ShapeDtypeStruct(((n_main + 1) * FLAG_ROWS, LANES), jnp.int32)],
            compiler_params=cparams,
            name=f"ffn_inproj_{l}",
        )(*acts, w1g, w1u, w1d, l1g, l1b, wtm, wgu, bg, wcm, wgt, wgut, bgc)
        h1, u_tm, k_tm, la_tm, q_t, k_t, la_t, v_t, r_t, tile_flags = outs
        sc_flags = tile_flags.reshape(n_main + 1, FLAG_ROWS, LANES)[:, :tc, 0].reshape(-1)[:n_sc]

        gn = jnp.broadcast_to(gla_norm_g[l].reshape(GLA_WIDTH, 1), (GLA_WIDTH, SC_ROWS))
        y_cat = pl.pallas_call(
            functools.partial(_mixer_kernel, tc),
            grid_spec=pltpu.PrefetchScalarGridSpec(
                num_scalar_prefetch=1,
                grid=(n_main + 1,),
                in_specs=[mix_rows(POOL_WIDTH), mix_rows(GLA_KEY_WIDTH), mix_rows(GLA_KEY_WIDTH),
                          mix_cols(GLA_KEY_WIDTH), mix_cols(GLA_KEY_WIDTH), mix_cols(GLA_KEY_WIDTH),
                          mix_cols(GLA_WIDTH), mix_cols(GLA_WIDTH),
                          _resident(lbd.shape), _resident(ubd.shape), _resident(uprev.shape), _resident(slot.shape),
                          _resident(dmask.shape), _resident(causal.shape), _resident(eye.shape), _resident(gn.shape),
                          pl.BlockSpec((None, SC_ROWS, POOL_WIDTH), lambda i, f: (jnp.where(i == 0, 1, 0), 0, 0)),
                          _layer_resident(wpool_all.shape, l), _layer_resident(pscale_all.shape, l)],
                out_specs=mix_rows(d),
                scratch_shapes=[pltpu.VMEM((GLA_HEADS, GLA_DV, SC_ROWS), F32),
                                pltpu.VMEM((BATCH, SUB + tc * GLA_CHUNK, POOL_WIDTH), F32),
                                pltpu.VMEM((tc, SC_ROWS, GLA_KEY_WIDTH), F32),
                                pltpu.VMEM((GLA_HEADS * SUB, SC_ROWS), F32),
                                pltpu.VMEM((SC_ROWS, GLA_HEADS * SC_ROWS), F32)]),
            out_shape=jax.ShapeDtypeStruct((n_rows, d), BF16),
            compiler_params=cparams,
            name=f"mixer_{l}",
        )(sc_flags, u_tm, k_tm, la_tm, q_t, k_t, la_t, v_t, r_t, lbd, ubd, uprev, slot, dmask, causal, eye, gn, cnt,
          wpool_all, pscale_all)

        to_out = l == depth - 1
        h = pl.pallas_call(
            functools.partial(_outproj_ffn_kernel, alpha, to_out),
            grid=(n_main if to_out else n_main + 1,),
            in_specs=[row_tile(d), row_tile(d), _layer_resident(wo_all.shape, l),
                      _layer_resident(l2g.shape, l), _layer_resident(l2b.shape, l),
                      _layer_resident(w2g.shape, l), _layer_resident(w2u.shape, l), _layer_resident(w2d.shape, l),
                      _layer_resident(l3g.shape, l), _layer_resident(l3b.shape, l)],
            out_specs=pl.BlockSpec((batch, tc, GLA_CHUNK, d), lambda i: (0, i, 0, 0)) if to_out else row_tile(d),
            out_shape=jax.ShapeDtypeStruct(x4.shape if to_out else (n_rows, d), F32),
            compiler_params=cparams,
            name=f"outproj_ffn_{l}",
        )(h1, y_cat, wo_all, l2g, l2b, w2g, w2u, w2d, l3g, l3b)

    return h.reshape(batch, seq, d)
```

```python
import functools

import numpy as np
import jax
import jax.numpy as jnp
from jax import lax
from jax.experimental import pallas as pl
from jax.experimental.pallas import tpu as pltpu

D_MODEL = 1024
N_META = 16
POOL_WIDTH = 512
POOL_GROUPS = 4
POOL_GROUP_DIM = 128
POOL_WINDOWS = (2, 4, 8, 16)
GLA_WIDTH = 512
GLA_HEADS = 4
GLA_KEY_WIDTH = 256
GLA_DK = 64
GLA_DV = 128
GLA_GATE_RANK = 16
GLA_GATE_TEMP = 16.0
GLA_CHUNK = 64
D_FF = 2816
LN_EPS = 1e-5
RMS_EPS = 1e-6

BATCH = 2
SC_ROWS = BATCH * GLA_CHUNK
SUB = 16
N_SUB = GLA_CHUNK // SUB
LANES = 128
FF_SPLIT = 2
ROW_PARTS = 2
STAGGER_LAG = 1
VMEM_LIMIT_BYTES = 60 * 1024 * 1024
FAST_PATH_MAX_DECAY = 40.0
FLAG_ROWS = 8

F32 = jnp.float32
BF16 = jnp.bfloat16


def _dot(a, b):
    return jnp.dot(a, b, preferred_element_type=F32)


def _dot_nt(a, b):
    return lax.dot_general(a, b, (((1,), (1,)), ((), ())), preferred_element_type=F32)


def _layer_norm(z, g, b):
    mu = jnp.mean(z, axis=-1, keepdims=True)
    zc = z - mu
    var = jnp.mean(zc * zc, axis=-1, keepdims=True)
    return zc * lax.rsqrt(var + LN_EPS) * g + b


def _silu(x):
    return x * jax.nn.sigmoid(x)


def _log_sigmoid(x):
    return jnp.minimum(x, 0.0) - jnp.log1p(jnp.exp(-jnp.abs(x)))


def _swiglu_half_step(x, wg_ref, wu_ref, wd_ref, alpha):
    xb = x.astype(BF16)
    fc = D_FF // FF_SPLIT
    y = None
    for c in range(FF_SPLIT):
        cols = slice(c * fc, (c + 1) * fc)
        g = _dot(xb, wg_ref[:, cols])
        u = _dot(xb, wu_ref[:, cols])
        yield
        act = (_silu(g) * u).astype(BF16)
        yield
        part = _dot(act, wd_ref[cols, :])
        y = part if y is None else y + part
    yield
    return alpha * x + 0.5 * y


def _staggered(gens, lag):
    results = [None] * len(gens)
    done = [False] * len(gens)
    t = 0
    while not all(done):
        for k, gen in enumerate(gens):
            if done[k] or t < k * lag:
                continue
            try:
                next(gen)
            except StopIteration as stop:
                results[k], done[k] = stop.value, True
        t += 1
    return results


def _ffn_inproj_rows(alpha, x, r0, is_meta, wg_ref, wu_ref, wd_ref, lng_ref, lnb_ref,
                     wtm_ref, wgu_ref, bg_ref, wcm_ref, wgt_ref, wgut_ref, bgc_ref,
                     h_ref, u_ref, k_ref, la_ref, qt_ref, kt_ref, lat_ref, vt_ref, rt_ref, flag_ref):
    n = x.shape[0]
    rows = slice(r0, r0 + n)
    z = yield from _swiglu_half_step(x, wg_ref, wu_ref, wd_ref, alpha)
    h = _layer_norm(z, lng_ref[...], lnb_ref[...])
    h_ref[rows, :] = h
    hb = h.astype(BF16)
    yield

    if is_meta:
        pad = GLA_CHUNK - N_META
        keep_r = (lax.broadcasted_iota(jnp.int32, (n, 1), 0) % GLA_CHUNK) >= pad
        keep_l = (lax.broadcasted_iota(jnp.int32, (1, n), 1) % GLA_CHUNK) >= pad
        mask_r = lambda v: jnp.where(keep_r, v, 0.0)
        mask_l = lambda v: jnp.where(keep_l, v, 0.0)
    else:
        mask_r = mask_l = lambda v: v

    ztm = _dot(hb, wtm_ref[...])
    u_ref[rows, :] = mask_r(ztm[:, :POOL_WIDTH])
    k_ref[rows, :] = mask_r(ztm[:, POOL_WIDTH:POOL_WIDTH + GLA_KEY_WIDTH])
    glr = ztm[:, POOL_WIDTH + GLA_KEY_WIDTH:].astype(BF16)
    la = mask_r(_log_sigmoid(_dot(glr, wgu_ref[...]) + bg_ref[...]) * (1.0 / GLA_GATE_TEMP))
    la_ref[rows, :] = la
    yield

    zt = _dot_nt(wcm_ref[...], hb)
    kw = GLA_KEY_WIDTH
    qt_ref[:, rows] = mask_l(zt[:kw])
    kt_ref[:, rows] = mask_l(zt[kw:2 * kw])
    vt_ref[:, rows] = mask_l(zt[2 * kw:2 * kw + GLA_WIDTH]).astype(BF16)
    rt_ref[:, rows] = _silu(zt[2 * kw + GLA_WIDTH:])
    gt = _dot_nt(wgt_ref[...], hb).astype(BF16)
    lat_ref[:, rows] = mask_l(_log_sigmoid(_dot(wgut_ref[...], gt) + bgc_ref[...]) * (1.0 / GLA_GATE_TEMP))

    flags = []
    for s in range(n // SC_ROWS):
        tot = jnp.minimum(jnp.sum(la[SC_ROWS * s:SC_ROWS * s + GLA_CHUNK], axis=0, keepdims=True),
                          jnp.sum(la[SC_ROWS * s + GLA_CHUNK:SC_ROWS * (s + 1)], axis=0, keepdims=True))
        flags.append((jnp.min(tot, axis=1, keepdims=True) < -FAST_PATH_MAX_DECAY).astype(jnp.int32))
    return flags


def _ffn_inproj_kernel(alpha, from_x, tile_rows, x_ref, *refs):
    if from_x:
        meta_ref, *refs = refs
    flag_ref = refs[-1]
    last = pl.num_programs(0) - 1

    def x_rows(r0, n):
        if not from_x:
            return x_ref[r0:r0 + n, :]
        pieces = [divmod(r0 // GLA_CHUNK + i, BATCH) for i in range(n // GLA_CHUNK)]
        return jnp.concatenate([x_ref[b, s] for s, b in pieces], axis=0)

    def run(parts, is_meta):
        gens = [_ffn_inproj_rows(alpha, x, r0, is_meta, *refs) for x, r0 in parts]
        sc_flags = [f for fl in _staggered(gens, STAGGER_LAG) for f in fl]
        frow = lax.broadcasted_iota(jnp.int32, (FLAG_ROWS, LANES), 0)
        flags = jnp.zeros((FLAG_ROWS, LANES), jnp.int32)
        for s, f in enumerate(sc_flags):
            flags = jnp.where(frow == s, f, flags)
        flag_ref[...] = flags

    @pl.when(pl.program_id(0) < last)
    def _():
        pr = tile_rows // ROW_PARTS
        run([(x_rows(p * pr, pr), p * pr) for p in range(ROW_PARTS)], False)

    @pl.when(pl.program_id(0) == last)
    def _():
        run([(meta_ref[...] if from_x else x_ref[0:SC_ROWS, :], 0)], True)


def _split3(x):
    hi = x.astype(BF16)
    r1 = x - hi.astype(F32)
    mid = r1.astype(BF16)
    lo = (r1 - mid.astype(F32)).astype(BF16)
    return hi, mid, lo


def _dot3_left(c01, x):
    parts = _split3(x)
    return _dot(c01, parts[0]) + _dot(c01, parts[1]) + _dot(c01, parts[2])


def _dot3_right(x, c01):
    parts = _split3(x)
    return _dot(parts[0], c01) + _dot(parts[1], c01) + _dot(parts[2], c01)


def _mix_chunks(n, robust, in_refs, const_refs, y_ref, st_ref, ext_ref, b_scr, sd_scr, at_scr):
    u_ref, k_ref, la_ref, qt_ref, kt_ref, lat_ref, vt_ref, rt_ref = in_refs
    (lbd_ref, ubd_ref, uprev_ref, slot_ref, dmask_ref, causal_ref, eye_ref, gn_ref, cnt_ref,
     wpool_ref, pscale_ref) = const_refs
    C = GLA_CHUNK
    lane = lax.broadcasted_iota(jnp.int32, (SC_ROWS, SC_ROWS), 1)
    rowi = lax.broadcasted_iota(jnp.int32, (SC_ROWS, SC_ROWS), 0)
    lane64 = lax.broadcasted_iota(jnp.int32, (C, SC_ROWS), 1)
    lane8 = lax.broadcasted_iota(jnp.int32, (8, SC_ROWS), 1)
    lane_lo = lane < C
    row_lo = rowi < C
    row_head = lax.broadcasted_iota(jnp.int32, (GLA_KEY_WIDTH, SC_ROWS), 0) // GLA_DK

    def heads_on_lanes(xt):
        return jnp.concatenate([jnp.where(row_head == h, xt, 0.0) for h in range(GLA_HEADS)], axis=1).astype(BF16)

    for c in range(n):
        for b in range(BATCH):
            ext_ref[b, SUB + C * c:SUB + C * (c + 1), :] = u_ref[SC_ROWS * c + C * b:SC_ROWS * c + C * (b + 1), :]
    states = [st_ref[h] for h in range(GLA_HEADS)]
    cnt = cnt_ref[...]
    lbd = lbd_ref[...]
    ubd = ubd_ref[...]
    eye = eye_ref[...]
    gn = gn_ref[...]

    def chunk(c):
        rows = slice(SC_ROWS * c, SC_ROWS * (c + 1))

        u = u_ref[rows, :]
        parts = []
        for g, w in enumerate(POOL_WINDOWS):
            cols = slice(POOL_GROUP_DIM * g, POOL_GROUP_DIM * (g + 1))
            sums = []
            for b in range(BATCH):
                r0 = SUB + C * c
                s = ext_ref[b, r0:r0 + C, cols]
                for back in range(1, w):
                    s = s + ext_ref[b, r0 - back:r0 + C - back, cols]
                sums.append(s)
            s = jnp.concatenate(sums, axis=0)
            p = s / cnt[:, cols] - u[:, cols]
            parts.append(_dot(p.astype(BF16), wpool_ref[g]))
        y_pool = jnp.concatenate(parts, axis=1) * pscale_ref[...]

        b_tm = _dot3_left(lbd, la_ref[rows, :])
        bT = _dot3_right(lat_ref[:, rows], ubd)
        yield
        b_scr[c] = b_tm
        ends = [[b_scr[c, pl.ds(C * b + SUB * j + SUB - 1, 1), :] for j in range(N_SUB)] for b in range(BATCH)]

        def per_block(fn):
            return jnp.concatenate(
                [jnp.broadcast_to(fn(b, j), (SUB, GLA_KEY_WIDTH)) for b in range(BATCH) for j in range(N_SUB)], axis=0)

        k_tm = k_ref[rows, :]
        qT = qt_ref[:, rows]
        e_last = per_block(lambda b, j: ends[b][N_SUB - 1])
        ktil = k_tm * jnp.exp(e_last - b_tm)
        qeT = qT * jnp.exp(bT)

        if not robust:
            kneg = (k_tm * jnp.exp(-b_tm)).astype(BF16)
            at_all = jnp.where(causal_ref[...] != 0, _dot(kneg, heads_on_lanes(qeT)), 0.0)
        else:
            cprevT = _dot3_right(lat_ref[:, rows], uprev_ref[...])
            e_own = per_block(lambda b, j: ends[b][j])
            f2 = per_block(lambda b, j: jnp.exp(ends[b][min(j + 1, N_SUB - 1)] - ends[b][j]))
            f3 = per_block(lambda b, j: jnp.exp(ends[b][min(j + 2, N_SUB - 1)] - ends[b][j]))
            khat = k_tm * jnp.exp(e_own - b_tm)
            kslots = jnp.concatenate([khat, khat * f2, khat * f3], axis=0).astype(BF16)
            r_all = _dot(kslots, heads_on_lanes(qT * jnp.exp(bT - cprevT)))
            slot = slot_ref[...]
            at_off = jnp.where(slot == 1, r_all[0:SC_ROWS],
                               jnp.where(slot == 2, r_all[SC_ROWS:2 * SC_ROWS],
                                         jnp.where(slot == 3, r_all[2 * SC_ROWS:], 0.0)))
            kT = kt_ref[:, rows]
            for dist in range(SUB):
                if dist == 0:
                    qs, bs = qT, bT
                else:
                    qs = pltpu.roll(qT, SC_ROWS - dist, axis=1)
                    bs = pltpu.roll(bT, SC_ROWS - dist, axis=1)
                prod = qs * kT * jnp.exp(jnp.minimum(bs - bT, 0.0))
                for h in range(GLA_HEADS):
                    sd_scr[pl.ds(SUB * h + dist, 1), :] = jnp.sum(
                        prod[GLA_DK * h:GLA_DK * (h + 1)], axis=0, keepdims=True)
            sd = jnp.concatenate([sd_scr[...], jnp.zeros((SC_ROWS - GLA_HEADS * SUB, SC_ROWS), F32)], axis=0)
            sdt = sd.T
            dmask = dmask_ref[...]
            for h in range(GLA_HEADS):
                xh = jnp.where((lane >= SUB * h) & (lane < SUB * (h + 1)), sdt, 0.0)
                skew = pltpu.roll(xh, (SC_ROWS - SUB * h) % SC_ROWS, axis=1, stride=1, stride_axis=0)
                at_scr[:, SC_ROWS * h:SC_ROWS * (h + 1)] = (
                    jnp.where(dmask != 0, skew, 0.0) + at_off[:, SC_ROWS * h:SC_ROWS * (h + 1)])
            at_all = at_scr[...]
        yield

        vT = vt_ref[:, rows]
        rT = rt_ref[:, rows]
        y_heads = []
        for h in range(GLA_HEADS):
            at_h = at_all[:, SC_ROWS * h:SC_ROWS * (h + 1)]

            half = (h % 2) * C
            kt_cols = ktil[:, LANES * (h // 2):LANES * (h // 2 + 1)]
            kt_roll = pltpu.roll(kt_cols, C, axis=1)
            lo_src, hi_src = (kt_cols, kt_roll) if half == 0 else (kt_roll, kt_cols)
            kbd = jnp.where(row_lo & lane_lo, lo_src, jnp.where((~row_lo) & (~lane_lo), hi_src, 0.0))
            qe_h = qeT[GLA_DK * h:GLA_DK * (h + 1)]
            qebd = jnp.concatenate([jnp.where(lane64 < C, qe_h, 0.0), jnp.where(lane64 >= C, qe_h, 0.0)], axis=0)

            v_h = vT[GLA_DV * h:GLA_DV * (h + 1)]
            res = _dot(v_h, jnp.concatenate([at_h, kbd], axis=1).astype(BF16))
            st = states[h]
            o_t = res[:, :SC_ROWS] + _dot(st.astype(BF16), qebd.astype(BF16))

            e0 = jnp.broadcast_to(ends[0][N_SUB - 1][:, LANES * (h // 2):LANES * (h // 2 + 1)], (8, LANES))
            e1 = jnp.broadcast_to(ends[1][N_SUB - 1][:, LANES * (h // 2):LANES * (h // 2 + 1)], (8, LANES))
            if half == 0:
                e1 = pltpu.roll(e1, C, axis=1)
            else:
                e0 = pltpu.roll(e0, C, axis=1)
            dec = jnp.exp(jnp.where(lane8 < C, e0, e1))
            states[h] = st * jnp.broadcast_to(dec[0:1], (GLA_DV, SC_ROWS)) + res[:, SC_ROWS:]

            ms = jnp.mean(o_t * o_t, axis=0, keepdims=True)
            gsl = slice(GLA_DV * h, GLA_DV * (h + 1))
            y_heads.append(o_t * lax.rsqrt(ms + RMS_EPS) * gn[gsl] * rT[gsl])
        y_t = jnp.concatenate(y_heads, axis=0).astype(BF16)
        y_gla = _dot_nt(eye, y_t)
        yield
        y_ref[rows, :] = jnp.concatenate([y_pool, y_gla], axis=1).astype(BF16)

    if robust:
        for c in range(n):
            _staggered([chunk(c)], STAGGER_LAG)
    else:
        _staggered([chunk(c) for c in range(n)], STAGGER_LAG)

    for h in range(GLA_HEADS):
        st_ref[h] = states[h]
    for b in range(BATCH):
        ext_ref[b, 0:SUB, :] = ext_ref[b, C * n:C * n + SUB, :]


def _mixer_kernel(tc, flags_ref, *refs):
    in_refs, const_refs, y_ref, scratch = refs[:8], refs[8:19], refs[19], refs[20:]
    st_ref, ext_ref = scratch[0], scratch[1]
    step = pl.program_id(0)

    @pl.when(step == 0)
    def _():
        st_ref[...] = jnp.zeros(st_ref.shape, F32)
        ext_ref[:, 0:SUB, :] = jnp.zeros((BATCH, SUB, POOL_WIDTH), F32)
        _mix_chunks(1, True, in_refs, const_refs, y_ref, *scratch)

    @pl.when(step > 0)
    def _():
        base = (step - 1) * tc
        slow = flags_ref[base]
        for c in range(1, tc):
            slow = jnp.maximum(slow, flags_ref[base + c])

        @pl.when(slow == 0)
        def _():
            _mix_chunks(tc, False, in_refs, const_refs, y_ref, *scratch)

        @pl.when(slow != 0)
        def _():
            _mix_chunks(tc, True, in_refs, const_refs, y_ref, *scratch)


def _outproj_ffn_rows(alpha, h, y, store, wo_ref, l2g_ref, l2b_ref, wg_ref, wu_ref, wd_ref, l3g_ref, l3b_ref):
    z = alpha * h + _dot(y, wo_ref[...])
    yield
    h2 = _layer_norm(z, l2g_ref[...], l2b_ref[...])
    yield
    z2 = yield from _swiglu_half_step(h2, wg_ref, wu_ref, wd_ref, alpha)
    store(_layer_norm(z2, l3g_ref[...], l3b_ref[...]))


def _outproj_ffn_kernel(alpha, to_out, h_ref, y_ref, *refs):
    *w_refs, o_ref = refs
    tile_rows = h_ref.shape[0]

    def store_rows(r0):
        def store(o):
            o_ref[r0:r0 + o.shape[0], :] = o
        return store

    def store_chunks(r0):
        def store(o):
            for sb in range(o.shape[0] // GLA_CHUNK):
                s, b = divmod(r0 // GLA_CHUNK + sb, BATCH)
                o_ref[b, s] = o[sb * GLA_CHUNK:(sb + 1) * GLA_CHUNK]
        return store

    def parts(n_rows, n_parts, make_store):
        pr = n_rows // n_parts
        _staggered([_outproj_ffn_rows(alpha, h_ref[p * pr:(p + 1) * pr, :], y_ref[p * pr:(p + 1) * pr, :],
                                      make_store(p * pr), *w_refs) for p in range(n_parts)], STAGGER_LAG)

    if to_out:
        parts(tile_rows, ROW_PARTS, store_chunks)
        return
    last = pl.num_programs(0) - 1

    @pl.when(pl.program_id(0) < last)
    def _():
        parts(tile_rows, ROW_PARTS, store_rows)

    @pl.when(pl.program_id(0) == last)
    def _():
        parts(SC_ROWS, 1, store_rows)


def _resident(shape):
    nd = len(shape)
    return pl.BlockSpec(shape, lambda *_: (0,) * nd, pipeline_mode=pl.Buffered(1))


def _layer_resident(shape, layer):
    nd = len(shape) - 1
    return pl.BlockSpec((None,) + tuple(shape[1:]), lambda *_: (layer,) + (0,) * nd, pipeline_mode=pl.Buffered(1))


def _mixer_constants():
    r = np.arange(SC_ROWS)
    b, t = r // GLA_CHUNK, r % GLA_CHUNK
    same_b = b[:, None] == b[None, :]
    lbd = same_b & (t[None, :] <= t[:, None])
    ubd = lbd.T
    blk = t // SUB
    uprev = same_b & (blk[:, None] < blk[None, :])
    dist = np.where(same_b, blk[None, :] - blk[:, None], 0)
    slot = np.where((dist >= 1) & (dist < N_SUB), dist, 0).astype(np.int32)
    slot = np.tile(slot, (1, GLA_HEADS))
    dmask = (same_b & (blk[:, None] == blk[None, :]) & (t[None, :] >= t[:, None])).astype(np.int32)
    causal = np.tile((same_b & (t[None, :] >= t[:, None])).astype(np.int32), (1, GLA_HEADS))
    eye = np.eye(SC_ROWS)
    w_lane = np.repeat(np.array(POOL_WINDOWS, np.float32), POOL_GROUP_DIM)[None, :]
    t_meta = np.maximum(t - (GLA_CHUNK - N_META), 0).astype(np.float32)[:, None]
    cnt = np.stack([np.broadcast_to(w_lane, (SC_ROWS, POOL_WIDTH)), np.minimum(t_meta + 1.0, w_lane)])
    return (jnp.asarray(lbd, BF16), jnp.asarray(ubd, BF16), jnp.asarray(uprev, BF16), jnp.asarray(slot),
            jnp.asarray(dmask), jnp.asarray(causal), jnp.asarray(eye, BF16), jnp.asarray(cnt, F32))


def _tile_chunks(n_ch):
    return next(k for k in (4, 2, 1) if n_ch % k == 0)


def kernel(x, meta_tokens, ffn1_w_gate, ffn1_w_up, ffn1_w_down, ln1_g, ln1_b, w_in, w_gate_up, b_gate, w_pool, pool_scale, gla_norm_g, w_out, ln2_g, ln2_b, ffn2_w_gate, ffn2_w_up, ffn2_w_down, ln3_g, ln3_b):
    batch, seq, d = x.shape
    depth = w_in.shape[0]
    assert batch == BATCH and d == D_MODEL and seq % GLA_CHUNK == 0
    n_ch = seq // GLA_CHUNK
    n_sc = n_ch + 1
    n_rows = n_sc * SC_ROWS
    tc = _tile_chunks(n_ch)
    tr = tc * SC_ROWS
    n_main = n_ch // tc
    alpha = (2.0 * depth) ** 0.25

    x4 = x.reshape(batch, n_ch, GLA_CHUNK, d)
    meta_half = jnp.concatenate([jnp.zeros((GLA_CHUNK - N_META, d), x.dtype), meta_tokens.astype(x.dtype)], axis=0)
    meta_sc = jnp.concatenate([meta_half, meta_half], axis=0)
    h = None

    lbd, ubd, uprev, slot, dmask, causal, eye, cnt = _mixer_constants()
    s0 = POOL_WIDTH
    s1 = s0 + GLA_KEY_WIDTH
    s2 = s1 + GLA_KEY_WIDTH
    s3 = s2 + GLA_WIDTH
    s4 = s3 + GLA_WIDTH
    cparams = pltpu.CompilerParams(dimension_semantics=("arbitrary",), vmem_limit_bytes=VMEM_LIMIT_BYTES)
    row_tile = lambda cols: pl.BlockSpec((tr, cols), lambda i: (i, 0))
    col_tile = lambda rows: pl.BlockSpec((rows, tr), lambda i: (0, i))
    x4_tile = pl.BlockSpec((batch, tc, GLA_CHUNK, d), lambda i: (0, jnp.minimum(i, n_main - 1), 0, 0))
    tile_of = lambda i: lax.rem(i + n_main, n_main + 1)
    mix_rows = lambda cols: pl.BlockSpec((tr, cols), lambda i, f: (tile_of(i), 0))
    mix_cols = lambda rows: pl.BlockSpec((rows, tr), lambda i, f: (0, tile_of(i)))

    bf = lambda w: w.astype(BF16)
    w1g, w1u, w1d = bf(ffn1_w_gate), bf(ffn1_w_up), bf(ffn1_w_down)
    w2g, w2u, w2d = bf(ffn2_w_gate), bf(ffn2_w_up), bf(ffn2_w_down)
    wo_all, wpool_all = bf(w_out), bf(w_pool)
    row3 = lambda p: p.reshape(depth, 1, p.shape[-1])
    l1g, l1b, l2g, l2b, l3g, l3b = (row3(p) for p in (ln1_g, ln1_b, ln2_g, ln2_b, ln3_g, ln3_b))
    pscale_all = row3(pool_scale)

    for l in range(depth):
        wl = w_in[l]
        glr_pad = jnp.zeros((d, LANES - GLA_GATE_RANK), F32)
        wtm = jnp.concatenate([wl[:, :s0], wl[:, s1:s2], wl[:, s4:], glr_pad], axis=1).astype(BF16)
        wgu = jnp.concatenate([w_gate_up[l], jnp.zeros((LANES - GLA_GATE_RANK, GLA_KEY_WIDTH), F32)], axis=0).astype(BF16)
        wcm = jnp.concatenate([wl[:, s0:s1] * (GLA_DK ** -0.5), wl[:, s1:s2], wl[:, s2:s3], wl[:, s3:s4]], axis=1).T.astype(BF16)
        wgt = jnp.concatenate([wl[:, s4:], glr_pad], axis=1).T.astype(BF16)
        wgut = wgu.T
        bg = b_gate[l].reshape(1, GLA_KEY_WIDTH)
        bgc = b_gate[l].reshape(GLA_KEY_WIDTH, 1)

        from_x = l == 0
        acts = (x4, meta_sc) if from_x else (h,)
        act_specs = [x4_tile, _resident(meta_sc.shape)] if from_x else [row_tile(d)]
        outs = pl.pallas_call(
            functools.partial(_ffn_inproj_kernel, alpha, from_x, tr),
            grid=(n_main + 1,),
            in_specs=act_specs + [_layer_resident(w1g.shape, l), _layer_resident(w1u.shape, l),
                                  _layer_resident(w1d.shape, l), _layer_resident(l1g.shape, l),
                                  _layer_resident(l1b.shape, l),
                                  _resident(wtm.shape), _resident(wgu.shape), _resident(bg.shape),
                                  _resident(wcm.shape), _resident(wgt.shape), _resident(wgut.shape),
                                  _resident(bgc.shape)],
            out_specs=[row_tile(d), row_tile(POOL_WIDTH), row_tile(GLA_KEY_WIDTH), row_tile(GLA_KEY_WIDTH),
                       col_tile(GLA_KEY_WIDTH), col_tile(GLA_KEY_WIDTH), col_tile(GLA_KEY_WIDTH),
                       col_tile(GLA_WIDTH), col_tile(GLA_WIDTH),
                       pl.BlockSpec((FLAG_ROWS, LANES), lambda i: (i, 0))],
            out_shape=[jax.ShapeDtypeStruct((n_rows, d), F32),
                       jax.ShapeDtypeStruct((n_rows, POOL_WIDTH), F32),
                       jax.ShapeDtypeStruct((n_rows, GLA_KEY_WIDTH), F32),
                       jax.ShapeDtypeStruct((n_rows, GLA_KEY_WIDTH), F32),
                       jax.ShapeDtypeStruct((GLA_KEY_WIDTH, n_rows), F32),
                       jax.ShapeDtypeStruct((GLA_KEY_WIDTH, n_rows), F32),
                       jax.ShapeDtypeStruct((GLA_KEY_WIDTH, n_rows), F32),
                       jax.ShapeDtypeStruct((GLA_WIDTH, n_rows), BF16),
                       jax.ShapeDtypeStruct((GLA_WIDTH, n_rows), F32),
                       jax.ShapeDtypeStruct(((n_main + 1) * FLAG_ROWS, LANES), jnp.int32)],
            compiler_params=cparams,
            name=f"ffn_inproj_{l}",
        )(*acts, w1g, w1u, w1d, l1g, l1b, wtm, wgu, bg, wcm, wgt, wgut, bgc)
        h1, u_tm, k_tm, la_tm, q_t, k_t, la_t, v_t, r_t, tile_flags = outs
        sc_flags = tile_flags.reshape(n_main + 1, FLAG_ROWS, LANES)[:, :tc, 0].reshape(-1)[:n_sc]

        gn = jnp.broadcast_to(gla_norm_g[l].reshape(GLA_WIDTH, 1), (GLA_WIDTH, SC_ROWS))
        y_cat = pl.pallas_call(
            functools.partial(_mixer_kernel, tc),
            grid_spec=pltpu.PrefetchScalarGridSpec(
                num_scalar_prefetch=1,
                grid=(n_main + 1,),
                in_specs=[mix_rows(POOL_WIDTH), mix_rows(GLA_KEY_WIDTH), mix_rows(GLA_KEY_WIDTH),
                          mix_cols(GLA_KEY_WIDTH), mix_cols(GLA_KEY_WIDTH), mix_cols(GLA_KEY_WIDTH),
                          mix_cols(GLA_WIDTH), mix_cols(GLA_WIDTH),
                          _resident(lbd.shape), _resident(ubd.shape), _resident(uprev.shape), _resident(slot.shape),
                          _resident(dmask.shape), _resident(causal.shape), _resident(eye.shape), _resident(gn.shape),
                          pl.BlockSpec((None, SC_ROWS, POOL_WIDTH), lambda i, f: (jnp.where(i == 0, 1, 0), 0, 0)),
                          _layer_resident(wpool_all.shape, l), _layer_resident(pscale_all.shape, l)],
                out_specs=mix_rows(d),
                scratch_shapes=[pltpu.VMEM((GLA_HEADS, GLA_DV, SC_ROWS), F32),
                                pltpu.VMEM((BATCH, SUB + tc * GLA_CHUNK, POOL_WIDTH), F32),
                                pltpu.VMEM((tc, SC_ROWS, GLA_KEY_WIDTH), F32),
                                pltpu.VMEM((GLA_HEADS * SUB, SC_ROWS), F32),
                                pltpu.VMEM((SC_ROWS, GLA_HEADS * SC_ROWS), F32)]),
            out_shape=jax.ShapeDtypeStruct((n_rows, d), BF16),
            compiler_params=cparams,
            name=f"mixer_{l}",
        )(sc_flags, u_tm, k_tm, la_tm, q_t, k_t, la_t, v_t, r_t, lbd, ubd, uprev, slot, dmask, causal, eye, gn, cnt,
          wpool_all, pscale_all)

        to_out = l == depth - 1
        h = pl.pallas_call(
            functools.partial(_outproj_ffn_kernel, alpha, to_out),
            grid=(n_main if to_out else n_main + 1,),
            in_specs=[row_tile(d), row_tile(d), _layer_resident(wo_all.shape, l),
                      _layer_resident(l2g.shape, l), _layer_resident(l2b.shape, l),
                      _layer_resident(w2g.shape, l), _layer_resident(w2u.shape, l), _layer_resident(w2d.shape, l),
                      _layer_resident(l3g.shape, l), _layer_resident(l3b.shape, l)],
            out_specs=pl.BlockSpec((batch, tc, GLA_CHUNK, d), lambda i: (0, i, 0, 0)) if to_out else row_tile(d),
            out_shape=jax.ShapeDtypeStruct(x4.shape if to_out else (n_rows, d), F32),
            compiler_params=cparams,
            name=f"outproj_ffn_{l}",
        )(h1, y_cat, wo_all, l2g, l2b, w2g, w2u, w2d, l3g, l3b)

    return h.reshape(batch, seq, d)
```

```python
import functools

import numpy as np
import jax
import jax.numpy as jnp
from jax import lax
from jax.experimental import pallas as pl
from jax.experimental.pallas import tpu as pltpu

D_MODEL = 1024
N_META = 16
POOL_WIDTH = 512
POOL_GROUPS = 4
POOL_GROUP_DIM = 128
POOL_WINDOWS = (2, 4, 8, 16)
GLA_WIDTH = 512
GLA_HEADS = 4
GLA_KEY_WIDTH = 256
GLA_DK = 64
GLA_DV = 128
GLA_GATE_RANK = 16
GLA_GATE_TEMP = 16.0
GLA_CHUNK = 64
D_FF = 2816
LN_EPS = 1e-5
RMS_EPS = 1e-6

BATCH = 2
SC_ROWS = BATCH * GLA_CHUNK
SUB = 16
N_SUB = GLA_CHUNK // SUB
LANES = 128
FF_SPLIT = 2
ROW_PARTS = 2
STAGGER_LAG = 1
VMEM_LIMIT_BYTES = 60 * 1024 * 1024
FAST_PATH_MAX_DECAY = 40.0
FLAG_ROWS = 8

F32 = jnp.float32
BF16 = jnp.bfloat16


def _dot(a, b):
    return jnp.dot(a, b, preferred_element_type=F32)


def _dot_nt(a, b):
    return lax.dot_general(a, b, (((1,), (1,)), ((), ())), preferred_element_type=F32)


def _layer_norm(z, g, b):
    mu = jnp.mean(z, axis=-1, keepdims=True)
    zc = z - mu
    var = jnp.mean(zc * zc, axis=-1, keepdims=True)
    return zc * lax.rsqrt(var + LN_EPS) * g + b


def _silu(x):
    return x * jax.nn.sigmoid(x)


def _log_sigmoid(x):
    return jnp.minimum(x, 0.0) - jnp.log1p(jnp.exp(-jnp.abs(x)))


def _swiglu_half_step(x, wg_ref, wu_ref, wd_ref, alpha):
    xb = x.astype(BF16)
    fc = D_FF // FF_SPLIT
    y = None
    for c in range(FF_SPLIT):
        cols = slice(c * fc, (c + 1) * fc)
        g = _dot(xb, wg_ref[:, cols])
        u = _dot(xb, wu_ref[:, cols])
        yield
        act = (_silu(g) * u).astype(BF16)
        yield
        part = _dot(act, wd_ref[cols, :])
        y = part if y is None else y + part
    yield
    return alpha * x + 0.5 * y


class _Stagger:
    def __init__(self, gens, lag=STAGGER_LAG):
        self.gens = list(gens)
        self.lag = lag
        self.t = 0
        self.done = [False] * len(self.gens)
        self.results = [None] * len(self.gens)

    def tick(self):
        for k, gen in enumerate(self.gens):
            if self.done[k] or self.t < k * self.lag:
                continue
            try:
                next(gen)
            except StopIteration as stop:
                self.results[k], self.done[k] = stop.value, True
        self.t += 1
        return not all(self.done)


def _run_phases(*staggers):
    while any([s.tick() for s in staggers]):
        pass


def _staggered(gens, lag=STAGGER_LAG):
    s = _Stagger(gens, lag)
    _run_phases(s)
    return s.results


def _ffn_inproj_rows(alpha, x, r0, is_meta, wg_ref, wu_ref, wd_ref, lng_ref, lnb_ref,
                     wtm_ref, wgu_ref, bg_ref, wcm_ref, wgt_ref, wgut_ref, bgc_ref,
                     h_ref, u_ref, k_ref, la_ref, qt_ref, kt_ref, lat_ref, vt_ref, rt_ref, flag_ref):
    n = x.shape[0]
    rows = slice(r0, r0 + n)
    z = yield from _swiglu_half_step(x, wg_ref, wu_ref, wd_ref, alpha)
    h = _layer_norm(z, lng_ref[...], lnb_ref[...])
    h_ref[rows, :] = h
    hb = h.astype(BF16)
    yield

    if is_meta:
        pad = GLA_CHUNK - N_META
        keep_r = (lax.broadcasted_iota(jnp.int32, (n, 1), 0) % GLA_CHUNK) >= pad
        keep_l = (lax.broadcasted_iota(jnp.int32, (1, n), 1) % GLA_CHUNK) >= pad
        mask_r = lambda v: jnp.where(keep_r, v, 0.0)
        mask_l = lambda v: jnp.where(keep_l, v, 0.0)
    else:
        mask_r = mask_l = lambda v: v

    ztm = _dot(hb, wtm_ref[...])
    u_ref[rows, :] = mask_r(ztm[:, :POOL_WIDTH])
    k_ref[rows, :] = mask_r(ztm[:, POOL_WIDTH:POOL_WIDTH + GLA_KEY_WIDTH])
    glr = ztm[:, POOL_WIDTH + GLA_KEY_WIDTH:].astype(BF16)
    la = mask_r(_log_sigmoid(_dot(glr, wgu_ref[...]) + bg_ref[...]) * (1.0 / GLA_GATE_TEMP))
    la_ref[rows, :] = la
    yield

    zt = _dot_nt(wcm_ref[...], hb)
    kw = GLA_KEY_WIDTH
    qt_ref[:, rows] = mask_l(zt[:kw])
    kt_ref[:, rows] = mask_l(zt[kw:2 * kw])
    vt_ref[:, rows] = mask_l(zt[2 * kw:2 * kw + GLA_WIDTH]).astype(BF16)
    rt_ref[:, rows] = _silu(zt[2 * kw + GLA_WIDTH:])
    gt = _dot_nt(wgt_ref[...], hb).astype(BF16)
    lat_ref[:, rows] = mask_l(_log_sigmoid(_dot(wgut_ref[...], gt) + bgc_ref[...]) * (1.0 / GLA_GATE_TEMP))

    flags = []
    for s in range(n // SC_ROWS):
        tot = jnp.minimum(jnp.sum(la[SC_ROWS * s:SC_ROWS * s + GLA_CHUNK], axis=0, keepdims=True),
                          jnp.sum(la[SC_ROWS * s + GLA_CHUNK:SC_ROWS * (s + 1)], axis=0, keepdims=True))
        flags.append((jnp.min(tot, axis=1, keepdims=True) < -FAST_PATH_MAX_DECAY).astype(jnp.int32))
    return flags


def _ffn_inproj_kernel(alpha, from_x, tile_rows, x_ref, *refs):
    if from_x:
        meta_ref, *refs = refs
    flag_ref = refs[-1]
    last = pl.num_programs(0) - 1

    def x_rows(r0, n):
        if not from_x:
            return x_ref[r0:r0 + n, :]
        pieces = [divmod(r0 // GLA_CHUNK + i, BATCH) for i in range(n // GLA_CHUNK)]
        return jnp.concatenate([x_ref[b, s] for s, b in pieces], axis=0)

    def run(parts, is_meta):
        gens = [_ffn_inproj_rows(alpha, x, r0, is_meta, *refs) for x, r0 in parts]
        sc_flags = [f for fl in _staggered(gens, STAGGER_LAG) for f in fl]
        frow = lax.broadcasted_iota(jnp.int32, (FLAG_ROWS, LANES), 0)
        flags = jnp.zeros((FLAG_ROWS, LANES), jnp.int32)
        for s, f in enumerate(sc_flags):
            flags = jnp.where(frow == s, f, flags)
        flag_ref[...] = flags

    @pl.when(pl.program_id(0) < last)
    def _():
        pr = tile_rows // ROW_PARTS
        run([(x_rows(p * pr, pr), p * pr) for p in range(ROW_PARTS)], False)

    @pl.when(pl.program_id(0) == last)
    def _():
        run([(meta_ref[...] if from_x else x_ref[0:SC_ROWS, :], 0)], True)


def _split3(x):
    hi = x.astype(BF16)
    r1 = x - hi.astype(F32)
    mid = r1.astype(BF16)
    lo = (r1 - mid.astype(F32)).astype(BF16)
    return hi, mid, lo


def _dot3_left(c01, x):
    parts = _split3(x)
    return _dot(c01, parts[0]) + _dot(c01, parts[1]) + _dot(c01, parts[2])


def _dot3_right(x, c01):
    parts = _split3(x)
    return _dot(parts[0], c01) + _dot(parts[1], c01) + _dot(parts[2], c01)


def _mix_chunks(n, robust, in_refs, const_refs, y_ref, st_ref, st_new_ref, ext_ref, b_scr, sd_scr, at_scr):
    u_ref, k_ref, la_ref, qt_ref, kt_ref, lat_ref, vt_ref, rt_ref = in_refs
    (lbd_ref, ubd_ref, uprev_ref, slot_ref, dmask_ref, causal_ref, eye_ref, gn_ref, cnt_ref,
     wpool_ref, pscale_ref) = const_refs
    C = GLA_CHUNK
    lane = lax.broadcasted_iota(jnp.int32, (SC_ROWS, SC_ROWS), 1)
    rowi = lax.broadcasted_iota(jnp.int32, (SC_ROWS, SC_ROWS), 0)
    lane64 = lax.broadcasted_iota(jnp.int32, (C, SC_ROWS), 1)
    lane8 = lax.broadcasted_iota(jnp.int32, (8, SC_ROWS), 1)
    lane_lo = lane < C
    row_lo = rowi < C
    row_head = lax.broadcasted_iota(jnp.int32, (GLA_KEY_WIDTH, SC_ROWS), 0) // GLA_DK

    def heads_on_lanes(xt):
        return jnp.concatenate([jnp.where(row_head == h, xt, 0.0) for h in range(GLA_HEADS)], axis=1).astype(BF16)

    for c in range(n):
        for b in range(BATCH):
            ext_ref[b, SUB + C * c:SUB + C * (c + 1), :] = u_ref[SC_ROWS * c + C * b:SC_ROWS * c + C * (b + 1), :]
    states = [st_ref[h] for h in range(GLA_HEADS)]
    cnt = cnt_ref[...]
    lbd = lbd_ref[...]
    ubd = ubd_ref[...]
    eye = eye_ref[...]
    gn = gn_ref[...]

    def chunk(c):
        rows = slice(SC_ROWS * c, SC_ROWS * (c + 1))

        u = u_ref[rows, :]
        parts = []
        for g, w in enumerate(POOL_WINDOWS):
            cols = slice(POOL_GROUP_DIM * g, POOL_GROUP_DIM * (g + 1))
            sums = []
            for b in range(BATCH):
                r0 = SUB + C * c
                s = ext_ref[b, r0:r0 + C, cols]
                for back in range(1, w):
                    s = s + ext_ref[b, r0 - back:r0 + C - back, cols]
                sums.append(s)
            s = jnp.concatenate(sums, axis=0)
            p = s / cnt[:, cols] - u[:, cols]
            parts.append(_dot(p.astype(BF16), wpool_ref[g]))
        y_pool = jnp.concatenate(parts, axis=1) * pscale_ref[...]

        b_tm = _dot3_left(lbd, la_ref[rows, :])
        bT = _dot3_right(lat_ref[:, rows], ubd)
        yield
        b_scr[c] = b_tm
        ends = [[b_scr[c, pl.ds(C * b + SUB * j + SUB - 1, 1), :] for j in range(N_SUB)] for b in range(BATCH)]

        def per_block(fn):
            return jnp.concatenate(
                [jnp.broadcast_to(fn(b, j), (SUB, GLA_KEY_WIDTH)) for b in range(BATCH) for j in range(N_SUB)], axis=0)

        k_tm = k_ref[rows, :]
        qT = qt_ref[:, rows]
        e_last = per_block(lambda b, j: ends[b][N_SUB - 1])
        ktil = k_tm * jnp.exp(e_last - b_tm)
        qeT = qT * jnp.exp(bT)

        if not robust:
            kneg = (k_tm * jnp.exp(-b_tm)).astype(BF16)
            at_all = jnp.where(causal_ref[...] != 0, _dot(kneg, heads_on_lanes(qeT)), 0.0)
        else:
            cprevT = _dot3_right(lat_ref[:, rows], uprev_ref[...])
            e_own = per_block(lambda b, j: ends[b][j])
            f2 = per_block(lambda b, j: jnp.exp(ends[b][min(j + 1, N_SUB - 1)] - ends[b][j]))
            f3 = per_block(lambda b, j: jnp.exp(ends[b][min(j + 2, N_SUB - 1)] - ends[b][j]))
            khat = k_tm * jnp.exp(e_own - b_tm)
            kslots = jnp.concatenate([khat, khat * f2, khat * f3], axis=0).astype(BF16)
            r_all = _dot(kslots, heads_on_lanes(qT * jnp.exp(bT - cprevT)))
            slot = slot_ref[...]
            at_off = jnp.where(slot == 1, r_all[0:SC_ROWS],
                               jnp.where(slot == 2, r_all[SC_ROWS:2 * SC_ROWS],
                                         jnp.where(slot == 3, r_all[2 * SC_ROWS:], 0.0)))
            kT = kt_ref[:, rows]
            for dist in range(SUB):
                if dist == 0:
                    qs, bs = qT, bT
                else:
                    qs = pltpu.roll(qT, SC_ROWS - dist, axis=1)
                    bs = pltpu.roll(bT, SC_ROWS - dist, axis=1)
                prod = qs * kT * jnp.exp(jnp.minimum(bs - bT, 0.0))
                for h in range(GLA_HEADS):
                    sd_scr[pl.ds(SUB * h + dist, 1), :] = jnp.sum(
                        prod[GLA_DK * h:GLA_DK * (h + 1)], axis=0, keepdims=True)
            sd = jnp.concatenate([sd_scr[...], jnp.zeros((SC_ROWS - GLA_HEADS * SUB, SC_ROWS), F32)], axis=0)
            sdt = sd.T
            dmask = dmask_ref[...]
            for h in range(GLA_HEADS):
                xh = jnp.where((lane >= SUB * h) & (lane < SUB * (h + 1)), sdt, 0.0)
                skew = pltpu.roll(xh, (SC_ROWS - SUB * h) % SC_ROWS, axis=1, stride=1, stride_axis=0)
                at_scr[:, SC_ROWS * h:SC_ROWS * (h + 1)] = (
                    jnp.where(dmask != 0, skew, 0.0) + at_off[:, SC_ROWS * h:SC_ROWS * (h + 1)])
            at_all = at_scr[...]
        yield

        vT = vt_ref[:, rows]
        rT = rt_ref[:, rows]
        y_heads = []
        for h in range(GLA_HEADS):
            at_h = at_all[:, SC_ROWS * h:SC_ROWS * (h + 1)]

            half = (h % 2) * C
            kt_cols = ktil[:, LANES * (h // 2):LANES * (h // 2 + 1)]
            kt_roll = pltpu.roll(kt_cols, C, axis=1)
            lo_src, hi_src = (kt_cols, kt_roll) if half == 0 else (kt_roll, kt_cols)
            kbd = jnp.where(row_lo & lane_lo, lo_src, jnp.where((~row_lo) & (~lane_lo), hi_src, 0.0))
            qe_h = qeT[GLA_DK * h:GLA_DK * (h + 1)]
            qebd = jnp.concatenate([jnp.where(lane64 < C, qe_h, 0.0), jnp.where(lane64 >= C, qe_h, 0.0)], axis=0)

            v_h = vT[GLA_DV * h:GLA_DV * (h + 1)]
            res = _dot(v_h, jnp.concatenate([at_h, kbd], axis=1).astype(BF16))
            st = states[h]
            o_t = res[:, :SC_ROWS] + _dot(st.astype(BF16), qebd.astype(BF16))

            e0 = jnp.broadcast_to(ends[0][N_SUB - 1][:, LANES * (h // 2):LANES * (h // 2 + 1)], (8, LANES))
            e1 = jnp.broadcast_to(ends[1][N_SUB - 1][:, LANES * (h // 2):LANES * (h // 2 + 1)], (8, LANES))
            if half == 0:
                e1 = pltpu.roll(e1, C, axis=1)
            else:
                e0 = pltpu.roll(e0, C, axis=1)
            dec = jnp.exp(jnp.where(lane8 < C, e0, e1))
            states[h] = st * jnp.broadcast_to(dec[0:1], (GLA_DV, SC_ROWS)) + res[:, SC_ROWS:]

            ms = jnp.mean(o_t * o_t, axis=0, keepdims=True)
            gsl = slice(GLA_DV * h, GLA_DV * (h + 1))
            y_heads.append(o_t * lax.rsqrt(ms + RMS_EPS) * gn[gsl] * rT[gsl])
        y_t = jnp.concatenate(y_heads, axis=0).astype(BF16)
        y_gla = _dot_nt(eye, y_t)
        yield
        y_ref[rows, :] = jnp.concatenate([y_pool, y_gla], axis=1).astype(BF16)

    def in_turn():
        for c in range(n):
            yield from chunk(c)

    def finish():
        for h in range(GLA_HEADS):
            st_new_ref[h] = states[h]

    return _Stagger([in_turn()] if robust else [chunk(c) for c in range(n)]), finish


def _commit_mixer_step(n, st_ref, st_new_ref, ext_ref):
    st_ref[...] = st_new_ref[...]
    for b in range(BATCH):
        ext_ref[b, 0:SUB, :] = ext_ref[b, GLA_CHUNK * n:GLA_CHUNK * n + SUB, :]


def _outproj_ffn_rows(alpha, h, y, store, wo_ref, l2g_ref, l2b_ref, wg_ref, wu_ref, wd_ref, l3g_ref, l3b_ref):
    z = alpha * h + _dot(y, wo_ref[...])
    yield
    h2 = _layer_norm(z, l2g_ref[...], l2b_ref[...])
    yield
    z2 = yield from _swiglu_half_step(h2, wg_ref, wu_ref, wd_ref, alpha)
    store(_layer_norm(z2, l3g_ref[...], l3b_ref[...]))


def _mix_ffn_kernel(alpha, tc, to_out, flags_ref, *refs):
    in_refs, const_refs, h_ref, w_refs, o_ref = refs[:8], refs[8:19], refs[19], refs[20:28], refs[28]
    st_ref, st_new_ref, ext_ref, b_scr, sd_scr, at_scr, ybuf = refs[29:]
    step = pl.program_id(0)
    n_main = pl.num_programs(0) - 2
    tile_rows = h_ref.shape[0]

    def mixer(n, robust):
        return _mix_chunks(n, robust, in_refs, const_refs, ybuf, st_ref, st_new_ref, ext_ref, b_scr, sd_scr, at_scr)

    def store_rows(r0):
        def store(o):
            o_ref[r0:r0 + o.shape[0], :] = o
        return store

    def store_chunks(r0):
        def store(o):
            for sb in range(o.shape[0] // GLA_CHUNK):
                s, b = divmod(r0 // GLA_CHUNK + sb, BATCH)
                o_ref[b, s] = o[sb * GLA_CHUNK:(sb + 1) * GLA_CHUNK]
        return store

    def per_token(n_rows, n_parts, make_store):
        pr = n_rows // n_parts
        y = ybuf[0:n_rows, :]
        return _Stagger([_outproj_ffn_rows(alpha, h_ref[p * pr:(p + 1) * pr, :], y[p * pr:(p + 1) * pr],
                                           make_store(p * pr), *w_refs) for p in range(n_parts)])

    @pl.when(step == 0)
    def _():
        st_ref[...] = jnp.zeros(st_ref.shape, F32)
        ext_ref[:, 0:SUB, :] = jnp.zeros((BATCH, SUB, POOL_WIDTH), F32)
        phases, finish = mixer(1, True)
        _run_phases(phases)
        finish()
        _commit_mixer_step(1, st_ref, st_new_ref, ext_ref)

    if not to_out:
        @pl.when(step == 1)
        def _():
            _run_phases(per_token(SC_ROWS, 1, store_rows))

    @pl.when(step >= 2)
    def _():
        phases, finish = mixer(tc, False)
        _run_phases(per_token(tile_rows, ROW_PARTS, store_chunks if to_out else store_rows), phases)
        finish()

    base = jnp.clip(step - 1, 0, n_main - 1) * tc
    slow = flags_ref[base]
    for c in range(1, tc):
        slow = jnp.maximum(slow, flags_ref[base + c])

    @pl.when((step == 1) | ((step >= 2) & (step <= n_main) & (slow != 0)))
    def _():
        phases, finish = mixer(tc, True)
        _run_phases(phases)
        finish()

    @pl.when(step >= 1)
    def _():
        _commit_mixer_step(tc, st_ref, st_new_ref, ext_ref)


def _resident(shape):
    nd = len(shape)
    return pl.BlockSpec(shape, lambda *_: (0,) * nd, pipeline_mode=pl.Buffered(1))


def _layer_resident(shape, layer):
    nd = len(shape) - 1
    return pl.BlockSpec((None,) + tuple(shape[1:]), lambda *_: (layer,) + (0,) * nd, pipeline_mode=pl.Buffered(1))


def _mixer_constants():
    r = np.arange(SC_ROWS)
    b, t = r // GLA_CHUNK, r % GLA_CHUNK
    same_b = b[:, None] == b[None, :]
    lbd = same_b & (t[None, :] <= t[:, None])
    ubd = lbd.T
    blk = t // SUB
    uprev = same_b & (blk[:, None] < blk[None, :])
    dist = np.where(same_b, blk[None, :] - blk[:, None], 0)
    slot = np.where((dist >= 1) & (dist < N_SUB), dist, 0).astype(np.int32)
    slot = np.tile(slot, (1, GLA_HEADS))
    dmask = (same_b & (blk[:, None] == blk[None, :]) & (t[None, :] >= t[:, None])).astype(np.int32)
    causal = np.tile((same_b & (t[None, :] >= t[:, None])).astype(np.int32), (1, GLA_HEADS))
    eye = np.eye(SC_ROWS)
    w_lane = np.repeat(np.array(POOL_WINDOWS, np.float32), POOL_GROUP_DIM)[None, :]
    t_meta = np.maximum(t - (GLA_CHUNK - N_META), 0).astype(np.float32)[:, None]
    cnt = np.stack([np.broadcast_to(w_lane, (SC_ROWS, POOL_WIDTH)), np.minimum(t_meta + 1.0, w_lane)])
    return (jnp.asarray(lbd, BF16), jnp.asarray(ubd, BF16), jnp.asarray(uprev, BF16), jnp.asarray(slot),
            jnp.asarray(dmask), jnp.asarray(causal), jnp.asarray(eye, BF16), jnp.asarray(cnt, F32))


def _tile_chunks(n_ch):
    return next(k for k in (4, 2, 1) if n_ch % k == 0)


def kernel(x, meta_tokens, ffn1_w_gate, ffn1_w_up, ffn1_w_down, ln1_g, ln1_b, w_in, w_gate_up, b_gate, w_pool, pool_scale, gla_norm_g, w_out, ln2_g, ln2_b, ffn2_w_gate, ffn2_w_up, ffn2_w_down, ln3_g, ln3_b):
    batch, seq, d = x.shape
    depth = w_in.shape[0]
    assert batch == BATCH and d == D_MODEL and seq % GLA_CHUNK == 0
    n_ch = seq // GLA_CHUNK
    n_sc = n_ch + 1
    n_rows = n_sc * SC_ROWS
    tc = _tile_chunks(n_ch)
    tr = tc * SC_ROWS
    n_main = n_ch // tc
    alpha = (2.0 * depth) ** 0.25

    x4 = x.reshape(batch, n_ch, GLA_CHUNK, d)
    meta_half = jnp.concatenate([jnp.zeros((GLA_CHUNK - N_META, d), x.dtype), meta_tokens.astype(x.dtype)], axis=0)
    meta_sc = jnp.concatenate([meta_half, meta_half], axis=0)
    h = None

    lbd, ubd, uprev, slot, dmask, causal, eye, cnt = _mixer_constants()
    s0 = POOL_WIDTH
    s1 = s0 + GLA_KEY_WIDTH
    s2 = s1 + GLA_KEY_WIDTH
    s3 = s2 + GLA_WIDTH
    s4 = s3 + GLA_WIDTH
    cparams = pltpu.CompilerParams(dimension_semantics=("arbitrary",), vmem_limit_bytes=VMEM_LIMIT_BYTES)
    row_tile = lambda cols: pl.BlockSpec((tr, cols), lambda i: (i, 0))
    col_tile = lambda rows: pl.BlockSpec((rows, tr), lambda i: (0, i))
    x4_tile = pl.BlockSpec((batch, tc, GLA_CHUNK, d), lambda i: (0, jnp.minimum(i, n_main - 1), 0, 0))
    mix_tile = lambda i: jnp.where(i == 0, n_main, jnp.clip(i - 1, 0, n_main - 1))
    mix_rows = lambda cols: pl.BlockSpec((tr, cols), lambda i, f: (mix_tile(i), 0))
    mix_cols = lambda rows: pl.BlockSpec((rows, tr), lambda i, f: (0, mix_tile(i)))
    late_tile = lambda i: jnp.where(i <= 1, n_main, i - 2)
    late_rows = pl.BlockSpec((tr, d), lambda i, f: (late_tile(i), 0))
    late_rows_main = pl.BlockSpec((tr, d), lambda i, f: (jnp.maximum(i - 2, 0), 0))
    late_out4 = pl.BlockSpec((batch, tc, GLA_CHUNK, d), lambda i, f: (0, jnp.maximum(i - 2, 0), 0, 0))

    bf = lambda w: w.astype(BF16)
    w1g, w1u, w1d = bf(ffn1_w_gate), bf(ffn1_w_up), bf(ffn1_w_down)
    w2g, w2u, w2d = bf(ffn2_w_gate), bf(ffn2_w_up), bf(ffn2_w_down)
    wo_all, wpool_all = bf(w_out), bf(w_pool)
    row3 = lambda p: p.reshape(depth, 1, p.shape[-1])
    l1g, l1b, l2g, l2b, l3g, l3b = (row3(p) for p in (ln1_g, ln1_b, ln2_g, ln2_b, ln3_g, ln3_b))
    pscale_all = row3(pool_scale)

    for l in range(depth):
        wl = w_in[l]
        glr_pad = jnp.zeros((d, LANES - GLA_GATE_RANK), F32)
        wtm = jnp.concatenate([wl[:, :s0], wl[:, s1:s2], wl[:, s4:], glr_pad], axis=1).astype(BF16)
        wgu = jnp.concatenate([w_gate_up[l], jnp.zeros((LANES - GLA_GATE_RANK, GLA_KEY_WIDTH), F32)], axis=0).astype(BF16)
        wcm = jnp.concatenate([wl[:, s0:s1] * (GLA_DK ** -0.5), wl[:, s1:s2], wl[:, s2:s3], wl[:, s3:s4]], axis=1).T.astype(BF16)
        wgt = jnp.concatenate([wl[:, s4:], glr_pad], axis=1).T.astype(BF16)
        wgut = wgu.T
        bg = b_gate[l].reshape(1, GLA_KEY_WIDTH)
        bgc = b_gate[l].reshape(GLA_KEY_WIDTH, 1)

        from_x = l == 0
        acts = (x4, meta_sc) if from_x else (h,)
        act_specs = [x4_tile, _resident(meta_sc.shape)] if from_x else [row_tile(d)]
        outs = pl.pallas_call(
            functools.partial(_ffn_inproj_kernel, alpha, from_x, tr),
            grid=(n_main + 1,),
            in_specs=act_specs + [_layer_resident(w1g.shape, l), _layer_resident(w1u.shape, l),
                                  _layer_resident(w1d.shape, l), _layer_resident(l1g.shape, l),
                                  _layer_resident(l1b.shape, l),
                                  _resident(wtm.shape), _resident(wgu.shape), _resident(bg.shape),
                                  _resident(wcm.shape), _resident(wgt.shape), _resident(wgut.shape),
                                  _resident(bgc.shape)],
            out_specs=[row_tile(d), row_tile(POOL_WIDTH), row_tile(GLA_KEY_WIDTH), row_tile(GLA_KEY_WIDTH),
                       col_tile(GLA_KEY_WIDTH), col_tile(GLA_KEY_WIDTH), col_tile(GLA_KEY_WIDTH),
                       col_tile(GLA_WIDTH), col_tile(GLA_WIDTH),
                       pl.BlockSpec((FLAG_ROWS, LANES), lambda i: (i, 0))],
            out_shape=[jax.ShapeDtypeStruct((n_rows, d), F32),
                       jax.ShapeDtypeStruct((n_rows, POOL_WIDTH), F32),
                       jax.ShapeDtypeStruct((n_rows, GLA_KEY_WIDTH), F32),
                       jax.ShapeDtypeStruct((n_rows, GLA_KEY_WIDTH), F32),
                       jax.ShapeDtypeStruct((GLA_KEY_WIDTH, n_rows), F32),
                       jax.ShapeDtypeStruct((GLA_KEY_WIDTH, n_rows), F32),
                       jax.ShapeDtypeStruct((GLA_KEY_WIDTH, n_rows), F32),
                       jax.ShapeDtypeStruct((GLA_WIDTH, n_rows), BF16),
                       jax.ShapeDtypeStruct((GLA_WIDTH, n_rows), F32),
                       jax.ShapeDtypeStruct(((n_main + 1) * FLAG_ROWS, LANES), jnp.int32)],
            compiler_params=cparams,
            name=f"ffn_inproj_{l}",
        )(*acts, w1g, w1u, w1d, l1g, l1b, wtm, wgu, bg, wcm, wgt, wgut, bgc)
        h1, u_tm, k_tm, la_tm, q_t, k_t, la_t, v_t, r_t, tile_flags = outs
        sc_flags = tile_flags.reshape(n_main + 1, FLAG_ROWS, LANES)[:, :tc, 0].reshape(-1)[:n_sc]

        gn = jnp.broadcast_to(gla_norm_g[l].reshape(GLA_WIDTH, 1), (GLA_WIDTH, SC_ROWS))
        to_out = l == depth - 1
        h = pl.pallas_call(
            functools.partial(_mix_ffn_kernel, alpha, tc, to_out),
            grid_spec=pltpu.PrefetchScalarGridSpec(
                num_scalar_prefetch=1,
                grid=(n_main + 2,),
                in_specs=[mix_rows(POOL_WIDTH), mix_rows(GLA_KEY_WIDTH), mix_rows(GLA_KEY_WIDTH),
                          mix_cols(GLA_KEY_WIDTH), mix_cols(GLA_KEY_WIDTH), mix_cols(GLA_KEY_WIDTH),
                          mix_cols(GLA_WIDTH), mix_cols(GLA_WIDTH),
                          _resident(lbd.shape), _resident(ubd.shape), _resident(uprev.shape), _resident(slot.shape),
                          _resident(dmask.shape), _resident(causal.shape), _resident(eye.shape), _resident(gn.shape),
                          pl.BlockSpec((None, SC_ROWS, POOL_WIDTH), lambda i, f: (jnp.where(i == 0, 1, 0), 0, 0)),
                          _layer_resident(wpool_all.shape, l), _layer_resident(pscale_all.shape, l),
                          late_rows_main if to_out else late_rows,
                          _layer_resident(wo_all.shape, l),
                          _layer_resident(l2g.shape, l), _layer_resident(l2b.shape, l),
                          _layer_resident(w2g.shape, l), _layer_resident(w2u.shape, l), _layer_resident(w2d.shape, l),
                          _layer_resident(l3g.shape, l), _layer_resident(l3b.shape, l)],
                out_specs=late_out4 if to_out else late_rows,
                scratch_shapes=[pltpu.VMEM((GLA_HEADS, GLA_DV, SC_ROWS), F32),
                                pltpu.VMEM((GLA_HEADS, GLA_DV, SC_ROWS), F32),
                                pltpu.VMEM((BATCH, SUB + tc * GLA_CHUNK, POOL_WIDTH), F32),
                                pltpu.VMEM((tc, SC_ROWS, GLA_KEY_WIDTH), F32),
                                pltpu.VMEM((GLA_HEADS * SUB, SC_ROWS), F32),
                                pltpu.VMEM((SC_ROWS, GLA_HEADS * SC_ROWS), F32),
                                pltpu.VMEM((tr, d), BF16)]),
            out_shape=jax.ShapeDtypeStruct(x4.shape if to_out else (n_rows, d), F32),
            compiler_params=cparams,
            name=f"mix_ffn_{l}",
        )(sc_flags, u_tm, k_tm, la_tm, q_t, k_t, la_t, v_t, r_t, lbd, ubd, uprev, slot, dmask, causal, eye, gn, cnt,
          wpool_all, pscale_all, h1, wo_all, l2g, l2b, w2g, w2u, w2d, l3g, l3b)

    return h.reshape(batch, seq, d)
```

```python
import functools

import numpy as np
import jax
import jax.numpy as jnp
from jax import lax
from jax.experimental import pallas as pl
from jax.experimental.pallas import tpu as pltpu

D_MODEL = 1024
N_META = 16
POOL_WIDTH = 512
POOL_GROUPS = 4
POOL_GROUP_DIM = 128
POOL_WINDOWS = (2, 4, 8, 16)
GLA_WIDTH = 512
GLA_HEADS = 4
GLA_KEY_WIDTH = 256
GLA_DK = 64
GLA_DV = 128
GLA_GATE_RANK = 16
GLA_GATE_TEMP = 16.0
GLA_CHUNK = 64
D_FF = 2816
LN_EPS = 1e-5
RMS_EPS = 1e-6

BATCH = 2
SC_ROWS = BATCH * GLA_CHUNK
SUB = 16
N_SUB = GLA_CHUNK // SUB
LANES = 128
FF_SPLIT = 2
ROW_PARTS = 2
STAGGER_LAG = 1
VMEM_LIMIT_BYTES = 60 * 1024 * 1024
FAST_PATH_MAX_DECAY = 40.0
FLAG_ROWS = 8
N_MIX_IN = 6
N_MIX_CONST = 10

F32 = jnp.float32
BF16 = jnp.bfloat16


def _dot(a, b):
    return jnp.dot(a, b, preferred_element_type=F32)


def _dot_nt(a, b):
    return lax.dot_general(a, b, (((1,), (1,)), ((), ())), preferred_element_type=F32)


def _layer_norm(z, g, b):
    mu = jnp.mean(z, axis=-1, keepdims=True)
    zc = z - mu
    var = jnp.mean(zc * zc, axis=-1, keepdims=True)
    return zc * lax.rsqrt(var + LN_EPS) * g + b


def _silu(x):
    return x * jax.nn.sigmoid(x)


def _log_sigmoid(x):
    return jnp.minimum(x, 0.0) - jnp.log1p(jnp.exp(-jnp.abs(x)))


def _swiglu_half_step(x, wg_ref, wu_ref, wd_ref, alpha):
    xb = x.astype(BF16)
    fc = D_FF // FF_SPLIT
    y = None
    for c in range(FF_SPLIT):
        cols = slice(c * fc, (c + 1) * fc)
        g = _dot(xb, wg_ref[:, cols])
        u = _dot(xb, wu_ref[:, cols])
        yield
        act = (_silu(g) * u).astype(BF16)
        yield
        part = _dot(act, wd_ref[cols, :])
        y = part if y is None else y + part
    yield
    return alpha * x + 0.5 * y


class _Stagger:
    def __init__(self, gens, lag=STAGGER_LAG):
        self.gens = list(gens)
        self.lag = lag
        self.t = 0
        self.done = [False] * len(self.gens)
        self.results = [None] * len(self.gens)

    def tick(self):
        for k, gen in enumerate(self.gens):
            if self.done[k] or self.t < k * self.lag:
                continue
            try:
                next(gen)
            except StopIteration as stop:
                self.results[k], self.done[k] = stop.value, True
        self.t += 1
        return not all(self.done)


def _run_phases(*staggers):
    while any([s.tick() for s in staggers]):
        pass


def _staggered(gens, lag=STAGGER_LAG):
    s = _Stagger(gens, lag)
    _run_phases(s)
    return s.results


def _ffn_inproj_rows(alpha, x, r0, is_meta, wg_ref, wu_ref, wd_ref, lng_ref, lnb_ref,
                     wtm_ref, wcm_ref, wgt_ref, wgut_ref, bgc_ref,
                     h_ref, u_ref, k_ref, qt_ref, lat_ref, vt_ref, rt_ref, flag_ref):
    n = x.shape[0]
    rows = slice(r0, r0 + n)
    z = yield from _swiglu_half_step(x, wg_ref, wu_ref, wd_ref, alpha)
    h = _layer_norm(z, lng_ref[...], lnb_ref[...])
    h_ref[rows, :] = h
    hb = h.astype(BF16)
    yield

    if is_meta:
        pad = GLA_CHUNK - N_META
        keep_r = (lax.broadcasted_iota(jnp.int32, (n, 1), 0) % GLA_CHUNK) >= pad
        keep_l = (lax.broadcasted_iota(jnp.int32, (1, n), 1) % GLA_CHUNK) >= pad
        mask_r = lambda v: jnp.where(keep_r, v, 0.0)
        mask_l = lambda v: jnp.where(keep_l, v, 0.0)
    else:
        mask_r = mask_l = lambda v: v

    ztm = _dot(hb, wtm_ref[...])
    u_ref[rows, :] = mask_r(ztm[:, :POOL_WIDTH])
    k_ref[rows, :] = mask_r(ztm[:, POOL_WIDTH:])
    yield

    zt = _dot_nt(wcm_ref[...], hb)
    kw = GLA_KEY_WIDTH
    qt_ref[:, rows] = mask_l(zt[:kw])
    vt_ref[:, rows] = mask_l(zt[kw:kw + GLA_WIDTH]).astype(BF16)
    rt_ref[:, rows] = _silu(zt[kw + GLA_WIDTH:])
    gt = _dot_nt(wgt_ref[...], hb).astype(BF16)
    lat = mask_l(_log_sigmoid(_dot(wgut_ref[...], gt) + bgc_ref[...]) * (1.0 / GLA_GATE_TEMP))
    lat_ref[:, rows] = lat

    flags = []
    lane_lo = lax.broadcasted_iota(jnp.int32, (GLA_KEY_WIDTH, SC_ROWS), 1) < GLA_CHUNK
    for s in range(n // SC_ROWS):
        seg = lat[:, SC_ROWS * s:SC_ROWS * (s + 1)]
        tot = jnp.minimum(jnp.sum(jnp.where(lane_lo, seg, 0.0), axis=1, keepdims=True),
                          jnp.sum(jnp.where(lane_lo, 0.0, seg), axis=1, keepdims=True))
        flags.append((jnp.min(tot, axis=0, keepdims=True) < -FAST_PATH_MAX_DECAY).astype(jnp.int32))
    return flags


def _ffn_inproj_kernel(alpha, from_x, tile_rows, x_ref, *refs):
    if from_x:
        meta_ref, *refs = refs
    flag_ref = refs[-1]
    last = pl.num_programs(0) - 1

    def x_rows(r0, n):
        if not from_x:
            return x_ref[r0:r0 + n, :]
        pieces = [divmod(r0 // GLA_CHUNK + i, BATCH) for i in range(n // GLA_CHUNK)]
        return jnp.concatenate([x_ref[b, s] for s, b in pieces], axis=0)

    def run(parts, is_meta):
        gens = [_ffn_inproj_rows(alpha, x, r0, is_meta, *refs) for x, r0 in parts]
        sc_flags = [f for fl in _staggered(gens, STAGGER_LAG) for f in fl]
        frow = lax.broadcasted_iota(jnp.int32, (FLAG_ROWS, LANES), 0)
        flags = jnp.zeros((FLAG_ROWS, LANES), jnp.int32)
        for s, f in enumerate(sc_flags):
            flags = jnp.where(frow == s, f, flags)
        flag_ref[...] = flags

    @pl.when(pl.program_id(0) < last)
    def _():
        pr = tile_rows // ROW_PARTS
        run([(x_rows(p * pr, pr), p * pr) for p in range(ROW_PARTS)], False)

    @pl.when(pl.program_id(0) == last)
    def _():
        run([(meta_ref[...] if from_x else x_ref[0:SC_ROWS, :], 0)], True)


def _split3(x):
    hi = x.astype(BF16)
    r1 = x - hi.astype(F32)
    mid = r1.astype(BF16)
    lo = (r1 - mid.astype(F32)).astype(BF16)
    return hi, mid, lo


def _transpose_cm(xt):
    return jnp.concatenate([xt[:SC_ROWS].T, xt[SC_ROWS:].T], axis=1)


def _dot3_right(x, c01):
    parts = _split3(x)
    return _dot(parts[0], c01) + _dot(parts[1], c01) + _dot(parts[2], c01)


def _mix_chunks(n, robust, in_refs, const_refs, y_ref, st_ref, st_new_ref, ext_ref, b_scr, sd_scr, at_scr):
    u_ref, k_ref, qt_ref, lat_ref, vt_ref, rt_ref = in_refs
    (ubd_ref, uprev_ref, slot_ref, dmask_ref, causal_ref, eye_ref, gn_ref, cnt_ref,
     wpool_ref, pscale_ref) = const_refs
    C = GLA_CHUNK
    lane = lax.broadcasted_iota(jnp.int32, (SC_ROWS, SC_ROWS), 1)
    rowi = lax.broadcasted_iota(jnp.int32, (SC_ROWS, SC_ROWS), 0)
    lane64 = lax.broadcasted_iota(jnp.int32, (C, SC_ROWS), 1)
    lane8 = lax.broadcasted_iota(jnp.int32, (8, SC_ROWS), 1)
    lane_lo = lane < C
    row_lo = rowi < C
    row_head = lax.broadcasted_iota(jnp.int32, (GLA_KEY_WIDTH, SC_ROWS), 0) // GLA_DK

    def heads_on_lanes(xt):
        return jnp.concatenate([jnp.where(row_head == h, xt, 0.0) for h in range(GLA_HEADS)], axis=1).astype(BF16)

    for c in range(n):
        for b in range(BATCH):
            ext_ref[b, SUB + C * c:SUB + C * (c + 1), :] = u_ref[SC_ROWS * c + C * b:SC_ROWS * c + C * (b + 1), :]
    states = [st_ref[h] for h in range(GLA_HEADS)]
    cnt = cnt_ref[...]
    ubd = ubd_ref[...]
    eye = eye_ref[...]
    gn = gn_ref[...]

    def chunk(c):
        rows = slice(SC_ROWS * c, SC_ROWS * (c + 1))

        u = u_ref[rows, :]
        parts = []
        for g, w in enumerate(POOL_WINDOWS):
            cols = slice(POOL_GROUP_DIM * g, POOL_GROUP_DIM * (g + 1))
            sums = []
            for b in range(BATCH):
                r0 = SUB + C * c
                s = ext_ref[b, r0:r0 + C, cols]
                for back in range(1, w):
                    s = s + ext_ref[b, r0 - back:r0 + C - back, cols]
                sums.append(s)
            s = jnp.concatenate(sums, axis=0)
            p = s / cnt[:, cols] - u[:, cols]
            parts.append(_dot(p.astype(BF16), wpool_ref[g]))
        y_pool = jnp.concatenate(parts, axis=1) * pscale_ref[...]

        bT = _dot3_right(lat_ref[:, rows], ubd)
        yield
        b_tm = _transpose_cm(bT)
        b_scr[c] = b_tm
        ends = [[b_scr[c, pl.ds(C * b + SUB * j + SUB - 1, 1), :] for j in range(N_SUB)] for b in range(BATCH)]

        def per_block(fn):
            return jnp.concatenate(
                [jnp.broadcast_to(fn(b, j), (SUB, GLA_KEY_WIDTH)) for b in range(BATCH) for j in range(N_SUB)], axis=0)

        k_tm = k_ref[rows, :]
        qT = qt_ref[:, rows]
        e_last = per_block(lambda b, j: ends[b][N_SUB - 1])
        ktil = k_tm * jnp.exp(e_last - b_tm)
        qeT = qT * jnp.exp(bT)

        if not robust:
            kneg = (k_tm * jnp.exp(-b_tm)).astype(BF16)
            at_all = jnp.where(causal_ref[...] != 0, _dot(kneg, heads_on_lanes(qeT)), 0.0)
        else:
            cprevT = _dot3_right(lat_ref[:, rows], uprev_ref[...])
            e_own = per_block(lambda b, j: ends[b][j])
            f2 = per_block(lambda b, j: jnp.exp(ends[b][min(j + 1, N_SUB - 1)] - ends[b][j]))
            f3 = per_block(lambda b, j: jnp.exp(ends[b][min(j + 2, N_SUB - 1)] - ends[b][j]))
            khat = k_tm * jnp.exp(e_own - b_tm)
            kslots = jnp.concatenate([khat, khat * f2, khat * f3], axis=0).astype(BF16)
            r_all = _dot(kslots, heads_on_lanes(qT * jnp.exp(bT - cprevT)))
            slot = slot_ref[...]
            at_off = jnp.where(slot == 1, r_all[0:SC_ROWS],
                               jnp.where(slot == 2, r_all[SC_ROWS:2 * SC_ROWS],
                                         jnp.where(slot == 3, r_all[2 * SC_ROWS:], 0.0)))
            kT = jnp.concatenate([k_tm[:, :LANES].T, k_tm[:, LANES:].T], axis=0)
            for dist in range(SUB):
                if dist == 0:
                    qs, bs = qT, bT
                else:
                    qs = pltpu.roll(qT, SC_ROWS - dist, axis=1)
                    bs = pltpu.roll(bT, SC_ROWS - dist, axis=1)
                prod = qs * kT * jnp.exp(jnp.minimum(bs - bT, 0.0))
                for h in range(GLA_HEADS):
                    sd_scr[pl.ds(SUB * h + dist, 1), :] = jnp.sum(
                        prod[GLA_DK * h:GLA_DK * (h + 1)], axis=0, keepdims=True)
            sd = jnp.concatenate([sd_scr[...], jnp.zeros((SC_ROWS - GLA_HEADS * SUB, SC_ROWS), F32)], axis=0)
            sdt = sd.T
            dmask = dmask_ref[...]
            for h in range(GLA_HEADS):
                xh = jnp.where((lane >= SUB * h) & (lane < SUB * (h + 1)), sdt, 0.0)
                skew = pltpu.roll(xh, (SC_ROWS - SUB * h) % SC_ROWS, axis=1, stride=1, stride_axis=0)
                at_scr[:, SC_ROWS * h:SC_ROWS * (h + 1)] = (
                    jnp.where(dmask != 0, skew, 0.0) + at_off[:, SC_ROWS * h:SC_ROWS * (h + 1)])
            at_all = at_scr[...]
        yield

        vT = vt_ref[:, rows]
        rT = rt_ref[:, rows]
        y_heads = []
        for h in range(GLA_HEADS):
            at_h = at_all[:, SC_ROWS * h:SC_ROWS * (h + 1)]

            half = (h % 2) * C
            kt_cols = ktil[:, LANES * (h // 2):LANES * (h // 2 + 1)]
            kt_roll = pltpu.roll(kt_cols, C, axis=1)
            lo_src, hi_src = (kt_cols, kt_roll) if half == 0 else (kt_roll, kt_cols)
            kbd = jnp.where(row_lo & lane_lo, lo_src, jnp.where((~row_lo) & (~lane_lo), hi_src, 0.0))
            qe_h = qeT[GLA_DK * h:GLA_DK * (h + 1)]
            qebd = jnp.concatenate([jnp.where(lane64 < C, qe_h, 0.0), jnp.where(lane64 >= C, qe_h, 0.0)], axis=0)

            v_h = vT[GLA_DV * h:GLA_DV * (h + 1)]
            res = _dot(v_h, jnp.concatenate([at_h, kbd], axis=1).astype(BF16))
            st = states[h]
            o_t = res[:, :SC_ROWS] + _dot(st.astype(BF16), qebd.astype(BF16))

            e0 = jnp.broadcast_to(ends[0][N_SUB - 1][:, LANES * (h // 2):LANES * (h // 2 + 1)], (8, LANES))
            e1 = jnp.broadcast_to(ends[1][N_SUB - 1][:, LANES * (h // 2):LANES * (h // 2 + 1)], (8, LANES))
            if half == 0:
                e1 = pltpu.roll(e1, C, axis=1)
            else:
                e0 = pltpu.roll(e0, C, axis=1)
            dec = jnp.exp(jnp.where(lane8 < C, e0, e1))
            states[h] = st * jnp.broadcast_to(dec[0:1], (GLA_DV, SC_ROWS)) + res[:, SC_ROWS:]

            ms = jnp.mean(o_t * o_t, axis=0, keepdims=True)
            gsl = slice(GLA_DV * h, GLA_DV * (h + 1))
            y_heads.append(o_t * lax.rsqrt(ms + RMS_EPS) * gn[gsl] * rT[gsl])
        y_t = jnp.concatenate(y_heads, axis=0).astype(BF16)
        y_gla = _dot_nt(eye, y_t)
        yield
        y_ref[rows, :] = jnp.concatenate([y_pool, y_gla], axis=1).astype(BF16)

    def in_turn():
        for c in range(n):
            yield from chunk(c)

    def finish():
        for h in range(GLA_HEADS):
            st_new_ref[h] = states[h]

    return _Stagger([in_turn()] if robust else [chunk(c) for c in range(n)]), finish


def _outproj_ffn_rows(alpha, h, y, store, wo_ref, l2g_ref, l2b_ref, wg_ref, wu_ref, wd_ref, l3g_ref, l3b_ref):
    z = alpha * h + _dot(y, wo_ref[...])
    yield
    h2 = _layer_norm(z, l2g_ref[...], l2b_ref[...])
    yield
    z2 = yield from _swiglu_half_step(h2, wg_ref, wu_ref, wd_ref, alpha)
    store(_layer_norm(z2, l3g_ref[...], l3b_ref[...]))


def _mixer_kernel(tc, flags_ref, *refs):
    in_refs, const_refs, y_ref = refs[:N_MIX_IN], refs[N_MIX_IN:N_MIX_IN + N_MIX_CONST], refs[N_MIX_IN + N_MIX_CONST]
    st_ref, ext_ref, b_scr, sd_scr, at_scr = refs[N_MIX_IN + N_MIX_CONST + 1:]
    step = pl.program_id(0)

    def run(n, robust):
        phases, finish = _mix_chunks(n, robust, in_refs, const_refs, y_ref, st_ref, st_ref, ext_ref,
                                     b_scr, sd_scr, at_scr)
        _run_phases(phases)
        finish()
        for b in range(BATCH):
            ext_ref[b, 0:SUB, :] = ext_ref[b, GLA_CHUNK * n:GLA_CHUNK * n + SUB, :]

    @pl.when(step == 0)
    def _():
        st_ref[...] = jnp.zeros(st_ref.shape, F32)
        ext_ref[:, 0:SUB, :] = jnp.zeros((BATCH, SUB, POOL_WIDTH), F32)
        run(1, True)

    @pl.when(step > 0)
    def _():
        base = (step - 1) * tc
        slow = flags_ref[base]
        for c in range(1, tc):
            slow = jnp.maximum(slow, flags_ref[base + c])

        @pl.when(slow == 0)
        def _():
            run(tc, False)

        @pl.when(slow != 0)
        def _():
            run(tc, True)


def _outproj_ffn_kernel(alpha, to_out, h_ref, y_ref, *refs):
    *w_refs, o_ref = refs
    tile_rows = h_ref.shape[0]

    def store_rows(r0):
        def store(o):
            o_ref[r0:r0 + o.shape[0], :] = o
        return store

    def store_chunks(r0):
        def store(o):
            for sb in range(o.shape[0] // GLA_CHUNK):
                s, b = divmod(r0 // GLA_CHUNK + sb, BATCH)
                o_ref[b, s] = o[sb * GLA_CHUNK:(sb + 1) * GLA_CHUNK]
        return store

    def parts(n_rows, n_parts, make_store):
        pr = n_rows // n_parts
        _staggered([_outproj_ffn_rows(alpha, h_ref[p * pr:(p + 1) * pr, :], y_ref[p * pr:(p + 1) * pr, :],
                                      make_store(p * pr), *w_refs) for p in range(n_parts)])

    if to_out:
        parts(tile_rows, ROW_PARTS, store_chunks)
        return
    last = pl.num_programs(0) - 1

    @pl.when(pl.program_id(0) < last)
    def _():
        parts(tile_rows, ROW_PARTS, store_rows)

    @pl.when(pl.program_id(0) == last)
    def _():
        parts(SC_ROWS, 1, store_rows)


def _resident(shape):
    nd = len(shape)
    return pl.BlockSpec(shape, lambda *_: (0,) * nd, pipeline_mode=pl.Buffered(1))


def _layer_resident(shape, layer):
    nd = len(shape) - 1
    return pl.BlockSpec((None,) + tuple(shape[1:]), lambda *_: (layer,) + (0,) * nd, pipeline_mode=pl.Buffered(1))


def _mixer_constants():
    r = np.arange(SC_ROWS)
    b, t = r // GLA_CHUNK, r % GLA_CHUNK
    same_b = b[:, None] == b[None, :]
    ubd = same_b & (t[:, None] <= t[None, :])
    blk = t // SUB
    uprev = same_b & (blk[:, None] < blk[None, :])
    dist = np.where(same_b, blk[None, :] - blk[:, None], 0)
    slot = np.where((dist >= 1) & (dist < N_SUB), dist, 0).astype(np.int32)
    slot = np.tile(slot, (1, GLA_HEADS))
    dmask = (same_b & (blk[:, None] == blk[None, :]) & (t[None, :] >= t[:, None])).astype(np.int32)
    causal = np.tile((same_b & (t[None, :] >= t[:, None])).astype(np.int32), (1, GLA_HEADS))
    eye = np.eye(SC_ROWS)
    w_lane = np.repeat(np.array(POOL_WINDOWS, np.float32), POOL_GROUP_DIM)[None, :]
    t_meta = np.maximum(t - (GLA_CHUNK - N_META), 0).astype(np.float32)[:, None]
    cnt = np.stack([np.broadcast_to(w_lane, (SC_ROWS, POOL_WIDTH)), np.minimum(t_meta + 1.0, w_lane)])
    return (jnp.asarray(ubd, BF16), jnp.asarray(uprev, BF16), jnp.asarray(slot),
            jnp.asarray(dmask), jnp.asarray(causal), jnp.asarray(eye, BF16), jnp.asarray(cnt, F32))


def _tile_chunks(n_ch):
    return next(k for k in (4, 2, 1) if n_ch % k == 0)


def kernel(x, meta_tokens, ffn1_w_gate, ffn1_w_up, ffn1_w_down, ln1_g, ln1_b, w_in, w_gate_up, b_gate, w_pool, pool_scale, gla_norm_g, w_out, ln2_g, ln2_b, ffn2_w_gate, ffn2_w_up, ffn2_w_down, ln3_g, ln3_b):
    batch, seq, d = x.shape
    depth = w_in.shape[0]
    assert batch == BATCH and d == D_MODEL and seq % GLA_CHUNK == 0
    n_ch = seq // GLA_CHUNK
    n_sc = n_ch + 1
    n_rows = n_sc * SC_ROWS
    tc = _tile_chunks(n_ch)
    tr = tc * SC_ROWS
    n_main = n_ch // tc
    alpha = (2.0 * depth) ** 0.25

    x4 = x.reshape(batch, n_ch, GLA_CHUNK, d)
    meta_half = jnp.concatenate([jnp.zeros((GLA_CHUNK - N_META, d), x.dtype), meta_tokens.astype(x.dtype)], axis=0)
    meta_sc = jnp.concatenate([meta_half, meta_half], axis=0)
    h = None

    ubd, uprev, slot, dmask, causal, eye, cnt = _mixer_constants()
    s0 = POOL_WIDTH
    s1 = s0 + GLA_KEY_WIDTH
    s2 = s1 + GLA_KEY_WIDTH
    s3 = s2 + GLA_WIDTH
    s4 = s3 + GLA_WIDTH
    cparams = pltpu.CompilerParams(dimension_semantics=("arbitrary",), vmem_limit_bytes=VMEM_LIMIT_BYTES)
    row_tile = lambda cols: pl.BlockSpec((tr, cols), lambda i: (i, 0))
    col_tile = lambda rows: pl.BlockSpec((rows, tr), lambda i: (0, i))
    x4_tile = pl.BlockSpec((batch, tc, GLA_CHUNK, d), lambda i: (0, jnp.minimum(i, n_main - 1), 0, 0))
    mix_tile = lambda i: lax.rem(i + n_main, n_main + 1)
    mix_rows = lambda cols: pl.BlockSpec((tr, cols), lambda i, f: (mix_tile(i), 0))
    mix_cols = lambda rows: pl.BlockSpec((rows, tr), lambda i, f: (0, mix_tile(i)))

    bf = lambda w: w.astype(BF16)
    w1g, w1u, w1d = bf(ffn1_w_gate), bf(ffn1_w_up), bf(ffn1_w_down)
    w2g, w2u, w2d = bf(ffn2_w_gate), bf(ffn2_w_up), bf(ffn2_w_down)
    wo_all, wpool_all = bf(w_out), bf(w_pool)
    row3 = lambda p: p.reshape(depth, 1, p.shape[-1])
    l1g, l1b, l2g, l2b, l3g, l3b = (row3(p) for p in (ln1_g, ln1_b, ln2_g, ln2_b, ln3_g, ln3_b))
    pscale_all = row3(pool_scale)

    for l in range(depth):
        wl = w_in[l]
        glr_pad = jnp.zeros((d, LANES - GLA_GATE_RANK), F32)
        wtm = jnp.concatenate([wl[:, :s0], wl[:, s1:s2]], axis=1).astype(BF16)
        wcm = jnp.concatenate([wl[:, s0:s1] * (GLA_DK ** -0.5), wl[:, s2:s3], wl[:, s3:s4]], axis=1).T.astype(BF16)
        wgt = jnp.concatenate([wl[:, s4:], glr_pad], axis=1).T.astype(BF16)
        wgut = jnp.concatenate([w_gate_up[l], jnp.zeros((LANES - GLA_GATE_RANK, GLA_KEY_WIDTH), F32)], axis=0).T.astype(BF16)
        bgc = b_gate[l].reshape(GLA_KEY_WIDTH, 1)

        from_x = l == 0
        acts = (x4, meta_sc) if from_x else (h,)
        act_specs = [x4_tile, _resident(meta_sc.shape)] if from_x else [row_tile(d)]
        outs = pl.pallas_call(
            functools.partial(_ffn_inproj_kernel, alpha, from_x, tr),
            grid=(n_main + 1,),
            in_specs=act_specs + [_layer_resident(w1g.shape, l), _layer_resident(w1u.shape, l),
                                  _layer_resident(w1d.shape, l), _layer_resident(l1g.shape, l),
                                  _layer_resident(l1b.shape, l),
                                  _resident(wtm.shape), _resident(wcm.shape), _resident(wgt.shape),
                                  _resident(wgut.shape), _resident(bgc.shape)],
            out_specs=[row_tile(d), row_tile(POOL_WIDTH), row_tile(GLA_KEY_WIDTH),
                       col_tile(GLA_KEY_WIDTH), col_tile(GLA_KEY_WIDTH), col_tile(GLA_WIDTH), col_tile(GLA_WIDTH),
                       pl.BlockSpec((FLAG_ROWS, LANES), lambda i: (i, 0))],
            out_shape=[jax.ShapeDtypeStruct((n_rows, d), F32),
                       jax.ShapeDtypeStruct((n_rows, POOL_WIDTH), F32),
                       jax.ShapeDtypeStruct((n_rows, GLA_KEY_WIDTH), F32),
                       jax.ShapeDtypeStruct((GLA_KEY_WIDTH, n_rows), F32),
                       jax.ShapeDtypeStruct((GLA_KEY_WIDTH, n_rows), F32),
                       jax.ShapeDtypeStruct((GLA_WIDTH, n_rows), BF16),
                       jax.ShapeDtypeStruct((GLA_WIDTH, n_rows), F32),
                       jax.ShapeDtypeStruct(((n_main + 1) * FLAG_ROWS, LANES), jnp.int32)],
            compiler_params=cparams,
            name=f"ffn_inproj_{l}",
        )(*acts, w1g, w1u, w1d, l1g, l1b, wtm, wcm, wgt, wgut, bgc)
        h1, u_tm, k_tm, q_t, la_t, v_t, r_t, tile_flags = outs
        sc_flags = tile_flags.reshape(n_main + 1, FLAG_ROWS, LANES)[:, :tc, 0].reshape(-1)[:n_sc]

        gn = jnp.broadcast_to(gla_norm_g[l].reshape(GLA_WIDTH, 1), (GLA_WIDTH, SC_ROWS))
        mix_in = (u_tm, k_tm, q_t, la_t, v_t, r_t)
        mix_in_specs = [mix_rows(POOL_WIDTH), mix_rows(GLA_KEY_WIDTH), mix_cols(GLA_KEY_WIDTH),
                        mix_cols(GLA_KEY_WIDTH), mix_cols(GLA_WIDTH), mix_cols(GLA_WIDTH)]
        mix_const = (ubd, uprev, slot, dmask, causal, eye, gn, cnt, wpool_all, pscale_all)
        mix_const_specs = [_resident(ubd.shape), _resident(uprev.shape), _resident(slot.shape),
                           _resident(dmask.shape), _resident(causal.shape), _resident(eye.shape), _resident(gn.shape),
                           pl.BlockSpec((None, SC_ROWS, POOL_WIDTH), lambda i, f: (jnp.where(i == 0, 1, 0), 0, 0)),
                           _layer_resident(wpool_all.shape, l), _layer_resident(pscale_all.shape, l)]
        assert len(mix_in) == N_MIX_IN and len(mix_const) == N_MIX_CONST
        y_cat = pl.pallas_call(
            functools.partial(_mixer_kernel, tc),
            grid_spec=pltpu.PrefetchScalarGridSpec(
                num_scalar_prefetch=1,
                grid=(n_main + 1,),
                in_specs=mix_in_specs + mix_const_specs,
                out_specs=mix_rows(d),
                scratch_shapes=[pltpu.VMEM((GLA_HEADS, GLA_DV, SC_ROWS), F32),
                                pltpu.VMEM((BATCH, SUB + tc * GLA_CHUNK, POOL_WIDTH), F32),
                                pltpu.VMEM((tc, SC_ROWS, GLA_KEY_WIDTH), F32),
                                pltpu.VMEM((GLA_HEADS * SUB, SC_ROWS), F32),
                                pltpu.VMEM((SC_ROWS, GLA_HEADS * SC_ROWS), F32)]),
            out_shape=jax.ShapeDtypeStruct((n_rows, d), BF16),
            compiler_params=cparams,
            name=f"mixer_{l}",
        )(sc_flags, *mix_in, *mix_const)

        to_out = l == depth - 1
        h = pl.pallas_call(
            functools.partial(_outproj_ffn_kernel, alpha, to_out),
            grid=(n_main if to_out else n_main + 1,),
            in_specs=[row_tile(d), row_tile(d), _layer_resident(wo_all.shape, l),
                      _layer_resident(l2g.shape, l), _layer_resident(l2b.shape, l),
                      _layer_resident(w2g.shape, l), _layer_resident(w2u.shape, l), _layer_resident(w2d.shape, l),
                      _layer_resident(l3g.shape, l), _layer_resident(l3b.shape, l)],
            out_specs=pl.BlockSpec((batch, tc, GLA_CHUNK, d), lambda i: (0, i, 0, 0)) if to_out else row_tile(d),
            out_shape=jax.ShapeDtypeStruct(x4.shape if to_out else (n_rows, d), F32),
            compiler_params=cparams,
            name=f"outproj_ffn_{l}",
        )(h1, y_cat, wo_all, l2g, l2b, w2g, w2u, w2d, l3g, l3b)

    return h.reshape(batch, seq, d)
```

```python
import functools

import numpy as np
import jax
import jax.numpy as jnp
from jax import lax
from jax.experimental import pallas as pl
from jax.experimental.pallas import tpu as pltpu

D_MODEL = 1024
N_META = 16
POOL_WIDTH = 512
POOL_GROUPS = 4
POOL_GROUP_DIM = 128
POOL_WINDOWS = (2, 4, 8, 16)
GLA_WIDTH = 512
GLA_HEADS = 4
GLA_KEY_WIDTH = 256
GLA_DK = 64
GLA_DV = 128
GLA_GATE_RANK = 16
GLA_GATE_TEMP = 16.0
GLA_CHUNK = 64
D_FF = 2816
LN_EPS = 1e-5
RMS_EPS = 1e-6

BATCH = 2
SC_ROWS = BATCH * GLA_CHUNK
SUB = 16
N_SUB = GLA_CHUNK // SUB
LANES = 128
FF_SPLIT = 1
ROW_PARTS = 2
STAGGER_LAG = 1
VMEM_LIMIT_BYTES = 60 * 1024 * 1024
FAST_PATH_MAX_DECAY = 40.0
FLAG_ROWS = 8
N_MIX_IN = 6
N_MIX_CONST = 10

F32 = jnp.float32
BF16 = jnp.bfloat16


def _dot(a, b):
    return jnp.dot(a, b, preferred_element_type=F32)


def _dot_nt(a, b):
    return lax.dot_general(a, b, (((1,), (1,)), ((), ())), preferred_element_type=F32)


def _layer_norm(z, g, b):
    mu = jnp.mean(z, axis=-1, keepdims=True)
    zc = z - mu
    var = jnp.mean(zc * zc, axis=-1, keepdims=True)
    return zc * lax.rsqrt(var + LN_EPS) * g + b


def _silu(x):
    return x * jax.nn.sigmoid(x)


def _log_sigmoid(x):
    return jnp.minimum(x, 0.0) - jnp.log1p(jnp.exp(-jnp.abs(x)))


def _swiglu_half_step(x, wg_ref, wu_ref, wd_ref, alpha):
    xb = x.astype(BF16)
    fc = D_FF // FF_SPLIT
    y = None
    for c in range(FF_SPLIT):
        cols = slice(c * fc, (c + 1) * fc)
        g = _dot(xb, wg_ref[:, cols])
        u = _dot(xb, wu_ref[:, cols])
        yield
        act = (_silu(g) * u).astype(BF16)
        yield
        part = _dot(act, wd_ref[cols, :])
        y = part if y is None else y + part
    yield
    return alpha * x + 0.5 * y


class _Stagger:
    def __init__(self, gens, lag=STAGGER_LAG):
        self.gens = list(gens)
        self.lag = lag
        self.t = 0
        self.done = [False] * len(self.gens)
        self.results = [None] * len(self.gens)

    def tick(self):
        for k, gen in enumerate(self.gens):
            if self.done[k] or self.t < k * self.lag:
                continue
            try:
                next(gen)
            except StopIteration as stop:
                self.results[k], self.done[k] = stop.value, True
        self.t += 1
        return not all(self.done)


def _run_phases(*staggers):
    while any([s.tick() for s in staggers]):
        pass


def _staggered(gens, lag=STAGGER_LAG):
    s = _Stagger(gens, lag)
    _run_phases(s)
    return s.results


def _ffn_inproj_rows(alpha, x, r0, is_meta, wg_ref, wu_ref, wd_ref, lng_ref, lnb_ref,
                     wtm_ref, wcm_ref, wgt_ref, wgut_ref, bgc_ref,
                     h_ref, u_ref, k_ref, qt_ref, lat_ref, vt_ref, rt_ref, flag_ref):
    n = x.shape[0]
    rows = slice(r0, r0 + n)
    z = yield from _swiglu_half_step(x, wg_ref, wu_ref, wd_ref, alpha)
    h = _layer_norm(z, lng_ref[...], lnb_ref[...])
    h_ref[rows, :] = h
    hb = h.astype(BF16)
    yield

    if is_meta:
        pad = GLA_CHUNK - N_META
        keep_r = (lax.broadcasted_iota(jnp.int32, (n, 1), 0) % GLA_CHUNK) >= pad
        keep_l = (lax.broadcasted_iota(jnp.int32, (1, n), 1) % GLA_CHUNK) >= pad
        mask_r = lambda v: jnp.where(keep_r, v, 0.0)
        mask_l = lambda v: jnp.where(keep_l, v, 0.0)
    else:
        mask_r = mask_l = lambda v: v

    ztm = _dot(hb, wtm_ref[...])
    u_ref[rows, :] = mask_r(ztm[:, :POOL_WIDTH])
    k_ref[rows, :] = mask_r(ztm[:, POOL_WIDTH:])
    yield

    zt = _dot_nt(wcm_ref[...], hb)
    kw = GLA_KEY_WIDTH
    qt_ref[:, rows] = mask_l(zt[:kw])
    vt_ref[:, rows] = mask_l(zt[kw:kw + GLA_WIDTH]).astype(BF16)
    rt_ref[:, rows] = _silu(zt[kw + GLA_WIDTH:])
    gt = _dot_nt(wgt_ref[...], hb).astype(BF16)
    lat = mask_l(_log_sigmoid(_dot(wgut_ref[...], gt) + bgc_ref[...]) * (1.0 / GLA_GATE_TEMP))
    lat_ref[:, rows] = lat

    flags = []
    lane_lo = lax.broadcasted_iota(jnp.int32, (GLA_KEY_WIDTH, SC_ROWS), 1) < GLA_CHUNK
    for s in range(n // SC_ROWS):
        seg = lat[:, SC_ROWS * s:SC_ROWS * (s + 1)]
        tot = jnp.minimum(jnp.sum(jnp.where(lane_lo, seg, 0.0), axis=1, keepdims=True),
                          jnp.sum(jnp.where(lane_lo, 0.0, seg), axis=1, keepdims=True))
        flags.append((jnp.min(tot, axis=0, keepdims=True) < -FAST_PATH_MAX_DECAY).astype(jnp.int32))
    return flags


def _ffn_inproj_kernel(alpha, from_x, tile_rows, x_ref, *refs):
    if from_x:
        meta_ref, *refs = refs
    flag_ref = refs[-1]
    last = pl.num_programs(0) - 1

    def x_rows(r0, n):
        if not from_x:
            return x_ref[r0:r0 + n, :]
        pieces = [divmod(r0 // GLA_CHUNK + i, BATCH) for i in range(n // GLA_CHUNK)]
        return jnp.concatenate([x_ref[b, s] for s, b in pieces], axis=0)

    def run(parts, is_meta):
        gens = [_ffn_inproj_rows(alpha, x, r0, is_meta, *refs) for x, r0 in parts]
        sc_flags = [f for fl in _staggered(gens, STAGGER_LAG) for f in fl]
        frow = lax.broadcasted_iota(jnp.int32, (FLAG_ROWS, LANES), 0)
        flags = jnp.zeros((FLAG_ROWS, LANES), jnp.int32)
        for s, f in enumerate(sc_flags):
            flags = jnp.where(frow == s, f, flags)
        flag_ref[...] = flags

    @pl.when(pl.program_id(0) < last)
    def _():
        pr = tile_rows // ROW_PARTS
        run([(x_rows(p * pr, pr), p * pr) for p in range(ROW_PARTS)], False)

    @pl.when(pl.program_id(0) == last)
    def _():
        run([(meta_ref[...] if from_x else x_ref[0:SC_ROWS, :], 0)], True)


def _split3(x):
    hi = x.astype(BF16)
    r1 = x - hi.astype(F32)
    mid = r1.astype(BF16)
    lo = (r1 - mid.astype(F32)).astype(BF16)
    return hi, mid, lo


def _transpose_cm(xt):
    return jnp.concatenate([xt[:SC_ROWS].T, xt[SC_ROWS:].T], axis=1)


def _dot3_right(x, c01):
    parts = _split3(x)
    return _dot(parts[0], c01) + _dot(parts[1], c01) + _dot(parts[2], c01)


def _mix_chunks(n, robust, in_refs, const_refs, y_ref, st_ref, st_new_ref, ext_ref, b_scr, sd_scr, at_scr):
    u_ref, k_ref, qt_ref, lat_ref, vt_ref, rt_ref = in_refs
    (ubd_ref, uprev_ref, slot_ref, dmask_ref, causal_ref, eye_ref, gn_ref, cnt_ref,
     wpool_ref, pscale_ref) = const_refs
    C = GLA_CHUNK
    lane = lax.broadcasted_iota(jnp.int32, (SC_ROWS, SC_ROWS), 1)
    rowi = lax.broadcasted_iota(jnp.int32, (SC_ROWS, SC_ROWS), 0)
    lane64 = lax.broadcasted_iota(jnp.int32, (C, SC_ROWS), 1)
    lane8 = lax.broadcasted_iota(jnp.int32, (8, SC_ROWS), 1)
    lane_lo = lane < C
    row_lo = rowi < C
    row_head = lax.broadcasted_iota(jnp.int32, (GLA_KEY_WIDTH, SC_ROWS), 0) // GLA_DK

    def heads_on_lanes(xt):
        return jnp.concatenate([jnp.where(row_head == h, xt, 0.0) for h in range(GLA_HEADS)], axis=1).astype(BF16)

    for c in range(n):
        for b in range(BATCH):
            ext_ref[b, SUB + C * c:SUB + C * (c + 1), :] = u_ref[SC_ROWS * c + C * b:SC_ROWS * c + C * (b + 1), :]
    states = [st_ref[h] for h in range(GLA_HEADS)]
    cnt = cnt_ref[...]
    ubd = ubd_ref[...]
    eye = eye_ref[...]
    gn = gn_ref[...]

    def chunk(c):
        rows = slice(SC_ROWS * c, SC_ROWS * (c + 1))

        u = u_ref[rows, :]
        parts = []
        for g, w in enumerate(POOL_WINDOWS):
            cols = slice(POOL_GROUP_DIM * g, POOL_GROUP_DIM * (g + 1))
            sums = []
            for b in range(BATCH):
                r0 = SUB + C * c
                s = ext_ref[b, r0:r0 + C, cols]
                for back in range(1, w):
                    s = s + ext_ref[b, r0 - back:r0 + C - back, cols]
                sums.append(s)
            s = jnp.concatenate(sums, axis=0)
            p = s / cnt[:, cols] - u[:, cols]
            parts.append(_dot(p.astype(BF16), wpool_ref[g]))
        y_pool = jnp.concatenate(parts, axis=1) * pscale_ref[...]

        bT = _dot3_right(lat_ref[:, rows], ubd)
        yield
        b_tm = _transpose_cm(bT)
        b_scr[c] = b_tm
        ends = [[b_scr[c, pl.ds(C * b + SUB * j + SUB - 1, 1), :] for j in range(N_SUB)] for b in range(BATCH)]

        def per_block(fn):
            return jnp.concatenate(
                [jnp.broadcast_to(fn(b, j), (SUB, GLA_KEY_WIDTH)) for b in range(BATCH) for j in range(N_SUB)], axis=0)

        k_tm = k_ref[rows, :]
        qT = qt_ref[:, rows]
        e_last = per_block(lambda b, j: ends[b][N_SUB - 1])
        ktil = k_tm * jnp.exp(e_last - b_tm)
        qeT = qT * jnp.exp(bT)

        if not robust:
            kneg = (k_tm * jnp.exp(-b_tm)).astype(BF16)
            at_all = jnp.where(causal_ref[...] != 0, _dot(kneg, heads_on_lanes(qeT)), 0.0)
        else:
            cprevT = _dot3_right(lat_ref[:, rows], uprev_ref[...])
            e_own = per_block(lambda b, j: ends[b][j])
            f2 = per_block(lambda b, j: jnp.exp(ends[b][min(j + 1, N_SUB - 1)] - ends[b][j]))
            f3 = per_block(lambda b, j: jnp.exp(ends[b][min(j + 2, N_SUB - 1)] - ends[b][j]))
            khat = k_tm * jnp.exp(e_own - b_tm)
            kslots = jnp.concatenate([khat, khat * f2, khat * f3], axis=0).astype(BF16)
            r_all = _dot(kslots, heads_on_lanes(qT * jnp.exp(bT - cprevT)))
            slot = slot_ref[...]
            at_off = jnp.where(slot == 1, r_all[0:SC_ROWS],
                               jnp.where(slot == 2, r_all[SC_ROWS:2 * SC_ROWS],
                                         jnp.where(slot == 3, r_all[2 * SC_ROWS:], 0.0)))
            kT = jnp.concatenate([k_tm[:, :LANES].T, k_tm[:, LANES:].T], axis=0)
            for dist in range(SUB):
                if dist == 0:
                    qs, bs = qT, bT
                else:
                    qs = pltpu.roll(qT, SC_ROWS - dist, axis=1)
                    bs = pltpu.roll(bT, SC_ROWS - dist, axis=1)
                prod = qs * kT * jnp.exp(jnp.minimum(bs - bT, 0.0))
                for h in range(GLA_HEADS):
                    sd_scr[pl.ds(SUB * h + dist, 1), :] = jnp.sum(
                        prod[GLA_DK * h:GLA_DK * (h + 1)], axis=0, keepdims=True)
            sd = jnp.concatenate([sd_scr[...], jnp.zeros((SC_ROWS - GLA_HEADS * SUB, SC_ROWS), F32)], axis=0)
            sdt = sd.T
            dmask = dmask_ref[...]
            for h in range(GLA_HEADS):
                xh = jnp.where((lane >= SUB * h) & (lane < SUB * (h + 1)), sdt, 0.0)
                skew = pltpu.roll(xh, (SC_ROWS - SUB * h) % SC_ROWS, axis=1, stride=1, stride_axis=0)
                at_scr[:, SC_ROWS * h:SC_ROWS * (h + 1)] = (
                    jnp.where(dmask != 0, skew, 0.0) + at_off[:, SC_ROWS * h:SC_ROWS * (h + 1)])
            at_all = at_scr[...]
        yield

        vT = vt_ref[:, rows]
        rT = rt_ref[:, rows]
        y_heads = []
        for h in range(GLA_HEADS):
            at_h = at_all[:, SC_ROWS * h:SC_ROWS * (h + 1)]

            half = (h % 2) * C
            kt_cols = ktil[:, LANES * (h // 2):LANES * (h // 2 + 1)]
            kt_roll = pltpu.roll(kt_cols, C, axis=1)
            lo_src, hi_src = (kt_cols, kt_roll) if half == 0 else (kt_roll, kt_cols)
            kbd = jnp.where(row_lo & lane_lo, lo_src, jnp.where((~row_lo) & (~lane_lo), hi_src, 0.0))
            qe_h = qeT[GLA_DK * h:GLA_DK * (h + 1)]
            qebd = jnp.concatenate([jnp.where(lane64 < C, qe_h, 0.0), jnp.where(lane64 >= C, qe_h, 0.0)], axis=0)

            v_h = vT[GLA_DV * h:GLA_DV * (h + 1)]
            res = _dot(v_h, jnp.concatenate([at_h, kbd], axis=1).astype(BF16))
            st = states[h]
            o_t = res[:, :SC_ROWS] + _dot(st.astype(BF16), qebd.astype(BF16))

            e0 = jnp.broadcast_to(ends[0][N_SUB - 1][:, LANES * (h // 2):LANES * (h // 2 + 1)], (8, LANES))
            e1 = jnp.broadcast_to(ends[1][N_SUB - 1][:, LANES * (h // 2):LANES * (h // 2 + 1)], (8, LANES))
            if half == 0:
                e1 = pltpu.roll(e1, C, axis=1)
            else:
                e0 = pltpu.roll(e0, C, axis=1)
            dec = jnp.exp(jnp.where(lane8 < C, e0, e1))
            states[h] = st * jnp.broadcast_to(dec[0:1], (GLA_DV, SC_ROWS)) + res[:, SC_ROWS:]

            ms = jnp.mean(o_t * o_t, axis=0, keepdims=True)
            gsl = slice(GLA_DV * h, GLA_DV * (h + 1))
            y_heads.append(o_t * lax.rsqrt(ms + RMS_EPS) * gn[gsl] * rT[gsl])
        y_t = jnp.concatenate(y_heads, axis=0).astype(BF16)
        y_gla = _dot_nt(eye, y_t)
        yield
        y_ref[rows, :] = jnp.concatenate([y_pool, y_gla], axis=1).astype(BF16)

    def in_turn():
        for c in range(n):
            yield from chunk(c)

    def finish():
        for h in range(GLA_HEADS):
            st_new_ref[h] = states[h]

    return _Stagger([in_turn()] if robust else [chunk(c) for c in range(n)]), finish


def _outproj_ffn_rows(alpha, h, y, store, wo_ref, l2g_ref, l2b_ref, wg_ref, wu_ref, wd_ref, l3g_ref, l3b_ref):
    z = alpha * h + _dot(y, wo_ref[...])
    yield
    h2 = _layer_norm(z, l2g_ref[...], l2b_ref[...])
    yield
    z2 = yield from _swiglu_half_step(h2, wg_ref, wu_ref, wd_ref, alpha)
    store(_layer_norm(z2, l3g_ref[...], l3b_ref[...]))


def _mixer_kernel(tc, flags_ref, *refs):
    in_refs, const_refs, y_ref = refs[:N_MIX_IN], refs[N_MIX_IN:N_MIX_IN + N_MIX_CONST], refs[N_MIX_IN + N_MIX_CONST]
    st_ref, ext_ref, b_scr, sd_scr, at_scr = refs[N_MIX_IN + N_MIX_CONST + 1:]
    step = pl.program_id(0)

    def run(n, robust):
        phases, finish = _mix_chunks(n, robust, in_refs, const_refs, y_ref, st_ref, st_ref, ext_ref,
                                     b_scr, sd_scr, at_scr)
        _run_phases(phases)
        finish()
        for b in range(BATCH):
            ext_ref[b, 0:SUB, :] = ext_ref[b, GLA_CHUNK * n:GLA_CHUNK * n + SUB, :]

    @pl.when(step == 0)
    def _():
        st_ref[...] = jnp.zeros(st_ref.shape, F32)
        ext_ref[:, 0:SUB, :] = jnp.zeros((BATCH, SUB, POOL_WIDTH), F32)
        run(1, True)

    @pl.when(step > 0)
    def _():
        base = (step - 1) * tc
        slow = flags_ref[base]
        for c in range(1, tc):
            slow = jnp.maximum(slow, flags_ref[base + c])

        @pl.when(slow == 0)
        def _():
            run(tc, False)

        @pl.when(slow != 0)
        def _():
            run(tc, True)


def _outproj_ffn_kernel(alpha, to_out, h_ref, y_ref, *refs):
    *w_refs, o_ref = refs
    tile_rows = h_ref.shape[0]

    def store_rows(r0):
        def store(o):
            o_ref[r0:r0 + o.shape[0], :] = o
        return store

    def store_chunks(r0):
        def store(o):
            for sb in range(o.shape[0] // GLA_CHUNK):
                s, b = divmod(r0 // GLA_CHUNK + sb, BATCH)
                o_ref[b, s] = o[sb * GLA_CHUNK:(sb + 1) * GLA_CHUNK]
        return store

    def parts(n_rows, n_parts, make_store):
        pr = n_rows // n_parts
        _staggered([_outproj_ffn_rows(alpha, h_ref[p * pr:(p + 1) * pr, :], y_ref[p * pr:(p + 1) * pr, :],
                                      make_store(p * pr), *w_refs) for p in range(n_parts)])

    if to_out:
        parts(tile_rows, ROW_PARTS, store_chunks)
        return
    last = pl.num_programs(0) - 1

    @pl.when(pl.program_id(0) < last)
    def _():
        parts(tile_rows, ROW_PARTS, store_rows)

    @pl.when(pl.program_id(0) == last)
    def _():
        parts(SC_ROWS, 1, store_rows)


def _resident(shape):
    nd = len(shape)
    return pl.BlockSpec(shape, lambda *_: (0,) * nd, pipeline_mode=pl.Buffered(1))


def _layer_resident(shape, layer):
    nd = len(shape) - 1
    return pl.BlockSpec((None,) + tuple(shape[1:]), lambda *_: (layer,) + (0,) * nd, pipeline_mode=pl.Buffered(1))


def _mixer_constants():
    r = np.arange(SC_ROWS)
    b, t = r // GLA_CHUNK, r % GLA_CHUNK
    same_b = b[:, None] == b[None, :]
    ubd = same_b & (t[:, None] <= t[None, :])
    blk = t // SUB
    uprev = same_b & (blk[:, None] < blk[None, :])
    dist = np.where(same_b, blk[None, :] - blk[:, None], 0)
    slot = np.where((dist >= 1) & (dist < N_SUB), dist, 0).astype(np.int32)
    slot = np.tile(slot, (1, GLA_HEADS))
    dmask = (same_b & (blk[:, None] == blk[None, :]) & (t[None, :] >= t[:, None])).astype(np.int32)
    causal = np.tile((same_b & (t[None, :] >= t[:, None])).astype(np.int32), (1, GLA_HEADS))
    eye = np.eye(SC_ROWS)
    w_lane = np.repeat(np.array(POOL_WINDOWS, np.float32), POOL_GROUP_DIM)[None, :]
    t_meta = np.maximum(t - (GLA_CHUNK - N_META), 0).astype(np.float32)[:, None]
    cnt = np.stack([np.broadcast_to(w_lane, (SC_ROWS, POOL_WIDTH)), np.minimum(t_meta + 1.0, w_lane)])
    return (jnp.asarray(ubd, BF16), jnp.asarray(uprev, BF16), jnp.asarray(slot),
            jnp.asarray(dmask), jnp.asarray(causal), jnp.asarray(eye, BF16), jnp.asarray(cnt, F32))


def _tile_chunks(n_ch):
    return next(k for k in (4, 2, 1) if n_ch % k == 0)


def kernel(x, meta_tokens, ffn1_w_gate, ffn1_w_up, ffn1_w_down, ln1_g, ln1_b, w_in, w_gate_up, b_gate, w_pool, pool_scale, gla_norm_g, w_out, ln2_g, ln2_b, ffn2_w_gate, ffn2_w_up, ffn2_w_down, ln3_g, ln3_b):
    batch, seq, d = x.shape
    depth = w_in.shape[0]
    assert batch == BATCH and d == D_MODEL and seq % GLA_CHUNK == 0
    n_ch = seq // GLA_CHUNK
    n_sc = n_ch + 1
    n_rows = n_sc * SC_ROWS
    tc = _tile_chunks(n_ch)
    tr = tc * SC_ROWS
    n_main = n_ch // tc
    alpha = (2.0 * depth) ** 0.25

    x4 = x.reshape(batch, n_ch, GLA_CHUNK, d)
    meta_half = jnp.concatenate([jnp.zeros((GLA_CHUNK - N_META, d), x.dtype), meta_tokens.astype(x.dtype)], axis=0)
    meta_sc = jnp.concatenate([meta_half, meta_half], axis=0)
    h = None

    ubd, uprev, slot, dmask, causal, eye, cnt = _mixer_constants()
    s0 = POOL_WIDTH
    s1 = s0 + GLA_KEY_WIDTH
    s2 = s1 + GLA_KEY_WIDTH
    s3 = s2 + GLA_WIDTH
    s4 = s3 + GLA_WIDTH
    cparams = pltpu.CompilerParams(dimension_semantics=("arbitrary",), vmem_limit_bytes=VMEM_LIMIT_BYTES)
    row_tile = lambda cols: pl.BlockSpec((tr, cols), lambda i: (i, 0))
    col_tile = lambda rows: pl.BlockSpec((rows, tr), lambda i: (0, i))
    x4_tile = pl.BlockSpec((batch, tc, GLA_CHUNK, d), lambda i: (0, jnp.minimum(i, n_main - 1), 0, 0))
    mix_tile = lambda i: lax.rem(i + n_main, n_main + 1)
    mix_rows = lambda cols: pl.BlockSpec((tr, cols), lambda i, f: (mix_tile(i), 0))
    mix_cols = lambda rows: pl.BlockSpec((rows, tr), lambda i, f: (0, mix_tile(i)))

    bf = lambda w: w.astype(BF16)
    w1g, w1u, w1d = bf(ffn1_w_gate), bf(ffn1_w_up), bf(ffn1_w_down)
    w2g, w2u, w2d = bf(ffn2_w_gate), bf(ffn2_w_up), bf(ffn2_w_down)
    wo_all, wpool_all = bf(w_out), bf(w_pool)
    row3 = lambda p: p.reshape(depth, 1, p.shape[-1])
    l1g, l1b, l2g, l2b, l3g, l3b = (row3(p) for p in (ln1_g, ln1_b, ln2_g, ln2_b, ln3_g, ln3_b))
    pscale_all = row3(pool_scale)

    for l in range(depth):
        wl = w_in[l]
        glr_pad = jnp.zeros((d, LANES - GLA_GATE_RANK), F32)
        wtm = jnp.concatenate([wl[:, :s0], wl[:, s1:s2]], axis=1).astype(BF16)
        wcm = jnp.concatenate([wl[:, s0:s1] * (GLA_DK ** -0.5), wl[:, s2:s3], wl[:, s3:s4]], axis=1).T.astype(BF16)
        wgt = jnp.concatenate([wl[:, s4:], glr_pad], axis=1).T.astype(BF16)
        wgut = jnp.concatenate([w_gate_up[l], jnp.zeros((LANES - GLA_GATE_RANK, GLA_KEY_WIDTH), F32)], axis=0).T.astype(BF16)
        bgc = b_gate[l].reshape(GLA_KEY_WIDTH, 1)

        from_x = l == 0
        acts = (x4, meta_sc) if from_x else (h,)
        act_specs = [x4_tile, _resident(meta_sc.shape)] if from_x else [row_tile(d)]
        outs = pl.pallas_call(
            functools.partial(_ffn_inproj_kernel, alpha, from_x, tr),
            grid=(n_main + 1,),
            in_specs=act_specs + [_layer_resident(w1g.shape, l), _layer_resident(w1u.shape, l),
                                  _layer_resident(w1d.shape, l), _layer_resident(l1g.shape, l),
                                  _layer_resident(l1b.shape, l),
                                  _resident(wtm.shape), _resident(wcm.shape), _resident(wgt.shape),
                                  _resident(wgut.shape), _resident(bgc.shape)],
            out_specs=[row_tile(d), row_tile(POOL_WIDTH), row_tile(GLA_KEY_WIDTH),
                       col_tile(GLA_KEY_WIDTH), col_tile(GLA_KEY_WIDTH), col_tile(GLA_WIDTH), col_tile(GLA_WIDTH),
                       pl.BlockSpec((FLAG_ROWS, LANES), lambda i: (i, 0))],
            out_shape=[jax.ShapeDtypeStruct((n_rows, d), F32),
                       jax.ShapeDtypeStruct((n_rows, POOL_WIDTH), F32),
                       jax.ShapeDtypeStruct((n_rows, GLA_KEY_WIDTH), F32),
                       jax.ShapeDtypeStruct((GLA_KEY_WIDTH, n_rows), F32),
                       jax.ShapeDtypeStruct((GLA_KEY_WIDTH, n_rows), F32),
                       jax.ShapeDtypeStruct((GLA_WIDTH, n_rows), BF16),
                       jax.ShapeDtypeStruct((GLA_WIDTH, n_rows), F32),
                       jax.ShapeDtypeStruct(((n_main + 1) * FLAG_ROWS, LANES), jnp.int32)],
            compiler_params=cparams,
            name=f"ffn_inproj_{l}",
        )(*acts, w1g, w1u, w1d, l1g, l1b, wtm, wcm, wgt, wgut, bgc)
        h1, u_tm, k_tm, q_t, la_t, v_t, r_t, tile_flags = outs
        sc_flags = tile_flags.reshape(n_main + 1, FLAG_ROWS, LANES)[:, :tc, 0].reshape(-1)[:n_sc]

        gn = jnp.broadcast_to(gla_norm_g[l].reshape(GLA_WIDTH, 1), (GLA_WIDTH, SC_ROWS))
        mix_in = (u_tm, k_tm, q_t, la_t, v_t, r_t)
        mix_in_specs = [mix_rows(POOL_WIDTH), mix_rows(GLA_KEY_WIDTH), mix_cols(GLA_KEY_WIDTH),
                        mix_cols(GLA_KEY_WIDTH), mix_cols(GLA_WIDTH), mix_cols(GLA_WIDTH)]
        mix_const = (ubd, uprev, slot, dmask, causal, eye, gn, cnt, wpool_all, pscale_all)
        mix_const_specs = [_resident(ubd.shape), _resident(uprev.shape), _resident(slot.shape),
                           _resident(dmask.shape), _resident(causal.shape), _resident(eye.shape), _resident(gn.shape),
                           pl.BlockSpec((None, SC_ROWS, POOL_WIDTH), lambda i, f: (jnp.where(i == 0, 1, 0), 0, 0)),
                           _layer_resident(wpool_all.shape, l), _layer_resident(pscale_all.shape, l)]
        assert len(mix_in) == N_MIX_IN and len(mix_const) == N_MIX_CONST
        y_cat = pl.pallas_call(
            functools.partial(_mixer_kernel, tc),
            grid_spec=pltpu.PrefetchScalarGridSpec(
                num_scalar_prefetch=1,
                grid=(n_main + 1,),
                in_specs=mix_in_specs + mix_const_specs,
                out_specs=mix_rows(d),
                scratch_shapes=[pltpu.VMEM((GLA_HEADS, GLA_DV, SC_ROWS), F32),
                                pltpu.VMEM((BATCH, SUB + tc * GLA_CHUNK, POOL_WIDTH), F32),
                                pltpu.VMEM((tc, SC_ROWS, GLA_KEY_WIDTH), F32),
                                pltpu.VMEM((GLA_HEADS * SUB, SC_ROWS), F32),
                                pltpu.VMEM((SC_ROWS, GLA_HEADS * SC_ROWS), F32)]),
            out_shape=jax.ShapeDtypeStruct((n_rows, d), BF16),
            compiler_params=cparams,
            name=f"mixer_{l}",
        )(sc_flags, *mix_in, *mix_const)

        to_out = l == depth - 1
        h = pl.pallas_call(
            functools.partial(_outproj_ffn_kernel, alpha, to_out),
            grid=(n_main if to_out else n_main + 1,),
            in_specs=[row_tile(d), row_tile(d), _layer_resident(wo_all.shape, l),
                      _layer_resident(l2g.shape, l), _layer_resident(l2b.shape, l),
                      _layer_resident(w2g.shape, l), _layer_resident(w2u.shape, l), _layer_resident(w2d.shape, l),
                      _layer_resident(l3g.shape, l), _layer_resident(l3b.shape, l)],
            out_specs=pl.BlockSpec((batch, tc, GLA_CHUNK, d), lambda i: (0, i, 0, 0)) if to_out else row_tile(d),
            out_shape=jax.ShapeDtypeStruct(x4.shape if to_out else (n_rows, d), F32),
            compiler_params=cparams,
            name=f"outproj_ffn_{l}",
        )(h1, y_cat, wo_all, l2g, l2b, w2g, w2u, w2d, l3g, l3b)

    return h.reshape(batch, seq, d)
```

```python
import functools

import numpy as np
import jax
import jax.numpy as jnp
from jax import lax
from jax.experimental import pallas as pl
from jax.experimental.pallas import tpu as pltpu

D_MODEL = 1024
N_META = 16
POOL_WIDTH = 512
POOL_GROUPS = 4
POOL_GROUP_DIM = 128
POOL_WINDOWS = (2, 4, 8, 16)
GLA_WIDTH = 512
GLA_HEADS = 4
GLA_KEY_WIDTH = 256
GLA_DK = 64
GLA_DV = 128
GLA_GATE_RANK = 16
GLA_GATE_TEMP = 16.0
GLA_CHUNK = 64
D_FF = 2816
LN_EPS = 1e-5
RMS_EPS = 1e-6

BATCH = 2
SC_ROWS = BATCH * GLA_CHUNK
SUB = 16
N_SUB = GLA_CHUNK // SUB
LANES = 128
FF_SPLIT = 1
ROW_PARTS = 2
DENSE_TILE_CHUNKS = 4
MIXER_TILE_CHUNKS = 8
STAGGER_LAG = 1
VMEM_LIMIT_BYTES = 60 * 1024 * 1024
FAST_PATH_MAX_DECAY = 40.0
FLAG_ROWS = 8
N_MIX_IN = 6
N_MIX_CONST = 10

F32 = jnp.float32
BF16 = jnp.bfloat16

assert POOL_WINDOWS == tuple(2 << g for g in range(POOL_GROUPS)) and POOL_WINDOWS[-1] == SUB


def _dot(a, b):
    return jnp.dot(a, b, preferred_element_type=F32)


def _dot_nt(a, b):
    return lax.dot_general(a, b, (((1,), (1,)), ((), ())), preferred_element_type=F32)


def _layer_norm(z, g, b):
    mu = jnp.mean(z, axis=-1, keepdims=True)
    zc = z - mu
    var = jnp.mean(zc * zc, axis=-1, keepdims=True)
    return zc * lax.rsqrt(var + LN_EPS) * g + b


def _silu(x):
    return x * jax.nn.sigmoid(x)


def _log_sigmoid(x):
    return jnp.minimum(x, 0.0) - jnp.log1p(jnp.exp(-jnp.abs(x)))


def _swiglu_half_step(x, wg_ref, wu_ref, wd_ref, alpha):
    xb = x.astype(BF16)
    fc = D_FF // FF_SPLIT
    y = None
    for c in range(FF_SPLIT):
        cols = slice(c * fc, (c + 1) * fc)
        g = _dot(xb, wg_ref[:, cols])
        u = _dot(xb, wu_ref[:, cols])
        yield
        act = (_silu(g) * u).astype(BF16)
        yield
        part = _dot(act, wd_ref[cols, :])
        y = part if y is None else y + part
    yield
    return alpha * x + 0.5 * y


class _Stagger:
    def __init__(self, gens, lag=STAGGER_LAG):
        self.gens = list(gens)
        self.lag = lag
        self.t = 0
        self.done = [False] * len(self.gens)
        self.results = [None] * len(self.gens)

    def tick(self):
        for k, gen in enumerate(self.gens):
            if self.done[k] or self.t < k * self.lag:
                continue
            try:
                next(gen)
            except StopIteration as stop:
                self.results[k], self.done[k] = stop.value, True
        self.t += 1
        return not all(self.done)


def _run_phases(*staggers):
    while any([s.tick() for s in staggers]):
        pass


def _staggered(gens, lag=STAGGER_LAG):
    s = _Stagger(gens, lag)
    _run_phases(s)
    return s.results


def _ffn_inproj_rows(alpha, x, r0, is_meta, wg_ref, wu_ref, wd_ref, lng_ref, lnb_ref,
                     wtm_ref, wcm_ref, wgt_ref, wgut_ref, bgc_ref,
                     h_ref, u_ref, k_ref, qt_ref, lat_ref, vt_ref, rt_ref, flag_ref):
    n = x.shape[0]
    rows = slice(r0, r0 + n)
    z = yield from _swiglu_half_step(x, wg_ref, wu_ref, wd_ref, alpha)
    h = _layer_norm(z, lng_ref[...], lnb_ref[...])
    h_ref[rows, :] = h
    hb = h.astype(BF16)
    yield

    if is_meta:
        pad = GLA_CHUNK - N_META
        keep_r = (lax.broadcasted_iota(jnp.int32, (n, 1), 0) % GLA_CHUNK) >= pad
        keep_l = (lax.broadcasted_iota(jnp.int32, (1, n), 1) % GLA_CHUNK) >= pad
        mask_r = lambda v: jnp.where(keep_r, v, 0.0)
        mask_l = lambda v: jnp.where(keep_l, v, 0.0)
    else:
        mask_r = mask_l = lambda v: v

    ztm = _dot(hb, wtm_ref[...])
    u_ref[rows, :] = mask_r(ztm[:, :POOL_WIDTH])
    k_ref[rows, :] = mask_r(ztm[:, POOL_WIDTH:])
    yield

    zt = _dot_nt(wcm_ref[...], hb)
    kw = GLA_KEY_WIDTH
    qt_ref[:, rows] = mask_l(zt[:kw])
    vt_ref[:, rows] = mask_l(zt[kw:kw + GLA_WIDTH]).astype(BF16)
    rt_ref[:, rows] = _silu(zt[kw + GLA_WIDTH:])
    gt = _dot_nt(wgt_ref[...], hb).astype(BF16)
    lat = mask_l(_log_sigmoid(_dot(wgut_ref[...], gt) + bgc_ref[...]) * (1.0 / GLA_GATE_TEMP))
    lat_ref[:, rows] = lat

    flags = []
    lane_lo = lax.broadcasted_iota(jnp.int32, (GLA_KEY_WIDTH, SC_ROWS), 1) < GLA_CHUNK
    for s in range(n // SC_ROWS):
        seg = lat[:, SC_ROWS * s:SC_ROWS * (s + 1)]
        tot = jnp.minimum(jnp.sum(jnp.where(lane_lo, seg, 0.0), axis=1, keepdims=True),
                          jnp.sum(jnp.where(lane_lo, 0.0, seg), axis=1, keepdims=True))
        flags.append((jnp.min(tot, axis=0, keepdims=True) < -FAST_PATH_MAX_DECAY).astype(jnp.int32))
    return flags


def _ffn_inproj_kernel(alpha, from_x, tile_rows, x_ref, *refs):
    if from_x:
        meta_ref, *refs = refs
    flag_ref = refs[-1]
    last = pl.num_programs(0) - 1

    def x_rows(r0, n):
        if not from_x:
            return x_ref[r0:r0 + n, :]
        pieces = [divmod(r0 // GLA_CHUNK + i, BATCH) for i in range(n // GLA_CHUNK)]
        return jnp.concatenate([x_ref[b, s] for s, b in pieces], axis=0)

    def run(parts, is_meta):
        gens = [_ffn_inproj_rows(alpha, x, r0, is_meta, *refs) for x, r0 in parts]
        sc_flags = [f for fl in _staggered(gens, STAGGER_LAG) for f in fl]
        frow = lax.broadcasted_iota(jnp.int32, (FLAG_ROWS, LANES), 0)
        flags = jnp.zeros((FLAG_ROWS, LANES), jnp.int32)
        for s, f in enumerate(sc_flags):
            flags = jnp.where(frow == s, f, flags)
        flag_ref[...] = flags

    @pl.when(pl.program_id(0) < last)
    def _():
        pr = tile_rows // ROW_PARTS
        run([(x_rows(p * pr, pr), p * pr) for p in range(ROW_PARTS)], False)

    @pl.when(pl.program_id(0) == last)
    def _():
        run([(meta_ref[...] if from_x else x_ref[0:SC_ROWS, :], 0)], True)


def _split3(x):
    hi = x.astype(BF16)
    r1 = x - hi.astype(F32)
    mid = r1.astype(BF16)
    lo = (r1 - mid.astype(F32)).astype(BF16)
    return hi, mid, lo


def _transpose_cm(xt):
    return jnp.concatenate([xt[:SC_ROWS].T, xt[SC_ROWS:].T], axis=1)


def _dot3_right(x, c01):
    parts = _split3(x)
    return _dot(parts[0], c01) + _dot(parts[1], c01) + _dot(parts[2], c01)


def _mix_chunks(n, robust, in_refs, const_refs, y_ref, st_ref, st_new_ref, ext_ref, b_scr, sd_scr, at_scr):
    u_ref, k_ref, qt_ref, lat_ref, vt_ref, rt_ref = in_refs
    (ubd_ref, uprev_ref, slot_ref, dmask_ref, causal_ref, eye_ref, gn_ref, cnt_ref,
     wpool_ref, pscale_ref) = const_refs
    C = GLA_CHUNK
    lane = lax.broadcasted_iota(jnp.int32, (SC_ROWS, SC_ROWS), 1)
    rowi = lax.broadcasted_iota(jnp.int32, (SC_ROWS, SC_ROWS), 0)
    lane64 = lax.broadcasted_iota(jnp.int32, (C, SC_ROWS), 1)
    lane8 = lax.broadcasted_iota(jnp.int32, (8, SC_ROWS), 1)
    lane_lo = lane < C
    row_lo = rowi < C
    row_head = lax.broadcasted_iota(jnp.int32, (GLA_KEY_WIDTH, SC_ROWS), 0) // GLA_DK

    def heads_on_lanes(xt):
        return jnp.concatenate([jnp.where(row_head == h, xt, 0.0) for h in range(GLA_HEADS)], axis=1).astype(BF16)

    for c in range(n):
        for b in range(BATCH):
            ext_ref[b, SUB + C * c:SUB + C * (c + 1), :] = u_ref[SC_ROWS * c + C * b:SC_ROWS * c + C * (b + 1), :]
    states = [st_ref[h] for h in range(GLA_HEADS)]
    cnt = cnt_ref[...]
    ubd = ubd_ref[...]
    eye = eye_ref[...]
    gn = gn_ref[...]

    def chunk(c):
        rows = slice(SC_ROWS * c, SC_ROWS * (c + 1))
        bT = _dot3_right(lat_ref[:, rows], ubd)

        u = u_ref[rows, :]
        win = []
        for b in range(BATCH):
            level = ext_ref[b, C * c:C * c + SUB + C, :]
            for g in range(POOL_GROUPS):
                level = level[:, POOL_GROUP_DIM if g else 0:]
                level = level + pltpu.roll(level, 1 << g, axis=0)
                win.append(level[SUB:, :POOL_GROUP_DIM])
        parts = []
        for g in range(POOL_GROUPS):
            cols = slice(POOL_GROUP_DIM * g, POOL_GROUP_DIM * (g + 1))
            s = jnp.concatenate([win[b * POOL_GROUPS + g] for b in range(BATCH)], axis=0)
            p = s / cnt[:, cols] - u[:, cols]
            parts.append(_dot(p.astype(BF16), wpool_ref[g]))
        y_pool = jnp.concatenate(parts, axis=1) * pscale_ref[...]
        yield

        b_tm = _transpose_cm(bT)
        b_scr[c] = b_tm
        ends = [[b_scr[c, pl.ds(C * b + SUB * j + SUB - 1, 1), :] for j in range(N_SUB)] for b in range(BATCH)]

        def per_block(fn):
            return jnp.concatenate(
                [jnp.broadcast_to(fn(b, j), (SUB, GLA_KEY_WIDTH)) for b in range(BATCH) for j in range(N_SUB)], axis=0)

        k_tm = k_ref[rows, :]
        qT = qt_ref[:, rows]
        e_last = per_block(lambda b, j: ends[b][N_SUB - 1])
        ktil = k_tm * jnp.exp(e_last - b_tm)
        qeT = qT * jnp.exp(bT)

        if not robust:
            kneg = (k_tm * jnp.exp(-b_tm)).astype(BF16)
            at_all = jnp.where(causal_ref[...] != 0, _dot(kneg, heads_on_lanes(qeT)), 0.0)
        else:
            cprevT = _dot3_right(lat_ref[:, rows], uprev_ref[...])
            e_own = per_block(lambda b, j: ends[b][j])
            f2 = per_block(lambda b, j: jnp.exp(ends[b][min(j + 1, N_SUB - 1)] - ends[b][j]))
            f3 = per_block(lambda b, j: jnp.exp(ends[b][min(j + 2, N_SUB - 1)] - ends[b][j]))
            khat = k_tm * jnp.exp(e_own - b_tm)
            kslots = jnp.concatenate([khat, khat * f2, khat * f3], axis=0).astype(BF16)
            r_all = _dot(kslots, heads_on_lanes(qT * jnp.exp(bT - cprevT)))
            slot = slot_ref[...]
            at_off = jnp.where(slot == 1, r_all[0:SC_ROWS],
                               jnp.where(slot == 2, r_all[SC_ROWS:2 * SC_ROWS],
                                         jnp.where(slot == 3, r_all[2 * SC_ROWS:], 0.0)))
            kT = jnp.concatenate([k_tm[:, :LANES].T, k_tm[:, LANES:].T], axis=0)
            for dist in range(SUB):
                if dist == 0:
                    qs, bs = qT, bT
                else:
                    qs = pltpu.roll(qT, SC_ROWS - dist, axis=1)
                    bs = pltpu.roll(bT, SC_ROWS - dist, axis=1)
                prod = qs * kT * jnp.exp(jnp.minimum(bs - bT, 0.0))
                for h in range(GLA_HEADS):
                    sd_scr[pl.ds(SUB * h + dist, 1), :] = jnp.sum(
                        prod[GLA_DK * h:GLA_DK * (h + 1)], axis=0, keepdims=True)
            sd = jnp.concatenate([sd_scr[...], jnp.zeros((SC_ROWS - GLA_HEADS * SUB, SC_ROWS), F32)], axis=0)
            sdt = sd.T
            dmask = dmask_ref[...]
            for h in range(GLA_HEADS):
                xh = jnp.where((lane >= SUB * h) & (lane < SUB * (h + 1)), sdt, 0.0)
                skew = pltpu.roll(xh, (SC_ROWS - SUB * h) % SC_ROWS, axis=1, stride=1, stride_axis=0)
                at_scr[:, SC_ROWS * h:SC_ROWS * (h + 1)] = (
                    jnp.where(dmask != 0, skew, 0.0) + at_off[:, SC_ROWS * h:SC_ROWS * (h + 1)])
            at_all = at_scr[...]
        yield

        vT = vt_ref[:, rows]
        rT = rt_ref[:, rows]
        results = []
        for h in range(GLA_HEADS):
            at_h = at_all[:, SC_ROWS * h:SC_ROWS * (h + 1)]

            half = (h % 2) * C
            kt_cols = ktil[:, LANES * (h // 2):LANES * (h // 2 + 1)]
            kt_roll = pltpu.roll(kt_cols, C, axis=1)
            lo_src, hi_src = (kt_cols, kt_roll) if half == 0 else (kt_roll, kt_cols)
            kbd = jnp.where(row_lo & lane_lo, lo_src, jnp.where((~row_lo) & (~lane_lo), hi_src, 0.0))
            qe_h = qeT[GLA_DK * h:GLA_DK * (h + 1)]
            qebd = jnp.concatenate([jnp.where(lane64 < C, qe_h, 0.0), jnp.where(lane64 >= C, qe_h, 0.0)], axis=0)

            v_h = vT[GLA_DV * h:GLA_DV * (h + 1)]
            res = _dot(v_h, jnp.concatenate([at_h, kbd], axis=1).astype(BF16))
            results.append((res, _dot(states[h].astype(BF16), qebd.astype(BF16))))
        yield

        y_heads = []
        for h in range(GLA_HEADS):
            half = (h % 2) * C
            res, o_inter = results[h]
            st = states[h]
            o_t = res[:, :SC_ROWS] + o_inter

            e0 = jnp.broadcast_to(ends[0][N_SUB - 1][:, LANES * (h // 2):LANES * (h // 2 + 1)], (8, LANES))
            e1 = jnp.broadcast_to(ends[1][N_SUB - 1][:, LANES * (h // 2):LANES * (h // 2 + 1)], (8, LANES))
            if half == 0:
                e1 = pltpu.roll(e1, C, axis=1)
            else:
                e0 = pltpu.roll(e0, C, axis=1)
            dec = jnp.exp(jnp.where(lane8 < C, e0, e1))
            states[h] = st * jnp.broadcast_to(dec[0:1], (GLA_DV, SC_ROWS)) + res[:, SC_ROWS:]

            ms = jnp.mean(o_t * o_t, axis=0, keepdims=True)
            gsl = slice(GLA_DV * h, GLA_DV * (h + 1))
            y_heads.append(o_t * lax.rsqrt(ms + RMS_EPS) * gn[gsl] * rT[gsl])
        y_t = jnp.concatenate(y_heads, axis=0).astype(BF16)
        y_gla = _dot_nt(eye, y_t)
        yield
        y_ref[rows, :] = jnp.concatenate([y_pool, y_gla], axis=1).astype(BF16)

    def in_turn():
        for c in range(n):
            yield from chunk(c)

    def finish():
        for h in range(GLA_HEADS):
            st_new_ref[h] = states[h]

    return _Stagger([in_turn()] if robust else [chunk(c) for c in range(n)]), finish


def _outproj_ffn_rows(alpha, h, y, store, wo_ref, l2g_ref, l2b_ref, wg_ref, wu_ref, wd_ref, l3g_ref, l3b_ref):
    z = alpha * h + _dot(y, wo_ref[...])
    yield
    h2 = _layer_norm(z, l2g_ref[...], l2b_ref[...])
    yield
    z2 = yield from _swiglu_half_step(h2, wg_ref, wu_ref, wd_ref, alpha)
    store(_layer_norm(z2, l3g_ref[...], l3b_ref[...]))


def _mixer_kernel(tc, flags_ref, *refs):
    in_refs, const_refs, y_ref = refs[:N_MIX_IN], refs[N_MIX_IN:N_MIX_IN + N_MIX_CONST], refs[N_MIX_IN + N_MIX_CONST]
    st_ref, ext_ref, b_scr, sd_scr, at_scr = refs[N_MIX_IN + N_MIX_CONST + 1:]
    step = pl.program_id(0)

    def run(n, robust):
        phases, finish = _mix_chunks(n, robust, in_refs, const_refs, y_ref, st_ref, st_ref, ext_ref,
                                     b_scr, sd_scr, at_scr)
        _run_phases(phases)
        finish()
        for b in range(BATCH):
            ext_ref[b, 0:SUB, :] = ext_ref[b, GLA_CHUNK * n:GLA_CHUNK * n + SUB, :]

    @pl.when(step == 0)
    def _():
        st_ref[...] = jnp.zeros(st_ref.shape, F32)
        ext_ref[:, 0:SUB, :] = jnp.zeros((BATCH, SUB, POOL_WIDTH), F32)
        run(1, True)

    @pl.when(step > 0)
    def _():
        base = (step - 1) * tc
        slow = flags_ref[base]
        for c in range(1, tc):
            slow = jnp.maximum(slow, flags_ref[base + c])

        @pl.when(slow == 0)
        def _():
            run(tc, False)

        @pl.when(slow != 0)
        def _():
            run(tc, True)


def _outproj_ffn_kernel(alpha, to_out, h_ref, y_ref, *refs):
    *w_refs, o_ref = refs
    tile_rows = h_ref.shape[0]

    def store_rows(r0):
        def store(o):
            o_ref[r0:r0 + o.shape[0], :] = o
        return store

    def store_chunks(r0):
        def store(o):
            for sb in range(o.shape[0] // GLA_CHUNK):
                s, b = divmod(r0 // GLA_CHUNK + sb, BATCH)
                o_ref[b, s] = o[sb * GLA_CHUNK:(sb + 1) * GLA_CHUNK]
        return store

    def parts(n_rows, n_parts, make_store):
        pr = n_rows // n_parts
        _staggered([_outproj_ffn_rows(alpha, h_ref[p * pr:(p + 1) * pr, :], y_ref[p * pr:(p + 1) * pr, :],
                                      make_store(p * pr), *w_refs) for p in range(n_parts)])

    if to_out:
        parts(tile_rows, ROW_PARTS, store_chunks)
        return
    last = pl.num_programs(0) - 1

    @pl.when(pl.program_id(0) < last)
    def _():
        parts(tile_rows, ROW_PARTS, store_rows)

    @pl.when(pl.program_id(0) == last)
    def _():
        parts(SC_ROWS, 1, store_rows)


def _resident(shape):
    nd = len(shape)
    return pl.BlockSpec(shape, lambda *_: (0,) * nd, pipeline_mode=pl.Buffered(1))


def _layer_resident(shape, layer):
    nd = len(shape) - 1
    return pl.BlockSpec((None,) + tuple(shape[1:]), lambda *_: (layer,) + (0,) * nd, pipeline_mode=pl.Buffered(1))


def _mixer_constants():
    r = np.arange(SC_ROWS)
    b, t = r // GLA_CHUNK, r % GLA_CHUNK
    same_b = b[:, None] == b[None, :]
    ubd = same_b & (t[:, None] <= t[None, :])
    blk = t // SUB
    uprev = same_b & (blk[:, None] < blk[None, :])
    dist = np.where(same_b, blk[None, :] - blk[:, None], 0)
    slot = np.where((dist >= 1) & (dist < N_SUB), dist, 0).astype(np.int32)
    slot = np.tile(slot, (1, GLA_HEADS))
    dmask = (same_b & (blk[:, None] == blk[None, :]) & (t[None, :] >= t[:, None])).astype(np.int32)
    causal = np.tile((same_b & (t[None, :] >= t[:, None])).astype(np.int32), (1, GLA_HEADS))
    eye = np.eye(SC_ROWS)
    w_lane = np.repeat(np.array(POOL_WINDOWS, np.float32), POOL_GROUP_DIM)[None, :]
    t_meta = np.maximum(t - (GLA_CHUNK - N_META), 0).astype(np.float32)[:, None]
    cnt = np.stack([np.broadcast_to(w_lane, (SC_ROWS, POOL_WIDTH)), np.minimum(t_meta + 1.0, w_lane)])
    return (jnp.asarray(ubd, BF16), jnp.asarray(uprev, BF16), jnp.asarray(slot),
            jnp.asarray(dmask), jnp.asarray(causal), jnp.asarray(eye, BF16), jnp.asarray(cnt, F32))


def _tile_chunks(n_ch, most):
    return next(k for k in (8, 4, 2, 1) if k <= most and n_ch % k == 0)


def kernel(x, meta_tokens, ffn1_w_gate, ffn1_w_up, ffn1_w_down, ln1_g, ln1_b, w_in, w_gate_up, b_gate, w_pool, pool_scale, gla_norm_g, w_out, ln2_g, ln2_b, ffn2_w_gate, ffn2_w_up, ffn2_w_down, ln3_g, ln3_b):
    batch, seq, d = x.shape
    depth = w_in.shape[0]
    assert batch == BATCH and d == D_MODEL and seq % GLA_CHUNK == 0
    n_ch = seq // GLA_CHUNK
    n_sc = n_ch + 1
    n_rows = n_sc * SC_ROWS
    tc = _tile_chunks(n_ch, DENSE_TILE_CHUNKS)
    tr = tc * SC_ROWS
    n_main = n_ch // tc
    tcm = _tile_chunks(n_ch, MIXER_TILE_CHUNKS)
    n_mix = n_ch // tcm
    alpha = (2.0 * depth) ** 0.25

    x4 = x.reshape(batch, n_ch, GLA_CHUNK, d)
    meta_half = jnp.concatenate([jnp.zeros((GLA_CHUNK - N_META, d), x.dtype), meta_tokens.astype(x.dtype)], axis=0)
    meta_sc = jnp.concatenate([meta_half, meta_half], axis=0)
    h = None

    ubd, uprev, slot, dmask, causal, eye, cnt = _mixer_constants()
    s0 = POOL_WIDTH
    s1 = s0 + GLA_KEY_WIDTH
    s2 = s1 + GLA_KEY_WIDTH
    s3 = s2 + GLA_WIDTH
    s4 = s3 + GLA_WIDTH
    cparams = pltpu.CompilerParams(dimension_semantics=("arbitrary",), vmem_limit_bytes=VMEM_LIMIT_BYTES)
    row_tile = lambda cols: pl.BlockSpec((tr, cols), lambda i: (i, 0))
    col_tile = lambda rows: pl.BlockSpec((rows, tr), lambda i: (0, i))
    x4_tile = pl.BlockSpec((batch, tc, GLA_CHUNK, d), lambda i: (0, jnp.minimum(i, n_main - 1), 0, 0))
    mix_tile = lambda i: lax.rem(i + n_mix, n_mix + 1)
    mix_rows = lambda cols: pl.BlockSpec((tcm * SC_ROWS, cols), lambda i, f: (mix_tile(i), 0))
    mix_cols = lambda rows: pl.BlockSpec((rows, tcm * SC_ROWS), lambda i, f: (0, mix_tile(i)))

    bf = lambda w: w.astype(BF16)
    w1g, w1u, w1d = bf(ffn1_w_gate), bf(ffn1_w_up), bf(ffn1_w_down)
    w2g, w2u, w2d = bf(ffn2_w_gate), bf(ffn2_w_up), bf(ffn2_w_down)
    wo_all, wpool_all = bf(w_out), bf(w_pool)
    row3 = lambda p: p.reshape(depth, 1, p.shape[-1])
    l1g, l1b, l2g, l2b, l3g, l3b = (row3(p) for p in (ln1_g, ln1_b, ln2_g, ln2_b, ln3_g, ln3_b))
    pscale_all = row3(pool_scale)

    for l in range(depth):
        wl = w_in[l]
        glr_pad = jnp.zeros((d, LANES - GLA_GATE_RANK), F32)
        wtm = jnp.concatenate([wl[:, :s0], wl[:, s1:s2]], axis=1).astype(BF16)
        wcm = jnp.concatenate([wl[:, s0:s1] * (GLA_DK ** -0.5), wl[:, s2:s3], wl[:, s3:s4]], axis=1).T.astype(BF16)
        wgt = jnp.concatenate([wl[:, s4:], glr_pad], axis=1).T.astype(BF16)
        wgut = jnp.concatenate([w_gate_up[l], jnp.zeros((LANES - GLA_GATE_RANK, GLA_KEY_WIDTH), F32)], axis=0).T.astype(BF16)
        bgc = b_gate[l].reshape(GLA_KEY_WIDTH, 1)

        from_x = l == 0
        acts = (x4, meta_sc) if from_x else (h,)
        act_specs = [x4_tile, _resident(meta_sc.shape)] if from_x else [row_tile(d)]
        outs = pl.pallas_call(
            functools.partial(_ffn_inproj_kernel, alpha, from_x, tr),
            grid=(n_main + 1,),
            in_specs=act_specs + [_layer_resident(w1g.shape, l), _layer_resident(w1u.shape, l),
                                  _layer_resident(w1d.shape, l), _layer_resident(l1g.shape, l),
                                  _layer_resident(l1b.shape, l),
                                  _resident(wtm.shape), _resident(wcm.shape), _resident(wgt.shape),
                                  _resident(wgut.shape), _resident(bgc.shape)],
            out_specs=[row_tile(d), row_tile(POOL_WIDTH), row_tile(GLA_KEY_WIDTH),
                       col_tile(GLA_KEY_WIDTH), col_tile(GLA_KEY_WIDTH), col_tile(GLA_WIDTH), col_tile(GLA_WIDTH),
                       pl.BlockSpec((FLAG_ROWS, LANES), lambda i: (i, 0))],
            out_shape=[jax.ShapeDtypeStruct((n_rows, d), F32),
                       jax.ShapeDtypeStruct((n_rows, POOL_WIDTH), F32),
                       jax.ShapeDtypeStruct((n_rows, GLA_KEY_WIDTH), F32),
                       jax.ShapeDtypeStruct((GLA_KEY_WIDTH, n_rows), F32),
                       jax.ShapeDtypeStruct((GLA_KEY_WIDTH, n_rows), F32),
                       jax.ShapeDtypeStruct((GLA_WIDTH, n_rows), BF16),
                       jax.ShapeDtypeStruct((GLA_WIDTH, n_rows), F32),
                       jax.ShapeDtypeStruct(((n_main + 1) * FLAG_ROWS, LANES), jnp.int32)],
            compiler_params=cparams,
            name=f"ffn_inproj_{l}",
        )(*acts, w1g, w1u, w1d, l1g, l1b, wtm, wcm, wgt, wgut, bgc)
        h1, u_tm, k_tm, q_t, la_t, v_t, r_t, tile_flags = outs
        sc_flags = tile_flags.reshape(n_main + 1, FLAG_ROWS, LANES)[:, :tc, 0].reshape(-1)[:n_sc]

        gn = jnp.broadcast_to(gla_norm_g[l].reshape(GLA_WIDTH, 1), (GLA_WIDTH, SC_ROWS))
        mix_in = (u_tm, k_tm, q_t, la_t, v_t, r_t)
        mix_in_specs = [mix_rows(POOL_WIDTH), mix_rows(GLA_KEY_WIDTH), mix_cols(GLA_KEY_WIDTH),
                        mix_cols(GLA_KEY_WIDTH), mix_cols(GLA_WIDTH), mix_cols(GLA_WIDTH)]
        mix_const = (ubd, uprev, slot, dmask, causal, eye, gn, cnt, wpool_all, pscale_all)
        mix_const_specs = [_resident(ubd.shape), _resident(uprev.shape), _resident(slot.shape),
                           _resident(dmask.shape), _resident(causal.shape), _resident(eye.shape), _resident(gn.shape),
                           pl.BlockSpec((None, SC_ROWS, POOL_WIDTH), lambda i, f: (jnp.where(i == 0, 1, 0), 0, 0)),
                           _layer_resident(wpool_all.shape, l), _layer_resident(pscale_all.shape, l)]
        assert len(mix_in) == N_MIX_IN and len(mix_const) == N_MIX_CONST
        y_cat = pl.pallas_call(
            functools.partial(_mixer_kernel, tcm),
            grid_spec=pltpu.PrefetchScalarGridSpec(
                num_scalar_prefetch=1,
                grid=(n_mix + 1,),
                in_specs=mix_in_specs + mix_const_specs,
                out_specs=mix_rows(d),
                scratch_shapes=[pltpu.VMEM((GLA_HEADS, GLA_DV, SC_ROWS), F32),
                                pltpu.VMEM((BATCH, SUB + tcm * GLA_CHUNK, POOL_WIDTH), F32),
                                pltpu.VMEM((tcm, SC_ROWS, GLA_KEY_WIDTH), F32),
                                pltpu.VMEM((GLA_HEADS * SUB, SC_ROWS), F32),
                                pltpu.VMEM((SC_ROWS, GLA_HEADS * SC_ROWS), F32)]),
            out_shape=jax.ShapeDtypeStruct((n_rows, d), BF16),
            compiler_params=cparams,
            name=f"mixer_{l}",
        )(sc_flags, *mix_in, *mix_const)

        to_out = l == depth - 1
        h = pl.pallas_call(
            functools.partial(_outproj_ffn_kernel, alpha, to_out),
            grid=(n_main if to_out else n_main + 1,),
            in_specs=[row_tile(d), row_tile(d), _layer_resident(wo_all.shape, l),
                      _layer_resident(l2g.shape, l), _layer_resident(l2b.shape, l),
                      _layer_resident(w2g.shape, l), _layer_resident(w2u.shape, l), _layer_resident(w2d.shape, l),
                      _layer_resident(l3g.shape, l), _layer_resident(l3b.shape, l)],
            out_specs=pl.BlockSpec((batch, tc, GLA_CHUNK, d), lambda i: (0, i, 0, 0)) if to_out else row_tile(d),
            out_shape=jax.ShapeDtypeStruct(x4.shape if to_out else (n_rows, d), F32),
            compiler_params=cparams,
            name=f"outproj_ffn_{l}",
        )(h1, y_cat, wo_all, l2g, l2b, w2g, w2u, w2d, l3g, l3b)

    return h.reshape(batch, seq, d)
```

```python
import functools

import numpy as np
import jax
import jax.numpy as jnp
from jax import lax
from jax.experimental import pallas as pl
from jax.experimental.pallas import tpu as pltpu

D_MODEL = 1024
N_META = 16
POOL_WIDTH = 512
POOL_GROUPS = 4
POOL_GROUP_DIM = 128
POOL_WINDOWS = (2, 4, 8, 16)
GLA_WIDTH = 512
GLA_HEADS = 4
GLA_KEY_WIDTH = 256
GLA_DK = 64
GLA_DV = 128
GLA_GATE_RANK = 16
GLA_GATE_TEMP = 16.0
GLA_CHUNK = 64
D_FF = 2816
LN_EPS = 1e-5
RMS_EPS = 1e-6

BATCH = 2
SC_ROWS = BATCH * GLA_CHUNK
SUB = 16
N_SUB = GLA_CHUNK // SUB
LANES = 128
FF_SPLIT = 1
PART_ROWS = 256
DENSE_TILE_CHUNKS = 4
POST_TILE_CHUNKS = 8
MIXER_TILE_CHUNKS = 8
STAGGER_LAG = 1
VMEM_LIMIT_BYTES = 60 * 1024 * 1024
FAST_PATH_MAX_DECAY = 40.0
FLAG_ROWS = 8
N_MIX_IN = 6
N_MIX_CONST = 10

F32 = jnp.float32
BF16 = jnp.bfloat16

assert POOL_WINDOWS == tuple(2 << g for g in range(POOL_GROUPS)) and POOL_WINDOWS[-1] == SUB


def _dot(a, b):
    return jnp.dot(a, b, preferred_element_type=F32)


def _dot_nt(a, b):
    return lax.dot_general(a, b, (((1,), (1,)), ((), ())), preferred_element_type=F32)


def _layer_norm(z, g, b):
    mu = jnp.mean(z, axis=-1, keepdims=True)
    zc = z - mu
    var = jnp.mean(zc * zc, axis=-1, keepdims=True)
    return zc * lax.rsqrt(var + LN_EPS) * g + b


def _silu(x):
    return x * jax.nn.sigmoid(x)


def _log_sigmoid(x):
    return jnp.minimum(x, 0.0) - jnp.log1p(jnp.exp(-jnp.abs(x)))


def _swiglu_half_step(x, wg_ref, wu_ref, wd_ref, alpha):
    xb = x.astype(BF16)
    fc = D_FF // FF_SPLIT
    y = None
    for c in range(FF_SPLIT):
        cols = slice(c * fc, (c + 1) * fc)
        g = _dot(xb, wg_ref[:, cols])
        u = _dot(xb, wu_ref[:, cols])
        yield
        act = (_silu(g) * u).astype(BF16)
        yield
        part = _dot(act, wd_ref[cols, :])
        y = part if y is None else y + part
    yield
    return alpha * x + 0.5 * y


class _Stagger:
    def __init__(self, gens, lag=STAGGER_LAG):
        self.gens = list(gens)
        self.lag = lag
        self.t = 0
        self.done = [False] * len(self.gens)
        self.results = [None] * len(self.gens)

    def tick(self):
        for k, gen in enumerate(self.gens):
            if self.done[k] or self.t < k * self.lag:
                continue
            try:
                next(gen)
            except StopIteration as stop:
                self.results[k], self.done[k] = stop.value, True
        self.t += 1
        return not all(self.done)


def _run_phases(*staggers):
    while any([s.tick() for s in staggers]):
        pass


def _staggered(gens, lag=STAGGER_LAG):
    s = _Stagger(gens, lag)
    _run_phases(s)
    return s.results


def _ffn_inproj_rows(alpha, x, r0, is_meta, wg_ref, wu_ref, wd_ref, lng_ref, lnb_ref,
                     wtm_ref, wcm_ref, wgt_ref, wgut_ref, bgc_ref,
                     h_ref, u_ref, k_ref, qt_ref, lat_ref, vt_ref, rt_ref, flag_ref):
    n = x.shape[0]
    rows = slice(r0, r0 + n)
    z = yield from _swiglu_half_step(x, wg_ref, wu_ref, wd_ref, alpha)
    h = _layer_norm(z, lng_ref[...], lnb_ref[...])
    h_ref[rows, :] = h
    hb = h.astype(BF16)
    yield

    if is_meta:
        pad = GLA_CHUNK - N_META
        keep_r = (lax.broadcasted_iota(jnp.int32, (n, 1), 0) % GLA_CHUNK) >= pad
        keep_l = (lax.broadcasted_iota(jnp.int32, (1, n), 1) % GLA_CHUNK) >= pad
        mask_r = lambda v: jnp.where(keep_r, v, 0.0)
        mask_l = lambda v: jnp.where(keep_l, v, 0.0)
    else:
        mask_r = mask_l = lambda v: v

    ztm = _dot(hb, wtm_ref[...])
    u_ref[rows, :] = mask_r(ztm[:, :POOL_WIDTH])
    k_ref[rows, :] = mask_r(ztm[:, POOL_WIDTH:])
    yield

    zt = _dot_nt(wcm_ref[...], hb)
    kw = GLA_KEY_WIDTH
    qt_ref[:, rows] = mask_l(zt[:kw])
    vt_ref[:, rows] = mask_l(zt[kw:kw + GLA_WIDTH]).astype(BF16)
    rt_ref[:, rows] = _silu(zt[kw + GLA_WIDTH:])
    gt = _dot_nt(wgt_ref[...], hb).astype(BF16)
    lat = mask_l(_log_sigmoid(_dot(wgut_ref[...], gt) + bgc_ref[...]) * (1.0 / GLA_GATE_TEMP))
    lat_ref[:, rows] = lat

    flags = []
    lane_lo = lax.broadcasted_iota(jnp.int32, (GLA_KEY_WIDTH, SC_ROWS), 1) < GLA_CHUNK
    for s in range(n // SC_ROWS):
        seg = lat[:, SC_ROWS * s:SC_ROWS * (s + 1)]
        tot = jnp.minimum(jnp.sum(jnp.where(lane_lo, seg, 0.0), axis=1, keepdims=True),
                          jnp.sum(jnp.where(lane_lo, 0.0, seg), axis=1, keepdims=True))
        flags.append((jnp.min(tot, axis=0, keepdims=True) < -FAST_PATH_MAX_DECAY).astype(jnp.int32))
    return flags


def _ffn_inproj_kernel(alpha, from_x, tile_rows, x_ref, *refs):
    if from_x:
        meta_ref, *refs = refs
    flag_ref = refs[-1]
    last = pl.num_programs(0) - 1

    def x_rows(r0, n):
        if not from_x:
            return x_ref[r0:r0 + n, :]
        pieces = [divmod(r0 // GLA_CHUNK + i, BATCH) for i in range(n // GLA_CHUNK)]
        return jnp.concatenate([x_ref[b, s] for s, b in pieces], axis=0)

    def run(parts, is_meta):
        gens = [_ffn_inproj_rows(alpha, x, r0, is_meta, *refs) for x, r0 in parts]
        sc_flags = [f for fl in _staggered(gens, STAGGER_LAG) for f in fl]
        frow = lax.broadcasted_iota(jnp.int32, (FLAG_ROWS, LANES), 0)
        flags = jnp.zeros((FLAG_ROWS, LANES), jnp.int32)
        for s, f in enumerate(sc_flags):
            flags = jnp.where(frow == s, f, flags)
        flag_ref[...] = flags

    @pl.when(pl.program_id(0) < last)
    def _():
        pr = min(PART_ROWS, tile_rows)
        run([(x_rows(r0, pr), r0) for r0 in range(0, tile_rows, pr)], False)

    @pl.when(pl.program_id(0) == last)
    def _():
        run([(meta_ref[...] if from_x else x_ref[0:SC_ROWS, :], 0)], True)


def _split3(x):
    hi = x.astype(BF16)
    r1 = x - hi.astype(F32)
    mid = r1.astype(BF16)
    lo = (r1 - mid.astype(F32)).astype(BF16)
    return hi, mid, lo


def _transpose_cm(xt):
    return jnp.concatenate([xt[:SC_ROWS].T, xt[SC_ROWS:].T], axis=1)


def _dot3_right(x, c01):
    parts = _split3(x)
    return _dot(parts[0], c01) + _dot(parts[1], c01) + _dot(parts[2], c01)


def _mix_chunks(n, robust, in_refs, const_refs, y_ref, st_ref, st_new_ref, ext_ref, b_scr, sd_scr, at_scr):
    u_ref, k_ref, qt_ref, lat_ref, vt_ref, rt_ref = in_refs
    (ubd_ref, uprev_ref, slot_ref, dmask_ref, causal_ref, eye_ref, gn_ref, cnt_ref,
     wpool_ref, pscale_ref) = const_refs
    C = GLA_CHUNK
    lane = lax.broadcasted_iota(jnp.int32, (SC_ROWS, SC_ROWS), 1)
    rowi = lax.broadcasted_iota(jnp.int32, (SC_ROWS, SC_ROWS), 0)
    lane64 = lax.broadcasted_iota(jnp.int32, (C, SC_ROWS), 1)
    lane8 = lax.broadcasted_iota(jnp.int32, (8, SC_ROWS), 1)
    lane_lo = lane < C
    row_lo = rowi < C
    row_head = lax.broadcasted_iota(jnp.int32, (GLA_KEY_WIDTH, SC_ROWS), 0) // GLA_DK

    def heads_on_lanes(xt):
        return jnp.concatenate([jnp.where(row_head == h, xt, 0.0) for h in range(GLA_HEADS)], axis=1).astype(BF16)

    for c in range(n):
        for b in range(BATCH):
            ext_ref[b, SUB + C * c:SUB + C * (c + 1), :] = u_ref[SC_ROWS * c + C * b:SC_ROWS * c + C * (b + 1), :]
    states = [st_ref[h] for h in range(GLA_HEADS)]
    cnt = cnt_ref[...]
    ubd = ubd_ref[...]
    eye = eye_ref[...]
    gn = gn_ref[...]

    def chunk(c):
        rows = slice(SC_ROWS * c, SC_ROWS * (c + 1))
        bT = _dot3_right(lat_ref[:, rows], ubd)

        u = u_ref[rows, :]
        win = []
        for b in range(BATCH):
            level = ext_ref[b, C * c:C * c + SUB + C, :]
            for g in range(POOL_GROUPS):
                level = level[:, POOL_GROUP_DIM if g else 0:]
                level = level + pltpu.roll(level, 1 << g, axis=0)
                win.append(level[SUB:, :POOL_GROUP_DIM])
        parts = []
        for g in range(POOL_GROUPS):
            cols = slice(POOL_GROUP_DIM * g, POOL_GROUP_DIM * (g + 1))
            s = jnp.concatenate([win[b * POOL_GROUPS + g] for b in range(BATCH)], axis=0)
            p = s / cnt[:, cols] - u[:, cols]
            parts.append(_dot(p.astype(BF16), wpool_ref[g]))
        y_pool = jnp.concatenate(parts, axis=1) * pscale_ref[...]
        yield

        b_tm = _transpose_cm(bT)
        b_scr[c] = b_tm
        ends = [[b_scr[c, pl.ds(C * b + SUB * j + SUB - 1, 1), :] for j in range(N_SUB)] for b in range(BATCH)]

        def per_block(fn):
            return jnp.concatenate(
                [jnp.broadcast_to(fn(b, j), (SUB, GLA_KEY_WIDTH)) for b in range(BATCH) for j in range(N_SUB)], axis=0)

        k_tm = k_ref[rows, :]
        qT = qt_ref[:, rows]
        e_last = per_block(lambda b, j: ends[b][N_SUB - 1])
        ktil = k_tm * jnp.exp(e_last - b_tm)
        qeT = qT * jnp.exp(bT)

        if not robust:
            kneg = (k_tm * jnp.exp(-b_tm)).astype(BF16)
            at_all = jnp.where(causal_ref[...] != 0, _dot(kneg, heads_on_lanes(qeT)), 0.0)
        else:
            cprevT = _dot3_right(lat_ref[:, rows], uprev_ref[...])
            e_own = per_block(lambda b, j: ends[b][j])
            f2 = per_block(lambda b, j: jnp.exp(ends[b][min(j + 1, N_SUB - 1)] - ends[b][j]))
            f3 = per_block(lambda b, j: jnp.exp(ends[b][min(j + 2, N_SUB - 1)] - ends[b][j]))
            khat = k_tm * jnp.exp(e_own - b_tm)
            kslots = jnp.concatenate([khat, khat * f2, khat * f3], axis=0).astype(BF16)
            r_all = _dot(kslots, heads_on_lanes(qT * jnp.exp(bT - cprevT)))
            slot = slot_ref[...]
            at_off = jnp.where(slot == 1, r_all[0:SC_ROWS],
                               jnp.where(slot == 2, r_all[SC_ROWS:2 * SC_ROWS],
                                         jnp.where(slot == 3, r_all[2 * SC_ROWS:], 0.0)))
            kT = jnp.concatenate([k_tm[:, :LANES].T, k_tm[:, LANES:].T], axis=0)
            for dist in range(SUB):
                if dist == 0:
                    qs, bs = qT, bT
                else:
                    qs = pltpu.roll(qT, SC_ROWS - dist, axis=1)
                    bs = pltpu.roll(bT, SC_ROWS - dist, axis=1)
                prod = qs * kT * jnp.exp(jnp.minimum(bs - bT, 0.0))
                for h in range(GLA_HEADS):
                    sd_scr[pl.ds(SUB * h + dist, 1), :] = jnp.sum(
                        prod[GLA_DK * h:GLA_DK * (h + 1)], axis=0, keepdims=True)
            sd = jnp.concatenate([sd_scr[...], jnp.zeros((SC_ROWS - GLA_HEADS * SUB, SC_ROWS), F32)], axis=0)
            sdt = sd.T
            dmask = dmask_ref[...]
            for h in range(GLA_HEADS):
                xh = jnp.where((lane >= SUB * h) & (lane < SUB * (h + 1)), sdt, 0.0)
                skew = pltpu.roll(xh, (SC_ROWS - SUB * h) % SC_ROWS, axis=1, stride=1, stride_axis=0)
                at_scr[:, SC_ROWS * h:SC_ROWS * (h + 1)] = (
                    jnp.where(dmask != 0, skew, 0.0) + at_off[:, SC_ROWS * h:SC_ROWS * (h + 1)])
            at_all = at_scr[...]
        yield

        vT = vt_ref[:, rows]
        rT = rt_ref[:, rows]
        results = []
        for h in range(GLA_HEADS):
            at_h = at_all[:, SC_ROWS * h:SC_ROWS * (h + 1)]

            half = (h % 2) * C
            kt_cols = ktil[:, LANES * (h // 2):LANES * (h // 2 + 1)]
            kt_roll = pltpu.roll(kt_cols, C, axis=1)
            lo_src, hi_src = (kt_cols, kt_roll) if half == 0 else (kt_roll, kt_cols)
            kbd = jnp.where(row_lo & lane_lo, lo_src, jnp.where((~row_lo) & (~lane_lo), hi_src, 0.0))
            qe_h = qeT[GLA_DK * h:GLA_DK * (h + 1)]
            qebd = jnp.concatenate([jnp.where(lane64 < C, qe_h, 0.0), jnp.where(lane64 >= C, qe_h, 0.0)], axis=0)

            v_h = vT[GLA_DV * h:GLA_DV * (h + 1)]
            res = _dot(v_h, jnp.concatenate([at_h, kbd], axis=1).astype(BF16))
            results.append((res, _dot(states[h].astype(BF16), qebd.astype(BF16))))
        yield

        y_heads = []
        for h in range(GLA_HEADS):
            half = (h % 2) * C
            res, o_inter = results[h]
            st = states[h]
            o_t = res[:, :SC_ROWS] + o_inter

            e0 = jnp.broadcast_to(ends[0][N_SUB - 1][:, LANES * (h // 2):LANES * (h // 2 + 1)], (8, LANES))
            e1 = jnp.broadcast_to(ends[1][N_SUB - 1][:, LANES * (h // 2):LANES * (h // 2 + 1)], (8, LANES))
            if half == 0:
                e1 = pltpu.roll(e1, C, axis=1)
            else:
                e0 = pltpu.roll(e0, C, axis=1)
            dec = jnp.exp(jnp.where(lane8 < C, e0, e1))
            states[h] = st * jnp.broadcast_to(dec[0:1], (GLA_DV, SC_ROWS)) + res[:, SC_ROWS:]

            ms = jnp.mean(o_t * o_t, axis=0, keepdims=True)
            gsl = slice(GLA_DV * h, GLA_DV * (h + 1))
            y_heads.append(o_t * lax.rsqrt(ms + RMS_EPS) * gn[gsl] * rT[gsl])
        y_t = jnp.concatenate(y_heads, axis=0).astype(BF16)
        y_gla = _dot_nt(eye, y_t)
        yield
        y_ref[rows, :] = jnp.concatenate([y_pool, y_gla], axis=1).astype(BF16)

    def in_turn():
        for c in range(n):
            yield from chunk(c)

    def finish():
        for h in range(GLA_HEADS):
            st_new_ref[h] = states[h]

    return _Stagger([in_turn()] if robust else [chunk(c) for c in range(n)]), finish


def _outproj_ffn_rows(alpha, h, y, store, wo_ref, l2g_ref, l2b_ref, wg_ref, wu_ref, wd_ref, l3g_ref, l3b_ref):
    z = alpha * h + _dot(y, wo_ref[...])
    yield
    h2 = _layer_norm(z, l2g_ref[...], l2b_ref[...])
    yield
    z2 = yield from _swiglu_half_step(h2, wg_ref, wu_ref, wd_ref, alpha)
    store(_layer_norm(z2, l3g_ref[...], l3b_ref[...]))


def _mixer_kernel(tc, flags_ref, *refs):
    in_refs, const_refs, y_ref = refs[:N_MIX_IN], refs[N_MIX_IN:N_MIX_IN + N_MIX_CONST], refs[N_MIX_IN + N_MIX_CONST]
    st_ref, ext_ref, b_scr, sd_scr, at_scr = refs[N_MIX_IN + N_MIX_CONST + 1:]
    step = pl.program_id(0)

    def run(n, robust):
        phases, finish = _mix_chunks(n, robust, in_refs, const_refs, y_ref, st_ref, st_ref, ext_ref,
                                     b_scr, sd_scr, at_scr)
        _run_phases(phases)
        finish()
        for b in range(BATCH):
            ext_ref[b, 0:SUB, :] = ext_ref[b, GLA_CHUNK * n:GLA_CHUNK * n + SUB, :]

    @pl.when(step == 0)
    def _():
        st_ref[...] = jnp.zeros(st_ref.shape, F32)
        ext_ref[:, 0:SUB, :] = jnp.zeros((BATCH, SUB, POOL_WIDTH), F32)
        run(1, True)

    @pl.when(step > 0)
    def _():
        base = (step - 1) * tc
        slow = flags_ref[base]
        for c in range(1, tc):
            slow = jnp.maximum(slow, flags_ref[base + c])

        @pl.when(slow == 0)
        def _():
            run(tc, False)

        @pl.when(slow != 0)
        def _():
            run(tc, True)


def _outproj_ffn_kernel(alpha, to_out, h_ref, y_ref, *refs):
    *w_refs, o_ref = refs
    tile_rows = h_ref.shape[0]

    def store_rows(r0):
        def store(o):
            o_ref[r0:r0 + o.shape[0], :] = o
        return store

    def store_chunks(r0):
        def store(o):
            for sb in range(o.shape[0] // GLA_CHUNK):
                s, b = divmod(r0 // GLA_CHUNK + sb, BATCH)
                o_ref[b, s] = o[sb * GLA_CHUNK:(sb + 1) * GLA_CHUNK]
        return store

    def parts(n_rows, n_parts, make_store):
        pr = n_rows // n_parts
        _staggered([_outproj_ffn_rows(alpha, h_ref[p * pr:(p + 1) * pr, :], y_ref[p * pr:(p + 1) * pr, :],
                                      make_store(p * pr), *w_refs) for p in range(n_parts)])

    n_parts = max(tile_rows // PART_ROWS, 1)
    if to_out:
        parts(tile_rows, n_parts, store_chunks)
        return
    last = pl.num_programs(0) - 1

    @pl.when(pl.program_id(0) < last)
    def _():
        parts(tile_rows, n_parts, store_rows)

    @pl.when(pl.program_id(0) == last)
    def _():
        parts(SC_ROWS, 1, store_rows)


def _resident(shape):
    nd = len(shape)
    return pl.BlockSpec(shape, lambda *_: (0,) * nd, pipeline_mode=pl.Buffered(1))


def _layer_resident(shape, layer):
    nd = len(shape) - 1
    return pl.BlockSpec((None,) + tuple(shape[1:]), lambda *_: (layer,) + (0,) * nd, pipeline_mode=pl.Buffered(1))


def _mixer_constants():
    r = np.arange(SC_ROWS)
    b, t = r // GLA_CHUNK, r % GLA_CHUNK
    same_b = b[:, None] == b[None, :]
    ubd = same_b & (t[:, None] <= t[None, :])
    blk = t // SUB
    uprev = same_b & (blk[:, None] < blk[None, :])
    dist = np.where(same_b, blk[None, :] - blk[:, None], 0)
    slot = np.where((dist >= 1) & (dist < N_SUB), dist, 0).astype(np.int32)
    slot = np.tile(slot, (1, GLA_HEADS))
    dmask = (same_b & (blk[:, None] == blk[None, :]) & (t[None, :] >= t[:, None])).astype(np.int32)
    causal = np.tile((same_b & (t[None, :] >= t[:, None])).astype(np.int32), (1, GLA_HEADS))
    eye = np.eye(SC_ROWS)
    w_lane = np.repeat(np.array(POOL_WINDOWS, np.float32), POOL_GROUP_DIM)[None, :]
    t_meta = np.maximum(t - (GLA_CHUNK - N_META), 0).astype(np.float32)[:, None]
    cnt = np.stack([np.broadcast_to(w_lane, (SC_ROWS, POOL_WIDTH)), np.minimum(t_meta + 1.0, w_lane)])
    return (jnp.asarray(ubd, BF16), jnp.asarray(uprev, BF16), jnp.asarray(slot),
            jnp.asarray(dmask), jnp.asarray(causal), jnp.asarray(eye, BF16), jnp.asarray(cnt, F32))


def _tile_chunks(n_ch, most):
    return next(k for k in (8, 4, 2, 1) if k <= most and n_ch % k == 0)


def kernel(x, meta_tokens, ffn1_w_gate, ffn1_w_up, ffn1_w_down, ln1_g, ln1_b, w_in, w_gate_up, b_gate, w_pool, pool_scale, gla_norm_g, w_out, ln2_g, ln2_b, ffn2_w_gate, ffn2_w_up, ffn2_w_down, ln3_g, ln3_b):
    batch, seq, d = x.shape
    depth = w_in.shape[0]
    assert batch == BATCH and d == D_MODEL and seq % GLA_CHUNK == 0
    n_ch = seq // GLA_CHUNK
    n_sc = n_ch + 1
    n_rows = n_sc * SC_ROWS
    tc = _tile_chunks(n_ch, DENSE_TILE_CHUNKS)
    tr = tc * SC_ROWS
    n_main = n_ch // tc
    tcm = _tile_chunks(n_ch, MIXER_TILE_CHUNKS)
    n_mix = n_ch // tcm
    tcp = _tile_chunks(n_ch, POST_TILE_CHUNKS)
    n_post = n_ch // tcp
    alpha = (2.0 * depth) ** 0.25

    x4 = x.reshape(batch, n_ch, GLA_CHUNK, d)
    meta_half = jnp.concatenate([jnp.zeros((GLA_CHUNK - N_META, d), x.dtype), meta_tokens.astype(x.dtype)], axis=0)
    meta_sc = jnp.concatenate([meta_half, meta_half], axis=0)
    h = None

    ubd, uprev, slot, dmask, causal, eye, cnt = _mixer_constants()
    s0 = POOL_WIDTH
    s1 = s0 + GLA_KEY_WIDTH
    s2 = s1 + GLA_KEY_WIDTH
    s3 = s2 + GLA_WIDTH
    s4 = s3 + GLA_WIDTH
    cparams = pltpu.CompilerParams(dimension_semantics=("arbitrary",), vmem_limit_bytes=VMEM_LIMIT_BYTES)
    row_tile = lambda cols: pl.BlockSpec((tr, cols), lambda i: (i, 0))
    col_tile = lambda rows: pl.BlockSpec((rows, tr), lambda i: (0, i))
    post_tile = pl.BlockSpec((tcp * SC_ROWS, d), lambda i: (i, 0))
    x4_tile = pl.BlockSpec((batch, tc, GLA_CHUNK, d), lambda i: (0, jnp.minimum(i, n_main - 1), 0, 0))
    mix_tile = lambda i: lax.rem(i + n_mix, n_mix + 1)
    mix_rows = lambda cols: pl.BlockSpec((tcm * SC_ROWS, cols), lambda i, f: (mix_tile(i), 0))
    mix_cols = lambda rows: pl.BlockSpec((rows, tcm * SC_ROWS), lambda i, f: (0, mix_tile(i)))

    bf = lambda w: w.astype(BF16)
    w1g, w1u, w1d = bf(ffn1_w_gate), bf(ffn1_w_up), bf(ffn1_w_down)
    w2g, w2u, w2d = bf(ffn2_w_gate), bf(ffn2_w_up), bf(ffn2_w_down)
    wo_all, wpool_all = bf(w_out), bf(w_pool)
    row3 = lambda p: p.reshape(depth, 1, p.shape[-1])
    l1g, l1b, l2g, l2b, l3g, l3b = (row3(p) for p in (ln1_g, ln1_b, ln2_g, ln2_b, ln3_g, ln3_b))
    pscale_all = row3(pool_scale)

    for l in range(depth):
        wl = w_in[l]
        glr_pad = jnp.zeros((d, LANES - GLA_GATE_RANK), F32)
        wtm = jnp.concatenate([wl[:, :s0], wl[:, s1:s2]], axis=1).astype(BF16)
        wcm = jnp.concatenate([wl[:, s0:s1] * (GLA_DK ** -0.5), wl[:, s2:s3], wl[:, s3:s4]], axis=1).T.astype(BF16)
        wgt = jnp.concatenate([wl[:, s4:], glr_pad], axis=1).T.astype(BF16)
        wgut = jnp.concatenate([w_gate_up[l], jnp.zeros((LANES - GLA_GATE_RANK, GLA_KEY_WIDTH), F32)], axis=0).T.astype(BF16)
        bgc = b_gate[l].reshape(GLA_KEY_WIDTH, 1)

        from_x = l == 0
        acts = (x4, meta_sc) if from_x else (h,)
        act_specs = [x4_tile, _resident(meta_sc.shape)] if from_x else [row_tile(d)]
        outs = pl.pallas_call(
            functools.partial(_ffn_inproj_kernel, alpha, from_x, tr),
            grid=(n_main + 1,),
            in_specs=act_specs + [_layer_resident(w1g.shape, l), _layer_resident(w1u.shape, l),
                                  _layer_resident(w1d.shape, l), _layer_resident(l1g.shape, l),
                                  _layer_resident(l1b.shape, l),
                                  _resident(wtm.shape), _resident(wcm.shape), _resident(wgt.shape),
                                  _resident(wgut.shape), _resident(bgc.shape)],
            out_specs=[row_tile(d), row_tile(POOL_WIDTH), row_tile(GLA_KEY_WIDTH),
                       col_tile(GLA_KEY_WIDTH), col_tile(GLA_KEY_WIDTH), col_tile(GLA_WIDTH), col_tile(GLA_WIDTH),
                       pl.BlockSpec((FLAG_ROWS, LANES), lambda i: (i, 0))],
            out_shape=[jax.ShapeDtypeStruct((n_rows, d), F32),
                       jax.ShapeDtypeStruct((n_rows, POOL_WIDTH), F32),
                       jax.ShapeDtypeStruct((n_rows, GLA_KEY_WIDTH), F32),
                       jax.ShapeDtypeStruct((GLA_KEY_WIDTH, n_rows), F32),
                       jax.ShapeDtypeStruct((GLA_KEY_WIDTH, n_rows), F32),
                       jax.ShapeDtypeStruct((GLA_WIDTH, n_rows), BF16),
                       jax.ShapeDtypeStruct((GLA_WIDTH, n_rows), F32),
                       jax.ShapeDtypeStruct(((n_main + 1) * FLAG_ROWS, LANES), jnp.int32)],
            compiler_params=cparams,
            name=f"ffn_inproj_{l}",
        )(*acts, w1g, w1u, w1d, l1g, l1b, wtm, wcm, wgt, wgut, bgc)
        h1, u_tm, k_tm, q_t, la_t, v_t, r_t, tile_flags = outs
        sc_flags = tile_flags.reshape(n_main + 1, FLAG_ROWS, LANES)[:, :tc, 0].reshape(-1)[:n_sc]

        gn = jnp.broadcast_to(gla_norm_g[l].reshape(GLA_WIDTH, 1), (GLA_WIDTH, SC_ROWS))
        mix_in = (u_tm, k_tm, q_t, la_t, v_t, r_t)
        mix_in_specs = [mix_rows(POOL_WIDTH), mix_rows(GLA_KEY_WIDTH), mix_cols(GLA_KEY_WIDTH),
                        mix_cols(GLA_KEY_WIDTH), mix_cols(GLA_WIDTH), mix_cols(GLA_WIDTH)]
        mix_const = (ubd, uprev, slot, dmask, causal, eye, gn, cnt, wpool_all, pscale_all)
        mix_const_specs = [_resident(ubd.shape), _resident(uprev.shape), _resident(slot.shape),
                           _resident(dmask.shape), _resident(causal.shape), _resident(eye.shape), _resident(gn.shape),
                           pl.BlockSpec((None, SC_ROWS, POOL_WIDTH), lambda i, f: (jnp.where(i == 0, 1, 0), 0, 0)),
                           _layer_resident(wpool_all.shape, l), _layer_resident(pscale_all.shape, l)]
        assert len(mix_in) == N_MIX_IN and len(mix_const) == N_MIX_CONST
        y_cat = pl.pallas_call(
            functools.partial(_mixer_kernel, tcm),
            grid_spec=pltpu.PrefetchScalarGridSpec(
                num_scalar_prefetch=1,
                grid=(n_mix + 1,),
                in_specs=mix_in_specs + mix_const_specs,
                out_specs=mix_rows(d),
                scratch_shapes=[pltpu.VMEM((GLA_HEADS, GLA_DV, SC_ROWS), F32),
                                pltpu.VMEM((BATCH, SUB + tcm * GLA_CHUNK, POOL_WIDTH), F32),
                                pltpu.VMEM((tcm, SC_ROWS, GLA_KEY_WIDTH), F32),
                                pltpu.VMEM((GLA_HEADS * SUB, SC_ROWS), F32),
                                pltpu.VMEM((SC_ROWS, GLA_HEADS * SC_ROWS), F32)]),
            out_shape=jax.ShapeDtypeStruct((n_rows, d), BF16),
            compiler_params=cparams,
            name=f"mixer_{l}",
        )(sc_flags, *mix_in, *mix_const)

        to_out = l == depth - 1
        h = pl.pallas_call(
            functools.partial(_outproj_ffn_kernel, alpha, to_out),
            grid=(n_post if to_out else n_post + 1,),
            in_specs=[post_tile, post_tile, _layer_resident(wo_all.shape, l),
                      _layer_resident(l2g.shape, l), _layer_resident(l2b.shape, l),
                      _layer_resident(w2g.shape, l), _layer_resident(w2u.shape, l), _layer_resident(w2d.shape, l),
                      _layer_resident(l3g.shape, l), _layer_resident(l3b.shape, l)],
            out_specs=pl.BlockSpec((batch, tcp, GLA_CHUNK, d), lambda i: (0, i, 0, 0)) if to_out else post_tile,
            out_shape=jax.ShapeDtypeStruct(x4.shape if to_out else (n_rows, d), F32),
            compiler_params=cparams,
            name=f"outproj_ffn_{l}",
        )(h1, y_cat, wo_all, l2g, l2b, w2g, w2u, w2d, l3g, l3b)

    return h.reshape(batch, seq, d)
```

```python
import functools

import numpy as np
import jax
import jax.numpy as jnp
from jax import lax
from jax.experimental import pallas as pl
from jax.experimental.pallas import tpu as pltpu

D_MODEL = 1024
N_META = 16
POOL_WIDTH = 512
POOL_GROUPS = 4
POOL_GROUP_DIM = 128
POOL_WINDOWS = (2, 4, 8, 16)
GLA_WIDTH = 512
GLA_HEADS = 4
GLA_KEY_WIDTH = 256
GLA_DK = 64
GLA_DV = 128
GLA_GATE_RANK = 16
GLA_GATE_TEMP = 16.0
GLA_CHUNK = 64
D_FF = 2816
LN_EPS = 1e-5
RMS_EPS = 1e-6

BATCH = 2
SC_ROWS = BATCH * GLA_CHUNK
SUB = 16
N_SUB = GLA_CHUNK // SUB
LANES = 128
FF_SPLIT = 1
PART_ROWS = 256
DENSE_TILE_CHUNKS = 4
POST_TILE_CHUNKS = 4
MIXER_TILE_CHUNKS = 8
STAGGER_LAG = 1
VMEM_LIMIT_BYTES = 60 * 1024 * 1024
FAST_PATH_MAX_DECAY = 40.0
FLAG_ROWS = 8
N_FFN_WEIGHTS = 3
N_INPROJ_OUT = 8
WEIGHT_SLAB = 256
N_MIX_IN = 6
N_MIX_CONST = 10

F32 = jnp.float32
BF16 = jnp.bfloat16

assert POOL_WINDOWS == tuple(2 << g for g in range(POOL_GROUPS)) and POOL_WINDOWS[-1] == SUB


def _dot(a, b):
    return jnp.dot(a, b, preferred_element_type=F32)


def _dot_nt(a, b):
    return lax.dot_general(a, b, (((1,), (1,)), ((), ())), preferred_element_type=F32)


def _layer_norm(z, g, b):
    mu = jnp.mean(z, axis=-1, keepdims=True)
    zc = z - mu
    var = jnp.mean(zc * zc, axis=-1, keepdims=True)
    return zc * lax.rsqrt(var + LN_EPS) * g + b


def _silu(x):
    return x * jax.nn.sigmoid(x)


def _log_sigmoid(x):
    return jnp.minimum(x, 0.0) - jnp.log1p(jnp.exp(-jnp.abs(x)))


def _swiglu_half_step(x, wg_ref, wu_ref, wd_ref, alpha):
    xb = x.astype(BF16)
    fc = D_FF // FF_SPLIT
    y = None
    for c in range(FF_SPLIT):
        cols = slice(c * fc, (c + 1) * fc)
        g = _dot(xb, wg_ref[:, cols])
        u = _dot(xb, wu_ref[:, cols])
        yield
        act = (_silu(g) * u).astype(BF16)
        yield
        part = _dot(act, wd_ref[cols, :])
        y = part if y is None else y + part
    yield
    return alpha * x + 0.5 * y


class _Stagger:
    def __init__(self, gens, lag=STAGGER_LAG):
        self.gens = list(gens)
        self.lag = lag
        self.t = 0
        self.done = [False] * len(self.gens)
        self.results = [None] * len(self.gens)

    def tick(self):
        for k, gen in enumerate(self.gens):
            if self.done[k] or self.t < k * self.lag:
                continue
            try:
                next(gen)
            except StopIteration as stop:
                self.results[k], self.done[k] = stop.value, True
        self.t += 1
        return not all(self.done)


def _run_phases(*staggers):
    while any([s.tick() for s in staggers]):
        pass


def _staggered(gens, lag=STAGGER_LAG):
    s = _Stagger(gens, lag)
    _run_phases(s)
    return s.results


def _ffn_inproj_rows(alpha, x, r0, is_meta, wg_ref, wu_ref, wd_ref, lng_ref, lnb_ref,
                     wtm_ref, wcm_ref, wgt_ref, wgut_ref, bgc_ref,
                     h_ref, u_ref, k_ref, qt_ref, lat_ref, vt_ref, rt_ref, flag_ref):
    n = x.shape[0]
    rows = slice(r0, r0 + n)
    z = yield from _swiglu_half_step(x, wg_ref, wu_ref, wd_ref, alpha)
    h = _layer_norm(z, lng_ref[...], lnb_ref[...])
    h_ref[rows, :] = h
    hb = h.astype(BF16)
    yield

    if is_meta:
        pad = GLA_CHUNK - N_META
        keep_r = (lax.broadcasted_iota(jnp.int32, (n, 1), 0) % GLA_CHUNK) >= pad
        keep_l = (lax.broadcasted_iota(jnp.int32, (1, n), 1) % GLA_CHUNK) >= pad
        mask_r = lambda v: jnp.where(keep_r, v, 0.0)
        mask_l = lambda v: jnp.where(keep_l, v, 0.0)
    else:
        mask_r = mask_l = lambda v: v

    ztm = _dot(hb, wtm_ref[...])
    u_ref[rows, :] = mask_r(ztm[:, :POOL_WIDTH])
    k_ref[rows, :] = mask_r(ztm[:, POOL_WIDTH:])
    yield

    zt = _dot_nt(wcm_ref[...], hb)
    kw = GLA_KEY_WIDTH
    qt_ref[:, rows] = mask_l(zt[:kw])
    vt_ref[:, rows] = mask_l(zt[kw:kw + GLA_WIDTH]).astype(BF16)
    rt_ref[:, rows] = _silu(zt[kw + GLA_WIDTH:])
    gt = _dot_nt(wgt_ref[...], hb).astype(BF16)
    lat = mask_l(_log_sigmoid(_dot(wgut_ref[...], gt) + bgc_ref[...]) * (1.0 / GLA_GATE_TEMP))
    lat_ref[:, rows] = lat

    flags = []
    lane_lo = lax.broadcasted_iota(jnp.int32, (GLA_KEY_WIDTH, SC_ROWS), 1) < GLA_CHUNK
    for s in range(n // SC_ROWS):
        seg = lat[:, SC_ROWS * s:SC_ROWS * (s + 1)]
        tot = jnp.minimum(jnp.sum(jnp.where(lane_lo, seg, 0.0), axis=1, keepdims=True),
                          jnp.sum(jnp.where(lane_lo, 0.0, seg), axis=1, keepdims=True))
        flags.append((jnp.min(tot, axis=0, keepdims=True) < -FAST_PATH_MAX_DECAY).astype(jnp.int32))
    return flags


def _cast_weight_slabs(plan, in_refs, out_refs):
    steps, slabs = plan
    step = pl.program_id(0)
    for m, (src, dst) in enumerate(zip(in_refs, out_refs)):
        @pl.when((step >= m * steps) & (step < m * steps + slabs))
        def _():
            dst[...] = src[...].astype(BF16)


def _ffn_inproj_kernel(alpha, from_x, tile_rows, cast_plan, x_ref, *refs):
    if from_x:
        meta_ref, *refs = refs
    if cast_plan is not None:
        n_par = len(refs) - 2 * N_FFN_WEIGHTS - N_INPROJ_OUT
        _cast_weight_slabs(cast_plan, refs[n_par:n_par + N_FFN_WEIGHTS], refs[-N_FFN_WEIGHTS:])
        refs = refs[:n_par] + refs[n_par + N_FFN_WEIGHTS:-N_FFN_WEIGHTS]
    flag_ref = refs[-1]
    last = pl.num_programs(0) - 1

    def x_rows(r0, n):
        if not from_x:
            return x_ref[r0:r0 + n, :]
        pieces = [divmod(r0 // GLA_CHUNK + i, BATCH) for i in range(n // GLA_CHUNK)]
        return jnp.concatenate([x_ref[b, s] for s, b in pieces], axis=0)

    def run(parts, is_meta):
        gens = [_ffn_inproj_rows(alpha, x, r0, is_meta, *refs) for x, r0 in parts]
        sc_flags = [f for fl in _staggered(gens, STAGGER_LAG) for f in fl]
        frow = lax.broadcasted_iota(jnp.int32, (FLAG_ROWS, LANES), 0)
        flags = jnp.zeros((FLAG_ROWS, LANES), jnp.int32)
        for s, f in enumerate(sc_flags):
            flags = jnp.where(frow == s, f, flags)
        flag_ref[...] = flags

    @pl.when(pl.program_id(0) < last)
    def _():
        pr = min(PART_ROWS, tile_rows)
        run([(x_rows(r0, pr), r0) for r0 in range(0, tile_rows, pr)], False)

    @pl.when(pl.program_id(0) == last)
    def _():
        run([(meta_ref[...] if from_x else x_ref[0:SC_ROWS, :], 0)], True)


def _split3(x):
    hi = x.astype(BF16)
    r1 = x - hi.astype(F32)
    mid = r1.astype(BF16)
    lo = (r1 - mid.astype(F32)).astype(BF16)
    return hi, mid, lo


def _transpose_cm(xt):
    return jnp.concatenate([xt[:SC_ROWS].T, xt[SC_ROWS:].T], axis=1)


def _dot3_right(x, c01):
    parts = _split3(x)
    return _dot(parts[0], c01) + _dot(parts[1], c01) + _dot(parts[2], c01)


def _mix_chunks(n, robust, in_refs, const_refs, y_ref, st_ref, st_new_ref, ext_ref, b_scr, sd_scr, at_scr):
    u_ref, k_ref, qt_ref, lat_ref, vt_ref, rt_ref = in_refs
    (ubd_ref, uprev_ref, slot_ref, dmask_ref, causal_ref, eye_ref, gn_ref, cnt_ref,
     wpool_ref, pscale_ref) = const_refs
    C = GLA_CHUNK
    lane = lax.broadcasted_iota(jnp.int32, (SC_ROWS, SC_ROWS), 1)
    rowi = lax.broadcasted_iota(jnp.int32, (SC_ROWS, SC_ROWS), 0)
    lane64 = lax.broadcasted_iota(jnp.int32, (C, SC_ROWS), 1)
    lane8 = lax.broadcasted_iota(jnp.int32, (8, SC_ROWS), 1)
    lane_lo = lane < C
    row_lo = rowi < C
    row_head = lax.broadcasted_iota(jnp.int32, (GLA_KEY_WIDTH, SC_ROWS), 0) // GLA_DK

    def heads_on_lanes(xt):
        return jnp.concatenate([jnp.where(row_head == h, xt, 0.0) for h in range(GLA_HEADS)], axis=1).astype(BF16)

    for c in range(n):
        for b in range(BATCH):
            ext_ref[b, SUB + C * c:SUB + C * (c + 1), :] = u_ref[SC_ROWS * c + C * b:SC_ROWS * c + C * (b + 1), :]
    states = [st_ref[h] for h in range(GLA_HEADS)]
    cnt = cnt_ref[...]
    ubd = ubd_ref[...]
    eye = eye_ref[...]
    gn = gn_ref[...]

    def chunk(c):
        rows = slice(SC_ROWS * c, SC_ROWS * (c + 1))
        bT = _dot3_right(lat_ref[:, rows], ubd)

        u = u_ref[rows, :]
        win = []
        for b in range(BATCH):
            level = ext_ref[b, C * c:C * c + SUB + C, :]
            for g in range(POOL_GROUPS):
                level = level[:, POOL_GROUP_DIM if g else 0:]
                level = level + pltpu.roll(level, 1 << g, axis=0)
                win.append(level[SUB:, :POOL_GROUP_DIM])
        parts = []
        for g in range(POOL_GROUPS):
            cols = slice(POOL_GROUP_DIM * g, POOL_GROUP_DIM * (g + 1))
            s = jnp.concatenate([win[b * POOL_GROUPS + g] for b in range(BATCH)], axis=0)
            p = s / cnt[:, cols] - u[:, cols]
            parts.append(_dot(p.astype(BF16), wpool_ref[g]))
        y_pool = jnp.concatenate(parts, axis=1) * pscale_ref[...]
        yield

        b_tm = _transpose_cm(bT)
        b_scr[c] = b_tm
        ends = [[b_scr[c, pl.ds(C * b + SUB * j + SUB - 1, 1), :] for j in range(N_SUB)] for b in range(BATCH)]

        def per_block(fn):
            return jnp.concatenate(
                [jnp.broadcast_to(fn(b, j), (SUB, GLA_KEY_WIDTH)) for b in range(BATCH) for j in range(N_SUB)], axis=0)

        k_tm = k_ref[rows, :]
        qT = qt_ref[:, rows]
        e_last = per_block(lambda b, j: ends[b][N_SUB - 1])
        ktil = k_tm * jnp.exp(e_last - b_tm)
        qeT = qT * jnp.exp(bT)

        if not robust:
            kneg = (k_tm * jnp.exp(-b_tm)).astype(BF16)
            at_all = jnp.where(causal_ref[...] != 0, _dot(kneg, heads_on_lanes(qeT)), 0.0)
        else:
            cprevT = _dot3_right(lat_ref[:, rows], uprev_ref[...])
            e_own = per_block(lambda b, j: ends[b][j])
            f2 = per_block(lambda b, j: jnp.exp(ends[b][min(j + 1, N_SUB - 1)] - ends[b][j]))
            f3 = per_block(lambda b, j: jnp.exp(ends[b][min(j + 2, N_SUB - 1)] - ends[b][j]))
            khat = k_tm * jnp.exp(e_own - b_tm)
            kslots = jnp.concatenate([khat, khat * f2, khat * f3], axis=0).astype(BF16)
            r_all = _dot(kslots, heads_on_lanes(qT * jnp.exp(bT - cprevT)))
            slot = slot_ref[...]
            at_off = jnp.where(slot == 1, r_all[0:SC_ROWS],
                               jnp.where(slot == 2, r_all[SC_ROWS:2 * SC_ROWS],
                                         jnp.where(slot == 3, r_all[2 * SC_ROWS:], 0.0)))
            kT = jnp.concatenate([k_tm[:, :LANES].T, k_tm[:, LANES:].T], axis=0)
            for dist in range(SUB):
                if dist == 0:
                    qs, bs = qT, bT
                else:
                    qs = pltpu.roll(qT, SC_ROWS - dist, axis=1)
                    bs = pltpu.roll(bT, SC_ROWS - dist, axis=1)
                prod = qs * kT * jnp.exp(jnp.minimum(bs - bT, 0.0))
                for h in range(GLA_HEADS):
                    sd_scr[pl.ds(SUB * h + dist, 1), :] = jnp.sum(
                        prod[GLA_DK * h:GLA_DK * (h + 1)], axis=0, keepdims=True)
            sd = jnp.concatenate([sd_scr[...], jnp.zeros((SC_ROWS - GLA_HEADS * SUB, SC_ROWS), F32)], axis=0)
            sdt = sd.T
            dmask = dmask_ref[...]
            for h in range(GLA_HEADS):
                xh = jnp.where((lane >= SUB * h) & (lane < SUB * (h + 1)), sdt, 0.0)
                skew = pltpu.roll(xh, (SC_ROWS - SUB * h) % SC_ROWS, axis=1, stride=1, stride_axis=0)
                at_scr[:, SC_ROWS * h:SC_ROWS * (h + 1)] = (
                    jnp.where(dmask != 0, skew, 0.0) + at_off[:, SC_ROWS * h:SC_ROWS * (h + 1)])
            at_all = at_scr[...]
        yield

        vT = vt_ref[:, rows]
        rT = rt_ref[:, rows]
        results = []
        for h in range(GLA_HEADS):
            at_h = at_all[:, SC_ROWS * h:SC_ROWS * (h + 1)]

            half = (h % 2) * C
            kt_cols = ktil[:, LANES * (h // 2):LANES * (h // 2 + 1)]
            kt_roll = pltpu.roll(kt_cols, C, axis=1)
            lo_src, hi_src = (kt_cols, kt_roll) if half == 0 else (kt_roll, kt_cols)
            kbd = jnp.where(row_lo & lane_lo, lo_src, jnp.where((~row_lo) & (~lane_lo), hi_src, 0.0))
            qe_h = qeT[GLA_DK * h:GLA_DK * (h + 1)]
            qebd = jnp.concatenate([jnp.where(lane64 < C, qe_h, 0.0), jnp.where(lane64 >= C, qe_h, 0.0)], axis=0)

            v_h = vT[GLA_DV * h:GLA_DV * (h + 1)]
            res = _dot(v_h, jnp.concatenate([at_h, kbd], axis=1).astype(BF16))
            results.append((res, _dot(states[h].astype(BF16), qebd.astype(BF16))))
        yield

        y_heads = []
        for h in range(GLA_HEADS):
            half = (h % 2) * C
            res, o_inter = results[h]
            st = states[h]
            o_t = res[:, :SC_ROWS] + o_inter

            e0 = jnp.broadcast_to(ends[0][N_SUB - 1][:, LANES * (h // 2):LANES * (h // 2 + 1)], (8, LANES))
            e1 = jnp.broadcast_to(ends[1][N_SUB - 1][:, LANES * (h // 2):LANES * (h // 2 + 1)], (8, LANES))
            if half == 0:
                e1 = pltpu.roll(e1, C, axis=1)
            else:
                e0 = pltpu.roll(e0, C, axis=1)
            dec = jnp.exp(jnp.where(lane8 < C, e0, e1))
            states[h] = st * jnp.broadcast_to(dec[0:1], (GLA_DV, SC_ROWS)) + res[:, SC_ROWS:]

            ms = jnp.mean(o_t * o_t, axis=0, keepdims=True)
            gsl = slice(GLA_DV * h, GLA_DV * (h + 1))
            y_heads.append(o_t * lax.rsqrt(ms + RMS_EPS) * gn[gsl] * rT[gsl])
        y_t = jnp.concatenate(y_heads, axis=0).astype(BF16)
        y_gla = _dot_nt(eye, y_t)
        yield
        y_ref[rows, :] = jnp.concatenate([y_pool, y_gla], axis=1).astype(BF16)

    def in_turn():
        for c in range(n):
            yield from chunk(c)

    def finish():
        for h in range(GLA_HEADS):
            st_new_ref[h] = states[h]

    return _Stagger([in_turn()] if robust else [chunk(c) for c in range(n)]), finish


def _outproj_ffn_rows(alpha, h, y, store, wo_ref, l2g_ref, l2b_ref, wg_ref, wu_ref, wd_ref, l3g_ref, l3b_ref):
    z = alpha * h + _dot(y, wo_ref[...])
    yield
    h2 = _layer_norm(z, l2g_ref[...], l2b_ref[...])
    yield
    z2 = yield from _swiglu_half_step(h2, wg_ref, wu_ref, wd_ref, alpha)
    store(_layer_norm(z2, l3g_ref[...], l3b_ref[...]))


def _mixer_kernel(tc, flags_ref, *refs):
    in_refs, const_refs, y_ref = refs[:N_MIX_IN], refs[N_MIX_IN:N_MIX_IN + N_MIX_CONST], refs[N_MIX_IN + N_MIX_CONST]
    st_ref, ext_ref, b_scr, sd_scr, at_scr = refs[N_MIX_IN + N_MIX_CONST + 1:]
    step = pl.program_id(0)

    def run(n, robust):
        phases, finish = _mix_chunks(n, robust, in_refs, const_refs, y_ref, st_ref, st_ref, ext_ref,
                                     b_scr, sd_scr, at_scr)
        _run_phases(phases)
        finish()
        for b in range(BATCH):
            ext_ref[b, 0:SUB, :] = ext_ref[b, GLA_CHUNK * n:GLA_CHUNK * n + SUB, :]

    @pl.when(step == 0)
    def _():
        st_ref[...] = jnp.zeros(st_ref.shape, F32)
        ext_ref[:, 0:SUB, :] = jnp.zeros((BATCH, SUB, POOL_WIDTH), F32)
        run(1, True)

    @pl.when(step > 0)
    def _():
        base = (step - 1) * tc
        slow = flags_ref[base]
        for c in range(1, tc):
            slow = jnp.maximum(slow, flags_ref[base + c])

        @pl.when(slow == 0)
        def _():
            run(tc, False)

        @pl.when(slow != 0)
        def _():
            run(tc, True)


def _outproj_ffn_kernel(alpha, to_out, cast_plan, h_ref, y_ref, *refs):
    if cast_plan is not None:
        n_in = len(refs) - 2 * N_FFN_WEIGHTS - 1
        _cast_weight_slabs(cast_plan, refs[n_in:n_in + N_FFN_WEIGHTS], refs[-N_FFN_WEIGHTS:])
        refs = refs[:n_in] + refs[n_in + N_FFN_WEIGHTS:-N_FFN_WEIGHTS]
    *w_refs, o_ref = refs
    tile_rows = h_ref.shape[0]

    def store_rows(r0):
        def store(o):
            o_ref[r0:r0 + o.shape[0], :] = o
        return store

    def store_chunks(r0):
        def store(o):
            for sb in range(o.shape[0] // GLA_CHUNK):
                s, b = divmod(r0 // GLA_CHUNK + sb, BATCH)
                o_ref[b, s] = o[sb * GLA_CHUNK:(sb + 1) * GLA_CHUNK]
        return store

    def parts(n_rows, n_parts, make_store):
        pr = n_rows // n_parts
        _staggered([_outproj_ffn_rows(alpha, h_ref[p * pr:(p + 1) * pr, :], y_ref[p * pr:(p + 1) * pr, :],
                                      make_store(p * pr), *w_refs) for p in range(n_parts)])

    n_parts = max(tile_rows // PART_ROWS, 1)
    if to_out:
        parts(tile_rows, n_parts, store_chunks)
        return
    last = pl.num_programs(0) - 1

    @pl.when(pl.program_id(0) < last)
    def _():
        parts(tile_rows, n_parts, store_rows)

    @pl.when(pl.program_id(0) == last)
    def _():
        parts(SC_ROWS, 1, store_rows)


def _resident(shape):
    nd = len(shape)
    return pl.BlockSpec(shape, lambda *_: (0,) * nd, pipeline_mode=pl.Buffered(1))


def _layer_resident(shape, layer):
    nd = len(shape) - 1
    return pl.BlockSpec((None,) + tuple(shape[1:]), lambda *_: (layer,) + (0,) * nd, pipeline_mode=pl.Buffered(1))


def _mixer_constants():
    r = np.arange(SC_ROWS)
    b, t = r // GLA_CHUNK, r % GLA_CHUNK
    same_b = b[:, None] == b[None, :]
    ubd = same_b & (t[:, None] <= t[None, :])
    blk = t // SUB
    uprev = same_b & (blk[:, None] < blk[None, :])
    dist = np.where(same_b, blk[None, :] - blk[:, None], 0)
    slot = np.where((dist >= 1) & (dist < N_SUB), dist, 0).astype(np.int32)
    slot = np.tile(slot, (1, GLA_HEADS))
    dmask = (same_b & (blk[:, None] == blk[None, :]) & (t[None, :] >= t[:, None])).astype(np.int32)
    causal = np.tile((same_b & (t[None, :] >= t[:, None])).astype(np.int32), (1, GLA_HEADS))
    eye = np.eye(SC_ROWS)
    w_lane = np.repeat(np.array(POOL_WINDOWS, np.float32), POOL_GROUP_DIM)[None, :]
    t_meta = np.maximum(t - (GLA_CHUNK - N_META), 0).astype(np.float32)[:, None]
    cnt = np.stack([np.broadcast_to(w_lane, (SC_ROWS, POOL_WIDTH)), np.minimum(t_meta + 1.0, w_lane)])
    return (jnp.asarray(ubd, BF16), jnp.asarray(uprev, BF16), jnp.asarray(slot),
            jnp.asarray(dmask), jnp.asarray(causal), jnp.asarray(eye, BF16), jnp.asarray(cnt, F32))


def _tile_chunks(n_ch, most):
    return next(k for k in (8, 4, 2, 1) if k <= most and n_ch % k == 0)


def kernel(x, meta_tokens, ffn1_w_gate, ffn1_w_up, ffn1_w_down, ln1_g, ln1_b, w_in, w_gate_up, b_gate, w_pool, pool_scale, gla_norm_g, w_out, ln2_g, ln2_b, ffn2_w_gate, ffn2_w_up, ffn2_w_down, ln3_g, ln3_b):
    batch, seq, d = x.shape
    depth = w_in.shape[0]
    assert batch == BATCH and d == D_MODEL and seq % GLA_CHUNK == 0
    n_ch = seq // GLA_CHUNK
    n_sc = n_ch + 1
    n_rows = n_sc * SC_ROWS
    tc = _tile_chunks(n_ch, DENSE_TILE_CHUNKS)
    tr = tc * SC_ROWS
    n_main = n_ch // tc
    tcm = _tile_chunks(n_ch, MIXER_TILE_CHUNKS)
    n_mix = n_ch // tcm
    tcp = _tile_chunks(n_ch, POST_TILE_CHUNKS)
    n_post = n_ch // tcp
    alpha = (2.0 * depth) ** 0.25

    x4 = x.reshape(batch, n_ch, GLA_CHUNK, d)
    meta_half = jnp.concatenate([jnp.zeros((GLA_CHUNK - N_META, d), x.dtype), meta_tokens.astype(x.dtype)], axis=0)
    meta_sc = jnp.concatenate([meta_half, meta_half], axis=0)
    h = None

    ubd, uprev, slot, dmask, causal, eye, cnt = _mixer_constants()
    s0 = POOL_WIDTH
    s1 = s0 + GLA_KEY_WIDTH
    s2 = s1 + GLA_KEY_WIDTH
    s3 = s2 + GLA_WIDTH
    s4 = s3 + GLA_WIDTH
    cparams = pltpu.CompilerParams(dimension_semantics=("arbitrary",), vmem_limit_bytes=VMEM_LIMIT_BYTES)
    row_tile = lambda cols: pl.BlockSpec((tr, cols), lambda i: (i, 0))
    col_tile = lambda rows: pl.BlockSpec((rows, tr), lambda i: (0, i))
    post_tile = pl.BlockSpec((tcp * SC_ROWS, d), lambda i: (i, 0))
    x4_tile = pl.BlockSpec((batch, tc, GLA_CHUNK, d), lambda i: (0, jnp.minimum(i, n_main - 1), 0, 0))
    mix_tile = lambda i: lax.rem(i + n_mix, n_mix + 1)
    mix_rows = lambda cols: pl.BlockSpec((tcm * SC_ROWS, cols), lambda i, f: (mix_tile(i), 0))
    mix_cols = lambda rows: pl.BlockSpec((rows, tcm * SC_ROWS), lambda i, f: (0, mix_tile(i)))

    ffn1 = tuple(w[0].astype(BF16) for w in (ffn1_w_gate, ffn1_w_up, ffn1_w_down))
    ffn_specs = [_resident(w.shape) for w in ffn1]
    wo_all, wpool_all = w_out.astype(BF16), w_pool.astype(BF16)
    row3 = lambda p: p.reshape(depth, 1, p.shape[-1])
    l1g, l1b, l2g, l2b, l3g, l3b = (row3(p) for p in (ln1_g, ln1_b, ln2_g, ln2_b, ln3_g, ln3_b))
    pscale_all = row3(pool_scale)

    def cast_job(n_steps, layer, stacked):
        steps = n_steps // N_FFN_WEIGHTS
        assert steps >= 1, "weight-cast side job needs at least three grid steps"
        width, slabs = (WEIGHT_SLAB, D_FF // WEIGHT_SLAB) if steps >= D_FF // WEIGHT_SLAB else (D_FF, 1)
        slab = lambda m: (lambda i: jnp.clip(i - m * steps, 0, slabs - 1))
        in_specs = [pl.BlockSpec((None, d, width), lambda i, j=slab(0): (layer, 0, j(i))),
                    pl.BlockSpec((None, d, width), lambda i, j=slab(1): (layer, 0, j(i))),
                    pl.BlockSpec((None, width, d), lambda i, j=slab(2): (layer, j(i), 0))]
        out_specs = [pl.BlockSpec((d, width), lambda i, j=slab(0): (0, j(i))),
                     pl.BlockSpec((d, width), lambda i, j=slab(1): (0, j(i))),
                     pl.BlockSpec((width, d), lambda i, j=slab(2): (j(i), 0))]
        out_shapes = [jax.ShapeDtypeStruct(w.shape[1:], BF16) for w in stacked]
        return (steps, slabs), in_specs, out_specs, out_shapes

    for l in range(depth):
        wl = w_in[l]
        glr_pad = jnp.zeros((d, LANES - GLA_GATE_RANK), F32)
        wtm = jnp.concatenate([wl[:, :s0], wl[:, s1:s2]], axis=1).astype(BF16)
        wcm = jnp.concatenate([wl[:, s0:s1] * (GLA_DK ** -0.5), wl[:, s2:s3], wl[:, s3:s4]], axis=1).T.astype(BF16)
        wgt = jnp.concatenate([wl[:, s4:], glr_pad], axis=1).T.astype(BF16)
        wgut = jnp.concatenate([w_gate_up[l], jnp.zeros((LANES - GLA_GATE_RANK, GLA_KEY_WIDTH), F32)], axis=0).T.astype(BF16)
        bgc = b_gate[l].reshape(GLA_KEY_WIDTH, 1)

        from_x = l == 0
        acts = (x4, meta_sc) if from_x else (h,)
        act_specs = [x4_tile, _resident(meta_sc.shape)] if from_x else [row_tile(d)]
        ffn2_f32 = (ffn2_w_gate, ffn2_w_up, ffn2_w_down)
        plan, cast_in, cast_out, cast_shapes = cast_job(n_main + 1, l, ffn2_f32)
        outs = pl.pallas_call(
            functools.partial(_ffn_inproj_kernel, alpha, from_x, tr, plan),
            grid=(n_main + 1,),
            in_specs=act_specs + ffn_specs + [_layer_resident(l1g.shape, l), _layer_resident(l1b.shape, l),
                                              _resident(wtm.shape), _resident(wcm.shape), _resident(wgt.shape),
                                              _resident(wgut.shape), _resident(bgc.shape)] + cast_in,
            out_specs=[row_tile(d), row_tile(POOL_WIDTH), row_tile(GLA_KEY_WIDTH),
                       col_tile(GLA_KEY_WIDTH), col_tile(GLA_KEY_WIDTH), col_tile(GLA_WIDTH), col_tile(GLA_WIDTH),
                       pl.BlockSpec((FLAG_ROWS, LANES), lambda i: (i, 0))] + cast_out,
            out_shape=[jax.ShapeDtypeStruct((n_rows, d), F32),
                       jax.ShapeDtypeStruct((n_rows, POOL_WIDTH), F32),
                       jax.ShapeDtypeStruct((n_rows, GLA_KEY_WIDTH), F32),
                       jax.ShapeDtypeStruct((GLA_KEY_WIDTH, n_rows), F32),
                       jax.ShapeDtypeStruct((GLA_KEY_WIDTH, n_rows), F32),
                       jax.ShapeDtypeStruct((GLA_WIDTH, n_rows), BF16),
                       jax.ShapeDtypeStruct((GLA_WIDTH, n_rows), F32),
                       jax.ShapeDtypeStruct(((n_main + 1) * FLAG_ROWS, LANES), jnp.int32)] + cast_shapes,
            compiler_params=cparams,
            name=f"ffn_inproj_{l}",
        )(*acts, *ffn1, l1g, l1b, wtm, wcm, wgt, wgut, bgc, *ffn2_f32)
        h1, u_tm, k_tm, q_t, la_t, v_t, r_t, tile_flags = outs[:N_INPROJ_OUT]
        ffn2 = outs[N_INPROJ_OUT:]
        sc_flags = tile_flags.reshape(n_main + 1, FLAG_ROWS, LANES)[:, :tc, 0].reshape(-1)[:n_sc]

        gn = jnp.broadcast_to(gla_norm_g[l].reshape(GLA_WIDTH, 1), (GLA_WIDTH, SC_ROWS))
        mix_in = (u_tm, k_tm, q_t, la_t, v_t, r_t)
        mix_in_specs = [mix_rows(POOL_WIDTH), mix_rows(GLA_KEY_WIDTH), mix_cols(GLA_KEY_WIDTH),
                        mix_cols(GLA_KEY_WIDTH), mix_cols(GLA_WIDTH), mix_cols(GLA_WIDTH)]
        mix_const = (ubd, uprev, slot, dmask, causal, eye, gn, cnt, wpool_all, pscale_all)
        mix_const_specs = [_resident(ubd.shape), _resident(uprev.shape), _resident(slot.shape),
                           _resident(dmask.shape), _resident(causal.shape), _resident(eye.shape), _resident(gn.shape),
                           pl.BlockSpec((None, SC_ROWS, POOL_WIDTH), lambda i, f: (jnp.where(i == 0, 1, 0), 0, 0)),
                           _layer_resident(wpool_all.shape, l), _layer_resident(pscale_all.shape, l)]
        assert len(mix_in) == N_MIX_IN and len(mix_const) == N_MIX_CONST
        y_cat = pl.pallas_call(
            functools.partial(_mixer_kernel, tcm),
            grid_spec=pltpu.PrefetchScalarGridSpec(
                num_scalar_prefetch=1,
                grid=(n_mix + 1,),
                in_specs=mix_in_specs + mix_const_specs,
                out_specs=mix_rows(d),
                scratch_shapes=[pltpu.VMEM((GLA_HEADS, GLA_DV, SC_ROWS), F32),
                                pltpu.VMEM((BATCH, SUB + tcm * GLA_CHUNK, POOL_WIDTH), F32),
                                pltpu.VMEM((tcm, SC_ROWS, GLA_KEY_WIDTH), F32),
                                pltpu.VMEM((GLA_HEADS * SUB, SC_ROWS), F32),
                                pltpu.VMEM((SC_ROWS, GLA_HEADS * SC_ROWS), F32)]),
            out_shape=jax.ShapeDtypeStruct((n_rows, d), BF16),
            compiler_params=cparams,
            name=f"mixer_{l}",
        )(sc_flags, *mix_in, *mix_const)

        to_out = l == depth - 1
        post_specs = [post_tile, post_tile, _layer_resident(wo_all.shape, l),
                      _layer_resident(l2g.shape, l), _layer_resident(l2b.shape, l)] + ffn_specs + [
                      _layer_resident(l3g.shape, l), _layer_resident(l3b.shape, l)]
        post_args = (h1, y_cat, wo_all, l2g, l2b, *ffn2, l3g, l3b)
        if to_out:
            h = pl.pallas_call(
                functools.partial(_outproj_ffn_kernel, alpha, True, None),
                grid=(n_post,),
                in_specs=post_specs,
                out_specs=pl.BlockSpec((batch, tcp, GLA_CHUNK, d), lambda i: (0, i, 0, 0)),
                out_shape=jax.ShapeDtypeStruct(x4.shape, F32),
                compiler_params=cparams,
                name=f"outproj_ffn_{l}",
            )(*post_args)
        else:
            ffn1_f32 = (ffn1_w_gate, ffn1_w_up, ffn1_w_down)
            plan, cast_in, cast_out, cast_shapes = cast_job(n_post + 1, l + 1, ffn1_f32)
            h, *ffn1 = pl.pallas_call(
                functools.partial(_outproj_ffn_kernel, alpha, False, plan),
                grid=(n_post + 1,),
                in_specs=post_specs + cast_in,
                out_specs=[post_tile] + cast_out,
                out_shape=[jax.ShapeDtypeStruct((n_rows, d), F32)] + cast_shapes,
                compiler_params=cparams,
                name=f"outproj_ffn_{l}",
            )(*post_args, *ffn1_f32)

    return h.reshape(batch, seq, d)
```

```python
import functools

import numpy as np
import jax
import jax.numpy as jnp
from jax import lax
from jax.experimental import pallas as pl
from jax.experimental.pallas import tpu as pltpu

D_MODEL = 1024
N_META = 16
POOL_WIDTH = 512
POOL_GROUPS = 4
POOL_GROUP_DIM = 128
POOL_WINDOWS = (2, 4, 8, 16)
GLA_WIDTH = 512
GLA_HEADS = 4
GLA_KEY_WIDTH = 256
GLA_DK = 64
GLA_DV = 128
GLA_GATE_RANK = 16
GLA_GATE_TEMP = 16.0
GLA_CHUNK = 64
D_FF = 2816
LN_EPS = 1e-5
RMS_EPS = 1e-6

BATCH = 2
SC_ROWS = BATCH * GLA_CHUNK
SUB = 16
N_SUB = GLA_CHUNK // SUB
LANES = 128
FF_SPLIT = 1
PART_ROWS = 256
DENSE_TILE_CHUNKS = 4
POST_TILE_CHUNKS = 4
MIXER_TILE_CHUNKS = 8
STAGGER_LAG = 1
VMEM_LIMIT_BYTES = 60 * 1024 * 1024
FAST_PATH_MAX_DECAY = 40.0
FLAG_ROWS = 8
N_FFN_WEIGHTS = 3
N_INPROJ_OUT = 8
WEIGHT_SLAB = 256
N_MIX_IN = 6
N_MIX_CONST = 10

F32 = jnp.float32
BF16 = jnp.bfloat16

assert POOL_WINDOWS == tuple(2 << g for g in range(POOL_GROUPS)) and POOL_WINDOWS[-1] == SUB


def _dot(a, b):
    return jnp.dot(a, b, preferred_element_type=F32)


def _dot_nt(a, b):
    return lax.dot_general(a, b, (((1,), (1,)), ((), ())), preferred_element_type=F32)


def _layer_norm(z, g, b):
    mu = jnp.mean(z, axis=-1, keepdims=True)
    zc = z - mu
    var = jnp.mean(zc * zc, axis=-1, keepdims=True)
    return zc * lax.rsqrt(var + LN_EPS) * g + b


def _silu(x):
    return x * jax.nn.sigmoid(x)


def _log_sigmoid(x):
    return jnp.minimum(x, 0.0) - jnp.log1p(jnp.exp(-jnp.abs(x)))


def _swiglu_half_step(x, wg_ref, wu_ref, wd_ref, alpha):
    xb = x.astype(BF16)
    fc = D_FF // FF_SPLIT
    y = None
    for c in range(FF_SPLIT):
        cols = slice(c * fc, (c + 1) * fc)
        g = _dot(xb, wg_ref[:, cols])
        u = _dot(xb, wu_ref[:, cols])
        yield
        act = (_silu(g) * u).astype(BF16)
        yield
        part = _dot(act, wd_ref[cols, :])
        y = part if y is None else y + part
    yield
    return alpha * x + 0.5 * y


class _Stagger:
    def __init__(self, gens, lag=STAGGER_LAG):
        self.gens = list(gens)
        self.lag = lag
        self.t = 0
        self.done = [False] * len(self.gens)
        self.results = [None] * len(self.gens)

    def tick(self):
        for k, gen in enumerate(self.gens):
            if self.done[k] or self.t < k * self.lag:
                continue
            try:
                next(gen)
            except StopIteration as stop:
                self.results[k], self.done[k] = stop.value, True
        self.t += 1
        return not all(self.done)


def _run_phases(*staggers):
    while any([s.tick() for s in staggers]):
        pass


def _staggered(gens, lag=STAGGER_LAG):
    s = _Stagger(gens, lag)
    _run_phases(s)
    return s.results


def _ffn_inproj_rows(alpha, x, r0, is_meta, wg_ref, wu_ref, wd_ref, lng_ref, lnb_ref,
                     wtm_ref, wcm_ref, wgt_ref, wgut_ref, bgc_ref,
                     h_ref, u_ref, k_ref, qt_ref, lat_ref, vt_ref, rt_ref, flag_ref):
    n = x.shape[0]
    rows = slice(r0, r0 + n)
    z = yield from _swiglu_half_step(x, wg_ref, wu_ref, wd_ref, alpha)
    h = _layer_norm(z, lng_ref[...], lnb_ref[...])
    h_ref[rows, :] = h
    hb = h.astype(BF16)
    yield

    if is_meta:
        pad = GLA_CHUNK - N_META
        keep_r = (lax.broadcasted_iota(jnp.int32, (n, 1), 0) % GLA_CHUNK) >= pad
        keep_l = (lax.broadcasted_iota(jnp.int32, (1, n), 1) % GLA_CHUNK) >= pad
        mask_r = lambda v: jnp.where(keep_r, v, 0.0)
        mask_l = lambda v: jnp.where(keep_l, v, 0.0)
    else:
        mask_r = mask_l = lambda v: v

    qv = GLA_KEY_WIDTH + GLA_WIDTH
    gt = _dot_nt(wgt_ref[...], hb).astype(BF16)
    zt = _dot_nt(wcm_ref[:qv, :], hb)
    qt_ref[:, rows] = mask_l(zt[:GLA_KEY_WIDTH])
    vt_ref[:, rows] = mask_l(zt[GLA_KEY_WIDTH:]).astype(BF16)
    lat = mask_l(_log_sigmoid(_dot(wgut_ref[...], gt) + bgc_ref[...]) * (1.0 / GLA_GATE_TEMP))
    lat_ref[:, rows] = lat
    ztm = _dot(hb, wtm_ref[...])
    u_ref[rows, :] = mask_r(ztm[:, :POOL_WIDTH])
    k_ref[rows, :] = mask_r(ztm[:, POOL_WIDTH:])
    flags = []
    lane_lo = lax.broadcasted_iota(jnp.int32, (GLA_KEY_WIDTH, SC_ROWS), 1) < GLA_CHUNK
    for s in range(n // SC_ROWS):
        seg = lat[:, SC_ROWS * s:SC_ROWS * (s + 1)]
        tot = jnp.minimum(jnp.sum(jnp.where(lane_lo, seg, 0.0), axis=1, keepdims=True),
                          jnp.sum(jnp.where(lane_lo, 0.0, seg), axis=1, keepdims=True))
        flags.append((jnp.min(tot, axis=0, keepdims=True) < -FAST_PATH_MAX_DECAY).astype(jnp.int32))
    yield

    rt_ref[:, rows] = _silu(_dot_nt(wcm_ref[qv:, :], hb))
    return flags


def _cast_weight_slabs(plan, in_refs, out_refs):
    steps, slabs = plan
    step = pl.program_id(0)
    for m, (src, dst) in enumerate(zip(in_refs, out_refs)):
        @pl.when((step >= m * steps) & (step < m * steps + slabs))
        def _():
            dst[...] = src[...].astype(BF16)


def _ffn_inproj_kernel(alpha, from_x, tile_rows, cast_plan, x_ref, *refs):
    if from_x:
        meta_ref, *refs = refs
    if cast_plan is not None:
        n_par = len(refs) - 2 * N_FFN_WEIGHTS - N_INPROJ_OUT
        _cast_weight_slabs(cast_plan, refs[n_par:n_par + N_FFN_WEIGHTS], refs[-N_FFN_WEIGHTS:])
        refs = refs[:n_par] + refs[n_par + N_FFN_WEIGHTS:-N_FFN_WEIGHTS]
    flag_ref = refs[-1]
    last = pl.num_programs(0) - 1

    def x_rows(r0, n):
        if not from_x:
            return x_ref[r0:r0 + n, :]
        pieces = [divmod(r0 // GLA_CHUNK + i, BATCH) for i in range(n // GLA_CHUNK)]
        return jnp.concatenate([x_ref[b, s] for s, b in pieces], axis=0)

    def run(parts, is_meta):
        gens = [_ffn_inproj_rows(alpha, x, r0, is_meta, *refs) for x, r0 in parts]
        sc_flags = [f for fl in _staggered(gens, STAGGER_LAG) for f in fl]
        frow = lax.broadcasted_iota(jnp.int32, (FLAG_ROWS, LANES), 0)
        flags = jnp.zeros((FLAG_ROWS, LANES), jnp.int32)
        for s, f in enumerate(sc_flags):
            flags = jnp.where(frow == s, f, flags)
        flag_ref[...] = flags

    @pl.when(pl.program_id(0) < last)
    def _():
        pr = min(PART_ROWS, tile_rows)
        run([(x_rows(r0, pr), r0) for r0 in range(0, tile_rows, pr)], False)

    @pl.when(pl.program_id(0) == last)
    def _():
        run([(meta_ref[...] if from_x else x_ref[0:SC_ROWS, :], 0)], True)


def _split3(x):
    hi = x.astype(BF16)
    r1 = x - hi.astype(F32)
    mid = r1.astype(BF16)
    lo = (r1 - mid.astype(F32)).astype(BF16)
    return hi, mid, lo


def _transpose_cm(xt):
    return jnp.concatenate([xt[:SC_ROWS].T, xt[SC_ROWS:].T], axis=1)


def _dot3_right(x, c01):
    parts = _split3(x)
    return _dot(parts[0], c01) + _dot(parts[1], c01) + _dot(parts[2], c01)


def _mix_chunks(n, robust, in_refs, const_refs, y_ref, st_ref, st_new_ref, ext_ref, b_scr, sd_scr, at_scr):
    u_ref, k_ref, qt_ref, lat_ref, vt_ref, rt_ref = in_refs
    (ubd_ref, uprev_ref, slot_ref, dmask_ref, causal_ref, eye_ref, gn_ref, cnt_ref,
     wpool_ref, pscale_ref) = const_refs
    C = GLA_CHUNK
    lane = lax.broadcasted_iota(jnp.int32, (SC_ROWS, SC_ROWS), 1)
    rowi = lax.broadcasted_iota(jnp.int32, (SC_ROWS, SC_ROWS), 0)
    lane64 = lax.broadcasted_iota(jnp.int32, (C, SC_ROWS), 1)
    lane8 = lax.broadcasted_iota(jnp.int32, (8, SC_ROWS), 1)
    lane_lo = lane < C
    row_lo = rowi < C
    row_head = lax.broadcasted_iota(jnp.int32, (GLA_KEY_WIDTH, SC_ROWS), 0) // GLA_DK

    def heads_on_lanes(xt):
        return jnp.concatenate([jnp.where(row_head == h, xt, 0.0) for h in range(GLA_HEADS)], axis=1).astype(BF16)

    for c in range(n):
        for b in range(BATCH):
            ext_ref[b, SUB + C * c:SUB + C * (c + 1), :] = u_ref[SC_ROWS * c + C * b:SC_ROWS * c + C * (b + 1), :]
    states = [st_ref[h] for h in range(GLA_HEADS)]
    cnt = cnt_ref[...]
    ubd = ubd_ref[...]
    eye = eye_ref[...]
    gn = gn_ref[...]

    def chunk(c):
        rows = slice(SC_ROWS * c, SC_ROWS * (c + 1))
        bT = _dot3_right(lat_ref[:, rows], ubd)

        u = u_ref[rows, :]
        win = []
        for b in range(BATCH):
            level = ext_ref[b, C * c:C * c + SUB + C, :]
            for g in range(POOL_GROUPS):
                level = level[:, POOL_GROUP_DIM if g else 0:]
                level = level + pltpu.roll(level, 1 << g, axis=0)
                win.append(level[SUB:, :POOL_GROUP_DIM])
        parts = []
        for g in range(POOL_GROUPS):
            cols = slice(POOL_GROUP_DIM * g, POOL_GROUP_DIM * (g + 1))
            s = jnp.concatenate([win[b * POOL_GROUPS + g] for b in range(BATCH)], axis=0)
            p = s / cnt[:, cols] - u[:, cols]
            parts.append(_dot(p.astype(BF16), wpool_ref[g]))
        y_pool = jnp.concatenate(parts, axis=1) * pscale_ref[...]
        yield

        b_tm = _transpose_cm(bT)
        b_scr[c] = b_tm
        ends = [[b_scr[c, pl.ds(C * b + SUB * j + SUB - 1, 1), :] for j in range(N_SUB)] for b in range(BATCH)]

        def per_block(fn):
            return jnp.concatenate(
                [jnp.broadcast_to(fn(b, j), (SUB, GLA_KEY_WIDTH)) for b in range(BATCH) for j in range(N_SUB)], axis=0)

        k_tm = k_ref[rows, :]
        qT = qt_ref[:, rows]
        e_last = per_block(lambda b, j: ends[b][N_SUB - 1])
        ktil = k_tm * jnp.exp(e_last - b_tm)
        qeT = qT * jnp.exp(bT)

        if not robust:
            kneg = (k_tm * jnp.exp(-b_tm)).astype(BF16)
            at_all = jnp.where(causal_ref[...] != 0, _dot(kneg, heads_on_lanes(qeT)), 0.0)
        else:
            cprevT = _dot3_right(lat_ref[:, rows], uprev_ref[...])
            e_own = per_block(lambda b, j: ends[b][j])
            f2 = per_block(lambda b, j: jnp.exp(ends[b][min(j + 1, N_SUB - 1)] - ends[b][j]))
            f3 = per_block(lambda b, j: jnp.exp(ends[b][min(j + 2, N_SUB - 1)] - ends[b][j]))
            khat = k_tm * jnp.exp(e_own - b_tm)
            kslots = jnp.concatenate([khat, khat * f2, khat * f3], axis=0).astype(BF16)
            r_all = _dot(kslots, heads_on_lanes(qT * jnp.exp(bT - cprevT)))
            slot = slot_ref[...]
            at_off = jnp.where(slot == 1, r_all[0:SC_ROWS],
                               jnp.where(slot == 2, r_all[SC_ROWS:2 * SC_ROWS],
                                         jnp.where(slot == 3, r_all[2 * SC_ROWS:], 0.0)))
            kT = jnp.concatenate([k_tm[:, :LANES].T, k_tm[:, LANES:].T], axis=0)
            for dist in range(SUB):
                if dist == 0:
                    qs, bs = qT, bT
                else:
                    qs = pltpu.roll(qT, SC_ROWS - dist, axis=1)
                    bs = pltpu.roll(bT, SC_ROWS - dist, axis=1)
                prod = qs * kT * jnp.exp(jnp.minimum(bs - bT, 0.0))
                for h in range(GLA_HEADS):
                    sd_scr[pl.ds(SUB * h + dist, 1), :] = jnp.sum(
                        prod[GLA_DK * h:GLA_DK * (h + 1)], axis=0, keepdims=True)
            sd = jnp.concatenate([sd_scr[...], jnp.zeros((SC_ROWS - GLA_HEADS * SUB, SC_ROWS), F32)], axis=0)
            sdt = sd.T
            dmask = dmask_ref[...]
            for h in range(GLA_HEADS):
                xh = jnp.where((lane >= SUB * h) & (lane < SUB * (h + 1)), sdt, 0.0)
                skew = pltpu.roll(xh, (SC_ROWS - SUB * h) % SC_ROWS, axis=1, stride=1, stride_axis=0)
                at_scr[:, SC_ROWS * h:SC_ROWS * (h + 1)] = (
                    jnp.where(dmask != 0, skew, 0.0) + at_off[:, SC_ROWS * h:SC_ROWS * (h + 1)])
            at_all = at_scr[...]
        yield

        vT = vt_ref[:, rows]
        rT = rt_ref[:, rows]
        results = []
        for h in range(GLA_HEADS):
            at_h = at_all[:, SC_ROWS * h:SC_ROWS * (h + 1)]

            half = (h % 2) * C
            kt_cols = ktil[:, LANES * (h // 2):LANES * (h // 2 + 1)]
            kt_roll = pltpu.roll(kt_cols, C, axis=1)
            lo_src, hi_src = (kt_cols, kt_roll) if half == 0 else (kt_roll, kt_cols)
            kbd = jnp.where(row_lo & lane_lo, lo_src, jnp.where((~row_lo) & (~lane_lo), hi_src, 0.0))
            qe_h = qeT[GLA_DK * h:GLA_DK * (h + 1)]
            qebd = jnp.concatenate([jnp.where(lane64 < C, qe_h, 0.0), jnp.where(lane64 >= C, qe_h, 0.0)], axis=0)

            v_h = vT[GLA_DV * h:GLA_DV * (h + 1)]
            res = _dot(v_h, jnp.concatenate([at_h, kbd], axis=1).astype(BF16))
            results.append((res, _dot(states[h].astype(BF16), qebd.astype(BF16))))
        yield

        y_heads = []
        for h in range(GLA_HEADS):
            half = (h % 2) * C
            res, o_inter = results[h]
            st = states[h]
            o_t = res[:, :SC_ROWS] + o_inter

            e0 = jnp.broadcast_to(ends[0][N_SUB - 1][:, LANES * (h // 2):LANES * (h // 2 + 1)], (8, LANES))
            e1 = jnp.broadcast_to(ends[1][N_SUB - 1][:, LANES * (h // 2):LANES * (h // 2 + 1)], (8, LANES))
            if half == 0:
                e1 = pltpu.roll(e1, C, axis=1)
            else:
                e0 = pltpu.roll(e0, C, axis=1)
            dec = jnp.exp(jnp.where(lane8 < C, e0, e1))
            states[h] = st * jnp.broadcast_to(dec[0:1], (GLA_DV, SC_ROWS)) + res[:, SC_ROWS:]

            ms = jnp.mean(o_t * o_t, axis=0, keepdims=True)
            gsl = slice(GLA_DV * h, GLA_DV * (h + 1))
            y_heads.append(o_t * lax.rsqrt(ms + RMS_EPS) * gn[gsl] * rT[gsl])
        y_t = jnp.concatenate(y_heads, axis=0).astype(BF16)
        y_gla = _dot_nt(eye, y_t)
        yield
        y_ref[rows, :] = jnp.concatenate([y_pool, y_gla], axis=1).astype(BF16)

    def in_turn():
        for c in range(n):
            yield from chunk(c)

    def finish():
        for h in range(GLA_HEADS):
            st_new_ref[h] = states[h]

    return _Stagger([in_turn()] if robust else [chunk(c) for c in range(n)]), finish


def _outproj_ffn_rows(alpha, h, y, store, wo_ref, l2g_ref, l2b_ref, wg_ref, wu_ref, wd_ref, l3g_ref, l3b_ref):
    z = alpha * h + _dot(y, wo_ref[...])
    yield
    h2 = _layer_norm(z, l2g_ref[...], l2b_ref[...])
    yield
    z2 = yield from _swiglu_half_step(h2, wg_ref, wu_ref, wd_ref, alpha)
    store(_layer_norm(z2, l3g_ref[...], l3b_ref[...]))


def _mixer_kernel(tc, flags_ref, *refs):
    in_refs, const_refs, y_ref = refs[:N_MIX_IN], refs[N_MIX_IN:N_MIX_IN + N_MIX_CONST], refs[N_MIX_IN + N_MIX_CONST]
    st_ref, ext_ref, b_scr, sd_scr, at_scr = refs[N_MIX_IN + N_MIX_CONST + 1:]
    step = pl.program_id(0)

    def run(n, robust):
        phases, finish = _mix_chunks(n, robust, in_refs, const_refs, y_ref, st_ref, st_ref, ext_ref,
                                     b_scr, sd_scr, at_scr)
        _run_phases(phases)
        finish()
        for b in range(BATCH):
            ext_ref[b, 0:SUB, :] = ext_ref[b, GLA_CHUNK * n:GLA_CHUNK * n + SUB, :]

    @pl.when(step == 0)
    def _():
        st_ref[...] = jnp.zeros(st_ref.shape, F32)
        ext_ref[:, 0:SUB, :] = jnp.zeros((BATCH, SUB, POOL_WIDTH), F32)
        run(1, True)

    @pl.when(step > 0)
    def _():
        base = (step - 1) * tc
        slow = flags_ref[base]
        for c in range(1, tc):
            slow = jnp.maximum(slow, flags_ref[base + c])

        @pl.when(slow == 0)
        def _():
            run(tc, False)

        @pl.when(slow != 0)
        def _():
            run(tc, True)


def _outproj_ffn_kernel(alpha, to_out, cast_plan, h_ref, y_ref, *refs):
    if cast_plan is not None:
        n_in = len(refs) - 2 * N_FFN_WEIGHTS - 1
        _cast_weight_slabs(cast_plan, refs[n_in:n_in + N_FFN_WEIGHTS], refs[-N_FFN_WEIGHTS:])
        refs = refs[:n_in] + refs[n_in + N_FFN_WEIGHTS:-N_FFN_WEIGHTS]
    *w_refs, o_ref = refs
    tile_rows = h_ref.shape[0]

    def store_rows(r0):
        def store(o):
            o_ref[r0:r0 + o.shape[0], :] = o
        return store

    def store_chunks(r0):
        def store(o):
            for sb in range(o.shape[0] // GLA_CHUNK):
                s, b = divmod(r0 // GLA_CHUNK + sb, BATCH)
                o_ref[b, s] = o[sb * GLA_CHUNK:(sb + 1) * GLA_CHUNK]
        return store

    def parts(bounds, make_store):
        _staggered([_outproj_ffn_rows(alpha, h_ref[r0:r1, :], y_ref[r0:r1, :], make_store(r0), *w_refs)
                    for r0, r1 in zip(bounds[:-1], bounds[1:])])

    bounds = list(range(0, tile_rows, min(PART_ROWS, tile_rows))) + [tile_rows]
    if to_out:
        parts(bounds, store_chunks)
        return
    last = pl.num_programs(0) - 1

    @pl.when(pl.program_id(0) < last)
    def _():
        parts(bounds, store_rows)

    @pl.when(pl.program_id(0) == last)
    def _():
        parts([0, SC_ROWS], store_rows)


def _resident(shape):
    nd = len(shape)
    return pl.BlockSpec(shape, lambda *_: (0,) * nd, pipeline_mode=pl.Buffered(1))


def _layer_resident(shape, layer):
    nd = len(shape) - 1
    return pl.BlockSpec((None,) + tuple(shape[1:]), lambda *_: (layer,) + (0,) * nd, pipeline_mode=pl.Buffered(1))


def _mixer_constants():
    r = np.arange(SC_ROWS)
    b, t = r // GLA_CHUNK, r % GLA_CHUNK
    same_b = b[:, None] == b[None, :]
    ubd = same_b & (t[:, None] <= t[None, :])
    blk = t // SUB
    uprev = same_b & (blk[:, None] < blk[None, :])
    dist = np.where(same_b, blk[None, :] - blk[:, None], 0)
    slot = np.where((dist >= 1) & (dist < N_SUB), dist, 0).astype(np.int32)
    slot = np.tile(slot, (1, GLA_HEADS))
    dmask = (same_b & (blk[:, None] == blk[None, :]) & (t[None, :] >= t[:, None])).astype(np.int32)
    causal = np.tile((same_b & (t[None, :] >= t[:, None])).astype(np.int32), (1, GLA_HEADS))
    eye = np.eye(SC_ROWS)
    w_lane = np.repeat(np.array(POOL_WINDOWS, np.float32), POOL_GROUP_DIM)[None, :]
    t_meta = np.maximum(t - (GLA_CHUNK - N_META), 0).astype(np.float32)[:, None]
    cnt = np.stack([np.broadcast_to(w_lane, (SC_ROWS, POOL_WIDTH)), np.minimum(t_meta + 1.0, w_lane)])
    return (jnp.asarray(ubd, BF16), jnp.asarray(uprev, BF16), jnp.asarray(slot),
            jnp.asarray(dmask), jnp.asarray(causal), jnp.asarray(eye, BF16), jnp.asarray(cnt, F32))


def _tile_chunks(n_ch, most):
    return next(k for k in (8, 4, 2, 1) if k <= most and n_ch % k == 0)


def kernel(x, meta_tokens, ffn1_w_gate, ffn1_w_up, ffn1_w_down, ln1_g, ln1_b, w_in, w_gate_up, b_gate, w_pool, pool_scale, gla_norm_g, w_out, ln2_g, ln2_b, ffn2_w_gate, ffn2_w_up, ffn2_w_down, ln3_g, ln3_b):
    batch, seq, d = x.shape
    depth = w_in.shape[0]
    assert batch == BATCH and d == D_MODEL and seq % GLA_CHUNK == 0
    n_ch = seq // GLA_CHUNK
    n_sc = n_ch + 1
    n_rows = n_sc * SC_ROWS
    tc = _tile_chunks(n_ch, DENSE_TILE_CHUNKS)
    tr = tc * SC_ROWS
    n_main = n_ch // tc
    tcm = _tile_chunks(n_ch, MIXER_TILE_CHUNKS)
    n_mix = n_ch // tcm
    tcp = _tile_chunks(n_ch, POST_TILE_CHUNKS)
    n_post = n_ch // tcp
    alpha = (2.0 * depth) ** 0.25

    x4 = x.reshape(batch, n_ch, GLA_CHUNK, d)
    meta_half = jnp.concatenate([jnp.zeros((GLA_CHUNK - N_META, d), x.dtype), meta_tokens.astype(x.dtype)], axis=0)
    meta_sc = jnp.concatenate([meta_half, meta_half], axis=0)
    h = None

    ubd, uprev, slot, dmask, causal, eye, cnt = _mixer_constants()
    s0 = POOL_WIDTH
    s1 = s0 + GLA_KEY_WIDTH
    s2 = s1 + GLA_KEY_WIDTH
    s3 = s2 + GLA_WIDTH
    s4 = s3 + GLA_WIDTH
    cparams = pltpu.CompilerParams(dimension_semantics=("arbitrary",), vmem_limit_bytes=VMEM_LIMIT_BYTES)
    row_tile = lambda cols: pl.BlockSpec((tr, cols), lambda i: (i, 0))
    col_tile = lambda rows: pl.BlockSpec((rows, tr), lambda i: (0, i))
    post_tile = pl.BlockSpec((tcp * SC_ROWS, d), lambda i: (i, 0))
    x4_tile = pl.BlockSpec((batch, tc, GLA_CHUNK, d), lambda i: (0, jnp.minimum(i, n_main - 1), 0, 0))
    mix_tile = lambda i: lax.rem(i + n_mix, n_mix + 1)
    mix_rows = lambda cols: pl.BlockSpec((tcm * SC_ROWS, cols), lambda i, f: (mix_tile(i), 0))
    mix_cols = lambda rows: pl.BlockSpec((rows, tcm * SC_ROWS), lambda i, f: (0, mix_tile(i)))

    ffn1 = tuple(w[0].astype(BF16) for w in (ffn1_w_gate, ffn1_w_up, ffn1_w_down))
    ffn_specs = [_resident(w.shape) for w in ffn1]
    wo_all, wpool_all = w_out.astype(BF16), w_pool.astype(BF16)
    row3 = lambda p: p.reshape(depth, 1, p.shape[-1])
    l1g, l1b, l2g, l2b, l3g, l3b = (row3(p) for p in (ln1_g, ln1_b, ln2_g, ln2_b, ln3_g, ln3_b))
    pscale_all = row3(pool_scale)

    def cast_job(n_steps, layer, stacked):
        steps = n_steps // N_FFN_WEIGHTS
        assert steps >= 1, "weight-cast side job needs at least three grid steps"
        width, slabs = (WEIGHT_SLAB, D_FF // WEIGHT_SLAB) if steps >= D_FF // WEIGHT_SLAB else (D_FF, 1)
        slab = lambda m: (lambda i: jnp.clip(i - m * steps, 0, slabs - 1))
        in_specs = [pl.BlockSpec((None, d, width), lambda i, j=slab(0): (layer, 0, j(i))),
                    pl.BlockSpec((None, d, width), lambda i, j=slab(1): (layer, 0, j(i))),
                    pl.BlockSpec((None, width, d), lambda i, j=slab(2): (layer, j(i), 0))]
        out_specs = [pl.BlockSpec((d, width), lambda i, j=slab(0): (0, j(i))),
                     pl.BlockSpec((d, width), lambda i, j=slab(1): (0, j(i))),
                     pl.BlockSpec((width, d), lambda i, j=slab(2): (j(i), 0))]
        out_shapes = [jax.ShapeDtypeStruct(w.shape[1:], BF16) for w in stacked]
        return (steps, slabs), in_specs, out_specs, out_shapes

    for l in range(depth):
        wl = w_in[l]
        glr_pad = jnp.zeros((d, LANES - GLA_GATE_RANK), F32)
        wtm = jnp.concatenate([wl[:, :s0], wl[:, s1:s2]], axis=1).astype(BF16)
        wcm = jnp.concatenate([wl[:, s0:s1] * (GLA_DK ** -0.5), wl[:, s2:s3], wl[:, s3:s4]], axis=1).T.astype(BF16)
        wgt = jnp.concatenate([wl[:, s4:], glr_pad], axis=1).T.astype(BF16)
        wgut = jnp.concatenate([w_gate_up[l], jnp.zeros((LANES - GLA_GATE_RANK, GLA_KEY_WIDTH), F32)], axis=0).T.astype(BF16)
        bgc = b_gate[l].reshape(GLA_KEY_WIDTH, 1)

        from_x = l == 0
        acts = (x4, meta_sc) if from_x else (h,)
        act_specs = [x4_tile, _resident(meta_sc.shape)] if from_x else [row_tile(d)]
        ffn2_f32 = (ffn2_w_gate, ffn2_w_up, ffn2_w_down)
        plan, cast_in, cast_out, cast_shapes = cast_job(n_main + 1, l, ffn2_f32)
        outs = pl.pallas_call(
            functools.partial(_ffn_inproj_kernel, alpha, from_x, tr, plan),
            grid=(n_main + 1,),
            in_specs=act_specs + ffn_specs + [_layer_resident(l1g.shape, l), _layer_resident(l1b.shape, l),
                                              _resident(wtm.shape), _resident(wcm.shape), _resident(wgt.shape),
                                              _resident(wgut.shape), _resident(bgc.shape)] + cast_in,
            out_specs=[row_tile(d), row_tile(POOL_WIDTH), row_tile(GLA_KEY_WIDTH),
                       col_tile(GLA_KEY_WIDTH), col_tile(GLA_KEY_WIDTH), col_tile(GLA_WIDTH), col_tile(GLA_WIDTH),
                       pl.BlockSpec((FLAG_ROWS, LANES), lambda i: (i, 0))] + cast_out,
            out_shape=[jax.ShapeDtypeStruct((n_rows, d), F32),
                       jax.ShapeDtypeStruct((n_rows, POOL_WIDTH), F32),
                       jax.ShapeDtypeStruct((n_rows, GLA_KEY_WIDTH), F32),
                       jax.ShapeDtypeStruct((GLA_KEY_WIDTH, n_rows), F32),
                       jax.ShapeDtypeStruct((GLA_KEY_WIDTH, n_rows), F32),
                       jax.ShapeDtypeStruct((GLA_WIDTH, n_rows), BF16),
                       jax.ShapeDtypeStruct((GLA_WIDTH, n_rows), F32),
                       jax.ShapeDtypeStruct(((n_main + 1) * FLAG_ROWS, LANES), jnp.int32)] + cast_shapes,
            compiler_params=cparams,
            name=f"ffn_inproj_{l}",
        )(*acts, *ffn1, l1g, l1b, wtm, wcm, wgt, wgut, bgc, *ffn2_f32)
        h1, u_tm, k_tm, q_t, la_t, v_t, r_t, tile_flags = outs[:N_INPROJ_OUT]
        ffn2 = outs[N_INPROJ_OUT:]
        sc_flags = tile_flags.reshape(n_main + 1, FLAG_ROWS, LANES)[:, :tc, 0].reshape(-1)[:n_sc]

        gn = jnp.broadcast_to(gla_norm_g[l].reshape(GLA_WIDTH, 1), (GLA_WIDTH, SC_ROWS))
        mix_in = (u_tm, k_tm, q_t, la_t, v_t, r_t)
        mix_in_specs = [mix_rows(POOL_WIDTH), mix_rows(GLA_KEY_WIDTH), mix_cols(GLA_KEY_WIDTH),
                        mix_cols(GLA_KEY_WIDTH), mix_cols(GLA_WIDTH), mix_cols(GLA_WIDTH)]
        mix_const = (ubd, uprev, slot, dmask, causal, eye, gn, cnt, wpool_all, pscale_all)
        mix_const_specs = [_resident(ubd.shape), _resident(uprev.shape), _resident(slot.shape),
                           _resident(dmask.shape), _resident(causal.shape), _resident(eye.shape), _resident(gn.shape),
                           pl.BlockSpec((None, SC_ROWS, POOL_WIDTH), lambda i, f: (jnp.where(i == 0, 1, 0), 0, 0)),
                           _layer_resident(wpool_all.shape, l), _layer_resident(pscale_all.shape, l)]
        assert len(mix_in) == N_MIX_IN and len(mix_const) == N_MIX_CONST
        y_cat = pl.pallas_call(
            functools.partial(_mixer_kernel, tcm),
            grid_spec=pltpu.PrefetchScalarGridSpec(
                num_scalar_prefetch=1,
                grid=(n_mix + 1,),
                in_specs=mix_in_specs + mix_const_specs,
                out_specs=mix_rows(d),
                scratch_shapes=[pltpu.VMEM((GLA_HEADS, GLA_DV, SC_ROWS), F32),
                                pltpu.VMEM((BATCH, SUB + tcm * GLA_CHUNK, POOL_WIDTH), F32),
                                pltpu.VMEM((tcm, SC_ROWS, GLA_KEY_WIDTH), F32),
                                pltpu.VMEM((GLA_HEADS * SUB, SC_ROWS), F32),
                                pltpu.VMEM((SC_ROWS, GLA_HEADS * SC_ROWS), F32)]),
            out_shape=jax.ShapeDtypeStruct((n_rows, d), BF16),
            compiler_params=cparams,
            name=f"mixer_{l}",
        )(sc_flags, *mix_in, *mix_const)

        to_out = l == depth - 1
        post_specs = [post_tile, post_tile, _layer_resident(wo_all.shape, l),
                      _layer_resident(l2g.shape, l), _layer_resident(l2b.shape, l)] + ffn_specs + [
                      _layer_resident(l3g.shape, l), _layer_resident(l3b.shape, l)]
        post_args = (h1, y_cat, wo_all, l2g, l2b, *ffn2, l3g, l3b)
        if to_out:
            h = pl.pallas_call(
                functools.partial(_outproj_ffn_kernel, alpha, True, None),
                grid=(n_post,),
                in_specs=post_specs,
                out_specs=pl.BlockSpec((batch, tcp, GLA_CHUNK, d), lambda i: (0, i, 0, 0)),
                out_shape=jax.ShapeDtypeStruct(x4.shape, F32),
                compiler_params=cparams,
                name=f"outproj_ffn_{l}",
            )(*post_args)
        else:
            ffn1_f32 = (ffn1_w_gate, ffn1_w_up, ffn1_w_down)
            plan, cast_in, cast_out, cast_shapes = cast_job(n_post + 1, l + 1, ffn1_f32)
            h, *ffn1 = pl.pallas_call(
                functools.partial(_outproj_ffn_kernel, alpha, False, plan),
                grid=(n_post + 1,),
                in_specs=post_specs + cast_in,
                out_specs=[post_tile] + cast_out,
                out_shape=[jax.ShapeDtypeStruct((n_rows, d), F32)] + cast_shapes,
                compiler_params=cparams,
                name=f"outproj_ffn_{l}",
            )(*post_args, *ffn1_f32)

    return h.reshape(batch, seq, d)
```

```python
import functools

import numpy as np
import jax
import jax.numpy as jnp
from jax import lax
from jax.experimental import pallas as pl
from jax.experimental.pallas import tpu as pltpu

D_MODEL = 1024
N_META = 16
POOL_WIDTH = 512
POOL_GROUPS = 4
POOL_GROUP_DIM = 128
POOL_WINDOWS = (2, 4, 8, 16)
GLA_WIDTH = 512
GLA_HEADS = 4
GLA_KEY_WIDTH = 256
GLA_DK = 64
GLA_DV = 128
GLA_GATE_RANK = 16
GLA_GATE_TEMP = 16.0
GLA_CHUNK = 64
D_FF = 2816
LN_EPS = 1e-5
RMS_EPS = 1e-6

BATCH = 2
SC_ROWS = BATCH * GLA_CHUNK
SUB = 16
N_SUB = GLA_CHUNK // SUB
LANES = 128
SUBLANES = 8
PART_ROWS = 256
DENSE_TILE_CHUNKS = 4
MIXER_TILE_CHUNKS = 8
STAGGER_LAG = 1
VMEM_LIMIT_BYTES = 60 * 1024 * 1024
FAST_PATH_MAX_DECAY = 40.0
FLAG_ROWS = SUBLANES
N_FFN_WEIGHTS = 3
N_INPROJ_OUT = 8
WEIGHT_SLAB = 256
N_MIX_IN = 6
N_MIX_CONST = 10

F32 = jnp.float32
BF16 = jnp.bfloat16

assert POOL_WINDOWS == tuple(2 << g for g in range(POOL_GROUPS)) and POOL_WINDOWS[-1] == SUB


def _dot(a, b):
    return jnp.dot(a, b, preferred_element_type=F32)


def _dot_nt(a, b):
    return lax.dot_general(a, b, (((1,), (1,)), ((), ())), preferred_element_type=F32)


def _layer_norm(z, g, b):
    mu = jnp.mean(z, axis=-1, keepdims=True)
    zc = z - mu
    var = jnp.mean(zc * zc, axis=-1, keepdims=True)
    return zc * lax.rsqrt(var + LN_EPS) * g + b


def _silu(x):
    return x * jax.nn.sigmoid(x)


def _log_sigmoid(x):
    return jnp.minimum(x, 0.0) - jnp.log1p(jnp.exp(-jnp.abs(x)))


def _swiglu_half_step(x, wg_ref, wu_ref, wd_ref, alpha):
    xb = x.astype(BF16)
    g = _dot(xb, wg_ref[...])
    u = _dot(xb, wu_ref[...])
    yield
    act = (_silu(g) * u).astype(BF16)
    yield
    y = _dot(act, wd_ref[...])
    yield
    return alpha * x + 0.5 * y


def _staggered(gens):
    gens = list(gens)
    results = [None] * len(gens)
    done = [False] * len(gens)
    t = 0
    while not all(done):
        for k, gen in enumerate(gens):
            if done[k] or t < k * STAGGER_LAG:
                continue
            try:
                next(gen)
            except StopIteration as stop:
                results[k], done[k] = stop.value, True
        t += 1
    return results


def _ffn_inproj_rows(alpha, x, r0, is_meta, wg_ref, wu_ref, wd_ref, lng_ref, lnb_ref,
                     wtm_ref, wcm_ref, wgt_ref, wgut_ref, bgc_ref,
                     h_ref, u_ref, k_ref, qt_ref, lat_ref, vt_ref, rt_ref, flag_ref):
    n = x.shape[0]
    rows = slice(r0, r0 + n)
    z = yield from _swiglu_half_step(x, wg_ref, wu_ref, wd_ref, alpha)
    h = _layer_norm(z, lng_ref[...], lnb_ref[...])
    h_ref[rows, :] = h
    hb = h.astype(BF16)
    yield

    if is_meta:
        pad = GLA_CHUNK - N_META
        keep_r = (lax.broadcasted_iota(jnp.int32, (n, 1), 0) % GLA_CHUNK) >= pad
        keep_l = (lax.broadcasted_iota(jnp.int32, (1, n), 1) % GLA_CHUNK) >= pad
        mask_r = lambda v: jnp.where(keep_r, v, 0.0)
        mask_l = lambda v: jnp.where(keep_l, v, 0.0)
    else:
        mask_r = mask_l = lambda v: v

    qv = GLA_KEY_WIDTH + GLA_WIDTH
    gt = _dot_nt(wgt_ref[...], hb).astype(BF16)
    zt = _dot_nt(wcm_ref[:qv, :], hb)
    qt_ref[:, rows] = mask_l(zt[:GLA_KEY_WIDTH])
    vt_ref[:, rows] = mask_l(zt[GLA_KEY_WIDTH:]).astype(BF16)
    lat = mask_l(_log_sigmoid(_dot(wgut_ref[...], gt) + bgc_ref[...]) * (1.0 / GLA_GATE_TEMP))
    lat_ref[:, rows] = lat
    ztm = _dot(hb, wtm_ref[...])
    u_ref[rows, :] = mask_r(ztm[:, :POOL_WIDTH])
    k_ref[rows, :] = mask_r(ztm[:, POOL_WIDTH:])
    flags = []
    lane_lo = lax.broadcasted_iota(jnp.int32, (GLA_KEY_WIDTH, SC_ROWS), 1) < GLA_CHUNK
    for s in range(n // SC_ROWS):
        seg = lat[:, SC_ROWS * s:SC_ROWS * (s + 1)]
        tot = jnp.minimum(jnp.sum(jnp.where(lane_lo, seg, 0.0), axis=1, keepdims=True),
                          jnp.sum(jnp.where(lane_lo, 0.0, seg), axis=1, keepdims=True))
        flags.append((jnp.min(tot, axis=0, keepdims=True) < -FAST_PATH_MAX_DECAY).astype(jnp.int32))
    yield

    rt_ref[:, rows] = _silu(_dot_nt(wcm_ref[qv:, :], hb))
    return flags


def _cast_weight_slabs(plan, in_refs, out_refs):
    steps, slabs = plan
    step = pl.program_id(0)
    for m, (src, dst) in enumerate(zip(in_refs, out_refs)):
        @pl.when((step >= m * steps) & (step < m * steps + slabs))
        def _():
            dst[...] = src[...].astype(BF16)


def _ffn_inproj_kernel(alpha, from_x, tile_rows, cast_plan, x_ref, *refs):
    if from_x:
        meta_ref, *refs = refs
    if cast_plan is not None:
        n_par = len(refs) - 2 * N_FFN_WEIGHTS - N_INPROJ_OUT
        _cast_weight_slabs(cast_plan, refs[n_par:n_par + N_FFN_WEIGHTS], refs[-N_FFN_WEIGHTS:])
        refs = refs[:n_par] + refs[n_par + N_FFN_WEIGHTS:-N_FFN_WEIGHTS]
    flag_ref = refs[-1]
    last = pl.num_programs(0) - 1

    def x_rows(r0, n):
        if not from_x:
            return x_ref[r0:r0 + n, :]
        pieces = [divmod(r0 // GLA_CHUNK + i, BATCH) for i in range(n // GLA_CHUNK)]
        return jnp.concatenate([x_ref[b, s] for s, b in pieces], axis=0)

    def run(parts, is_meta):
        gens = [_ffn_inproj_rows(alpha, x, r0, is_meta, *refs) for x, r0 in parts]
        sc_flags = [f for fl in _staggered(gens) for f in fl]
        frow = lax.broadcasted_iota(jnp.int32, (FLAG_ROWS, LANES), 0)
        flags = jnp.zeros((FLAG_ROWS, LANES), jnp.int32)
        for s, f in enumerate(sc_flags):
            flags = jnp.where(frow == s, f, flags)
        flag_ref[...] = flags

    @pl.when(pl.program_id(0) < last)
    def _():
        pr = min(PART_ROWS, tile_rows)
        run([(x_rows(r0, pr), r0) for r0 in range(0, tile_rows, pr)], False)

    @pl.when(pl.program_id(0) == last)
    def _():
        run([(meta_ref[...] if from_x else x_ref[0:SC_ROWS, :], 0)], True)


def _split3(x):
    hi = x.astype(BF16)
    r1 = x - hi.astype(F32)
    mid = r1.astype(BF16)
    lo = (r1 - mid.astype(F32)).astype(BF16)
    return hi, mid, lo


def _transpose_cm(xt):
    return jnp.concatenate([xt[:SC_ROWS].T, xt[SC_ROWS:].T], axis=1)


def _dot3_right(x, c01):
    parts = _split3(x)
    return _dot(parts[0], c01) + _dot(parts[1], c01) + _dot(parts[2], c01)


def _mix_chunks(n, robust, in_refs, const_refs, y_ref, st_ref, ext_ref, b_scr, sd_scr, at_scr):
    u_ref, k_ref, qt_ref, lat_ref, vt_ref, rt_ref = in_refs
    (ubd_ref, uprev_ref, slot_ref, dmask_ref, causal_ref, eye_ref, gn_ref, cnt_ref,
     wpool_ref, pscale_ref) = const_refs
    C = GLA_CHUNK
    lane = lax.broadcasted_iota(jnp.int32, (SC_ROWS, SC_ROWS), 1)
    rowi = lax.broadcasted_iota(jnp.int32, (SC_ROWS, SC_ROWS), 0)
    lane64 = lax.broadcasted_iota(jnp.int32, (C, SC_ROWS), 1)
    lane8 = lax.broadcasted_iota(jnp.int32, (SUBLANES, SC_ROWS), 1)
    lane_lo = lane < C
    row_lo = rowi < C
    row_head = lax.broadcasted_iota(jnp.int32, (GLA_KEY_WIDTH, SC_ROWS), 0) // GLA_DK

    def heads_on_lanes(xt):
        return jnp.concatenate([jnp.where(row_head == h, xt, 0.0) for h in range(GLA_HEADS)], axis=1).astype(BF16)

    for c in range(n):
        for b in range(BATCH):
            ext_ref[b, SUB + C * c:SUB + C * (c + 1), :] = u_ref[SC_ROWS * c + C * b:SC_ROWS * c + C * (b + 1), :]
    states = [st_ref[h] for h in range(GLA_HEADS)]
    cnt = cnt_ref[...]
    ubd = ubd_ref[...]
    eye = eye_ref[...]
    gn = gn_ref[...]

    def chunk(c):
        rows = slice(SC_ROWS * c, SC_ROWS * (c + 1))
        bT = _dot3_right(lat_ref[:, rows], ubd)

        u = u_ref[rows, :]
        win = []
        for b in range(BATCH):
            level = ext_ref[b, C * c:C * c + SUB + C, :]
            for g in range(POOL_GROUPS):
                level = level[:, POOL_GROUP_DIM if g else 0:]
                level = level + pltpu.roll(level, 1 << g, axis=0)
                win.append(level[SUB:, :POOL_GROUP_DIM])
        parts = []
        for g in range(POOL_GROUPS):
            cols = slice(POOL_GROUP_DIM * g, POOL_GROUP_DIM * (g + 1))
            s = jnp.concatenate([win[b * POOL_GROUPS + g] for b in range(BATCH)], axis=0)
            p = s / cnt[:, cols] - u[:, cols]
            parts.append(_dot(p.astype(BF16), wpool_ref[g]))
        y_pool = jnp.concatenate(parts, axis=1) * pscale_ref[...]
        yield

        b_tm = _transpose_cm(bT)
        b_scr[c] = b_tm
        ends = [[b_scr[c, pl.ds(C * b + SUB * j + SUB - 1, 1), :] for j in range(N_SUB)] for b in range(BATCH)]

        def per_block(fn):
            return jnp.concatenate(
                [jnp.broadcast_to(fn(b, j), (SUB, GLA_KEY_WIDTH)) for b in range(BATCH) for j in range(N_SUB)], axis=0)

        k_tm = k_ref[rows, :]
        qT = qt_ref[:, rows]
        e_last = per_block(lambda b, j: ends[b][N_SUB - 1])
        ktil = k_tm * jnp.exp(e_last - b_tm)
        qeT = qT * jnp.exp(bT)

        if not robust:
            kneg = (k_tm * jnp.exp(-b_tm)).astype(BF16)
            at_all = jnp.where(causal_ref[...] != 0, _dot(kneg, heads_on_lanes(qeT)), 0.0)
        else:
            cprevT = _dot3_right(lat_ref[:, rows], uprev_ref[...])
            e_own = per_block(lambda b, j: ends[b][j])
            f2 = per_block(lambda b, j: jnp.exp(ends[b][min(j + 1, N_SUB - 1)] - ends[b][j]))
            f3 = per_block(lambda b, j: jnp.exp(ends[b][min(j + 2, N_SUB - 1)] - ends[b][j]))
            khat = k_tm * jnp.exp(e_own - b_tm)
            kslots = jnp.concatenate([khat, khat * f2, khat * f3], axis=0).astype(BF16)
            r_all = _dot(kslots, heads_on_lanes(qT * jnp.exp(bT - cprevT)))
            slot = slot_ref[...]
            at_off = jnp.where(slot == 1, r_all[0:SC_ROWS],
                               jnp.where(slot == 2, r_all[SC_ROWS:2 * SC_ROWS],
                                         jnp.where(slot == 3, r_all[2 * SC_ROWS:], 0.0)))
            kT = jnp.concatenate([k_tm[:, :LANES].T, k_tm[:, LANES:].T], axis=0)
            for dist in range(SUB):
                if dist == 0:
                    qs, bs = qT, bT
                else:
                    qs = pltpu.roll(qT, SC_ROWS - dist, axis=1)
                    bs = pltpu.roll(bT, SC_ROWS - dist, axis=1)
                prod = qs * kT * jnp.exp(jnp.minimum(bs - bT, 0.0))
                for h in range(GLA_HEADS):
                    sd_scr[pl.ds(SUB * h + dist, 1), :] = jnp.sum(
                        prod[GLA_DK * h:GLA_DK * (h + 1)], axis=0, keepdims=True)
            sd = jnp.concatenate([sd_scr[...], jnp.zeros((SC_ROWS - GLA_HEADS * SUB, SC_ROWS), F32)], axis=0)
            sdt = sd.T
            dmask = dmask_ref[...]
            for h in range(GLA_HEADS):
                xh = jnp.where((lane >= SUB * h) & (lane < SUB * (h + 1)), sdt, 0.0)
                skew = pltpu.roll(xh, (SC_ROWS - SUB * h) % SC_ROWS, axis=1, stride=1, stride_axis=0)
                at_scr[:, SC_ROWS * h:SC_ROWS * (h + 1)] = (
                    jnp.where(dmask != 0, skew, 0.0) + at_off[:, SC_ROWS * h:SC_ROWS * (h + 1)])
            at_all = at_scr[...]
        yield

        vT = vt_ref[:, rows]
        rT = rt_ref[:, rows]
        results = []
        for h in range(GLA_HEADS):
            at_h = at_all[:, SC_ROWS * h:SC_ROWS * (h + 1)]

            half = (h % 2) * C
            kt_cols = ktil[:, LANES * (h // 2):LANES * (h // 2 + 1)]
            kt_roll = pltpu.roll(kt_cols, C, axis=1)
            lo_src, hi_src = (kt_cols, kt_roll) if half == 0 else (kt_roll, kt_cols)
            kbd = jnp.where(row_lo & lane_lo, lo_src, jnp.where((~row_lo) & (~lane_lo), hi_src, 0.0))
            qe_h = qeT[GLA_DK * h:GLA_DK * (h + 1)]
            qebd = jnp.concatenate([jnp.where(lane64 < C, qe_h, 0.0), jnp.where(lane64 >= C, qe_h, 0.0)], axis=0)

            v_h = vT[GLA_DV * h:GLA_DV * (h + 1)]
            res = _dot(v_h, jnp.concatenate([at_h, kbd], axis=1).astype(BF16))
            results.append((res, _dot(states[h].astype(BF16), qebd.astype(BF16))))
        yield

        y_heads = []
        for h in range(GLA_HEADS):
            half = (h % 2) * C
            res, o_inter = results[h]
            st = states[h]
            o_t = res[:, :SC_ROWS] + o_inter

            e0 = jnp.broadcast_to(ends[0][N_SUB - 1][:, LANES * (h // 2):LANES * (h // 2 + 1)], (SUBLANES, LANES))
            e1 = jnp.broadcast_to(ends[1][N_SUB - 1][:, LANES * (h // 2):LANES * (h // 2 + 1)], (SUBLANES, LANES))
            if half == 0:
                e1 = pltpu.roll(e1, C, axis=1)
            else:
                e0 = pltpu.roll(e0, C, axis=1)
            dec = jnp.exp(jnp.where(lane8 < C, e0, e1))
            states[h] = st * jnp.broadcast_to(dec[0:1], (GLA_DV, SC_ROWS)) + res[:, SC_ROWS:]

            ms = jnp.mean(o_t * o_t, axis=0, keepdims=True)
            gsl = slice(GLA_DV * h, GLA_DV * (h + 1))
            y_heads.append(o_t * lax.rsqrt(ms + RMS_EPS) * gn[gsl] * rT[gsl])
        y_t = jnp.concatenate(y_heads, axis=0).astype(BF16)
        y_gla = _dot_nt(eye, y_t)
        yield
        y_ref[rows, :] = jnp.concatenate([y_pool, y_gla], axis=1).astype(BF16)

    def in_turn():
        for c in range(n):
            yield from chunk(c)

    _staggered([in_turn()] if robust else [chunk(c) for c in range(n)])
    for h in range(GLA_HEADS):
        st_ref[h] = states[h]
    for b in range(BATCH):
        ext_ref[b, 0:SUB, :] = ext_ref[b, C * n:C * n + SUB, :]


def _outproj_ffn_rows(alpha, h, y, store, wo_ref, l2g_ref, l2b_ref, wg_ref, wu_ref, wd_ref, l3g_ref, l3b_ref):
    z = alpha * h + _dot(y, wo_ref[...])
    yield
    h2 = _layer_norm(z, l2g_ref[...], l2b_ref[...])
    yield
    z2 = yield from _swiglu_half_step(h2, wg_ref, wu_ref, wd_ref, alpha)
    store(_layer_norm(z2, l3g_ref[...], l3b_ref[...]))


def _mixer_kernel(tc, flags_ref, *refs):
    in_refs, const_refs, y_ref = refs[:N_MIX_IN], refs[N_MIX_IN:N_MIX_IN + N_MIX_CONST], refs[N_MIX_IN + N_MIX_CONST]
    st_ref, ext_ref, b_scr, sd_scr, at_scr = refs[N_MIX_IN + N_MIX_CONST + 1:]
    step = pl.program_id(0)

    def run(n, robust):
        _mix_chunks(n, robust, in_refs, const_refs, y_ref, st_ref, ext_ref, b_scr, sd_scr, at_scr)

    @pl.when(step == 0)
    def _():
        st_ref[...] = jnp.zeros(st_ref.shape, F32)
        ext_ref[:, 0:SUB, :] = jnp.zeros((BATCH, SUB, POOL_WIDTH), F32)
        run(1, True)

    @pl.when(step > 0)
    def _():
        base = (step - 1) * tc
        slow = flags_ref[base]
        for c in range(1, tc):
            slow = jnp.maximum(slow, flags_ref[base + c])

        @pl.when(slow == 0)
        def _():
            run(tc, False)

        @pl.when(slow != 0)
        def _():
            run(tc, True)


def _outproj_ffn_kernel(alpha, to_out, cast_plan, h_ref, y_ref, *refs):
    if cast_plan is not None:
        n_in = len(refs) - 2 * N_FFN_WEIGHTS - 1
        _cast_weight_slabs(cast_plan, refs[n_in:n_in + N_FFN_WEIGHTS], refs[-N_FFN_WEIGHTS:])
        refs = refs[:n_in] + refs[n_in + N_FFN_WEIGHTS:-N_FFN_WEIGHTS]
    *w_refs, o_ref = refs
    tile_rows = h_ref.shape[0]

    def store_rows(r0):
        def store(o):
            o_ref[r0:r0 + o.shape[0], :] = o
        return store

    def store_chunks(r0):
        def store(o):
            for sb in range(o.shape[0] // GLA_CHUNK):
                s, b = divmod(r0 // GLA_CHUNK + sb, BATCH)
                o_ref[b, s] = o[sb * GLA_CHUNK:(sb + 1) * GLA_CHUNK]
        return store

    def parts(bounds, make_store):
        _staggered([_outproj_ffn_rows(alpha, h_ref[r0:r1, :], y_ref[r0:r1, :], make_store(r0), *w_refs)
                    for r0, r1 in zip(bounds[:-1], bounds[1:])])

    bounds = list(range(0, tile_rows, min(PART_ROWS, tile_rows))) + [tile_rows]
    if to_out:
        parts(bounds, store_chunks)
        return
    last = pl.num_programs(0) - 1

    @pl.when(pl.program_id(0) < last)
    def _():
        parts(bounds, store_rows)

    @pl.when(pl.program_id(0) == last)
    def _():
        parts([0, SC_ROWS], store_rows)


def _resident(shape):
    nd = len(shape)
    return pl.BlockSpec(shape, lambda *_: (0,) * nd, pipeline_mode=pl.Buffered(1))


def _layer_resident(shape, layer):
    nd = len(shape) - 1
    return pl.BlockSpec((None,) + tuple(shape[1:]), lambda *_: (layer,) + (0,) * nd, pipeline_mode=pl.Buffered(1))


def _mixer_constants():
    r = np.arange(SC_ROWS)
    b, t = r // GLA_CHUNK, r % GLA_CHUNK
    same_b = b[:, None] == b[None, :]
    ubd = same_b & (t[:, None] <= t[None, :])
    blk = t // SUB
    uprev = same_b & (blk[:, None] < blk[None, :])
    dist = np.where(same_b, blk[None, :] - blk[:, None], 0)
    slot = np.where((dist >= 1) & (dist < N_SUB), dist, 0).astype(np.int32)
    slot = np.tile(slot, (1, GLA_HEADS))
    dmask = (same_b & (blk[:, None] == blk[None, :]) & (t[None, :] >= t[:, None])).astype(np.int32)
    causal = np.tile((same_b & (t[None, :] >= t[:, None])).astype(np.int32), (1, GLA_HEADS))
    eye = np.eye(SC_ROWS)
    w_lane = np.repeat(np.array(POOL_WINDOWS, np.float32), POOL_GROUP_DIM)[None, :]
    t_meta = np.maximum(t - (GLA_CHUNK - N_META), 0).astype(np.float32)[:, None]
    cnt = np.stack([np.broadcast_to(w_lane, (SC_ROWS, POOL_WIDTH)), np.minimum(t_meta + 1.0, w_lane)])
    return (jnp.asarray(ubd, BF16), jnp.asarray(uprev, BF16), jnp.asarray(slot),
            jnp.asarray(dmask), jnp.asarray(causal), jnp.asarray(eye, BF16), jnp.asarray(cnt, F32))


def _tile_chunks(n_ch, most):
    return next(k for k in (8, 4, 2, 1) if k <= most and n_ch % k == 0)


def kernel(x, meta_tokens, ffn1_w_gate, ffn1_w_up, ffn1_w_down, ln1_g, ln1_b, w_in, w_gate_up, b_gate, w_pool, pool_scale, gla_norm_g, w_out, ln2_g, ln2_b, ffn2_w_gate, ffn2_w_up, ffn2_w_down, ln3_g, ln3_b):
    batch, seq, d = x.shape
    depth = w_in.shape[0]
    assert batch == BATCH and d == D_MODEL and seq % GLA_CHUNK == 0
    n_ch = seq // GLA_CHUNK
    n_sc = n_ch + 1
    n_rows = n_sc * SC_ROWS
    tc = _tile_chunks(n_ch, DENSE_TILE_CHUNKS)
    tr = tc * SC_ROWS
    n_main = n_ch // tc
    tcm = _tile_chunks(n_ch, MIXER_TILE_CHUNKS)
    n_mix = n_ch // tcm
    alpha = (2.0 * depth) ** 0.25

    x4 = x.reshape(batch, n_ch, GLA_CHUNK, d)
    meta_half = jnp.concatenate([jnp.zeros((GLA_CHUNK - N_META, d), x.dtype), meta_tokens.astype(x.dtype)], axis=0)
    meta_sc = jnp.concatenate([meta_half, meta_half], axis=0)
    h = None

    ubd, uprev, slot, dmask, causal, eye, cnt = _mixer_constants()
    s0 = POOL_WIDTH
    s1 = s0 + GLA_KEY_WIDTH
    s2 = s1 + GLA_KEY_WIDTH
    s3 = s2 + GLA_WIDTH
    s4 = s3 + GLA_WIDTH
    cparams = pltpu.CompilerParams(dimension_semantics=("arbitrary",), vmem_limit_bytes=VMEM_LIMIT_BYTES)
    row_tile = lambda cols: pl.BlockSpec((tr, cols), lambda i: (i, 0))
    col_tile = lambda rows: pl.BlockSpec((rows, tr), lambda i: (0, i))
    x4_tile = pl.BlockSpec((batch, tc, GLA_CHUNK, d), lambda i: (0, jnp.minimum(i, n_main - 1), 0, 0))
    mix_tile = lambda i: lax.rem(i + n_mix, n_mix + 1)
    mix_rows = lambda cols: pl.BlockSpec((tcm * SC_ROWS, cols), lambda i, f: (mix_tile(i), 0))
    mix_cols = lambda rows: pl.BlockSpec((rows, tcm * SC_ROWS), lambda i, f: (0, mix_tile(i)))

    ffn1 = tuple(w[0].astype(BF16) for w in (ffn1_w_gate, ffn1_w_up, ffn1_w_down))
    ffn_specs = [_resident(w.shape) for w in ffn1]
    wo_all, wpool_all = w_out.astype(BF16), w_pool.astype(BF16)
    row3 = lambda p: p.reshape(depth, 1, p.shape[-1])
    l1g, l1b, l2g, l2b, l3g, l3b = (row3(p) for p in (ln1_g, ln1_b, ln2_g, ln2_b, ln3_g, ln3_b))
    pscale_all = row3(pool_scale)

    def cast_job(n_steps, layer, stacked):
        steps = n_steps // N_FFN_WEIGHTS
        assert steps >= 1, "weight-cast side job needs at least three grid steps"
        width, slabs = (WEIGHT_SLAB, D_FF // WEIGHT_SLAB) if steps >= D_FF // WEIGHT_SLAB else (D_FF, 1)
        slab = lambda m: (lambda i: jnp.clip(i - m * steps, 0, slabs - 1))
        in_specs = [pl.BlockSpec((None, d, width), lambda i, j=slab(0): (layer, 0, j(i))),
                    pl.BlockSpec((None, d, width), lambda i, j=slab(1): (layer, 0, j(i))),
                    pl.BlockSpec((None, width, d), lambda i, j=slab(2): (layer, j(i), 0))]
        out_specs = [pl.BlockSpec((d, width), lambda i, j=slab(0): (0, j(i))),
                     pl.BlockSpec((d, width), lambda i, j=slab(1): (0, j(i))),
                     pl.BlockSpec((width, d), lambda i, j=slab(2): (j(i), 0))]
        out_shapes = [jax.ShapeDtypeStruct(w.shape[1:], BF16) for w in stacked]
        return (steps, slabs), in_specs, out_specs, out_shapes

    for l in range(depth):
        wl = w_in[l]
        glr_pad = jnp.zeros((d, LANES - GLA_GATE_RANK), F32)
        wtm = jnp.concatenate([wl[:, :s0], wl[:, s1:s2]], axis=1).astype(BF16)
        wcm = jnp.concatenate([wl[:, s0:s1] * (GLA_DK ** -0.5), wl[:, s2:s3], wl[:, s3:s4]], axis=1).T.astype(BF16)
        wgt = jnp.concatenate([wl[:, s4:], glr_pad], axis=1).T.astype(BF16)
        wgut = jnp.concatenate([w_gate_up[l], jnp.zeros((LANES - GLA_GATE_RANK, GLA_KEY_WIDTH), F32)], axis=0).T.astype(BF16)
        bgc = b_gate[l].reshape(GLA_KEY_WIDTH, 1)

        from_x = l == 0
        acts = (x4, meta_sc) if from_x else (h,)
        act_specs = [x4_tile, _resident(meta_sc.shape)] if from_x else [row_tile(d)]
        ffn2_f32 = (ffn2_w_gate, ffn2_w_up, ffn2_w_down)
        plan, cast_in, cast_out, cast_shapes = cast_job(n_main + 1, l, ffn2_f32)
        outs = pl.pallas_call(
            functools.partial(_ffn_inproj_kernel, alpha, from_x, tr, plan),
            grid=(n_main + 1,),
            in_specs=act_specs + ffn_specs + [_layer_resident(l1g.shape, l), _layer_resident(l1b.shape, l),
                                              _resident(wtm.shape), _resident(wcm.shape), _resident(wgt.shape),
                                              _resident(wgut.shape), _resident(bgc.shape)] + cast_in,
            out_specs=[row_tile(d), row_tile(POOL_WIDTH), row_tile(GLA_KEY_WIDTH),
                       col_tile(GLA_KEY_WIDTH), col_tile(GLA_KEY_WIDTH), col_tile(GLA_WIDTH), col_tile(GLA_WIDTH),
                       pl.BlockSpec((FLAG_ROWS, LANES), lambda i: (i, 0))] + cast_out,
            out_shape=[jax.ShapeDtypeStruct((n_rows, d), F32),
                       jax.ShapeDtypeStruct((n_rows, POOL_WIDTH), F32),
                       jax.ShapeDtypeStruct((n_rows, GLA_KEY_WIDTH), F32),
                       jax.ShapeDtypeStruct((GLA_KEY_WIDTH, n_rows), F32),
                       jax.ShapeDtypeStruct((GLA_KEY_WIDTH, n_rows), F32),
                       jax.ShapeDtypeStruct((GLA_WIDTH, n_rows), BF16),
                       jax.ShapeDtypeStruct((GLA_WIDTH, n_rows), F32),
                       jax.ShapeDtypeStruct(((n_main + 1) * FLAG_ROWS, LANES), jnp.int32)] + cast_shapes,
            compiler_params=cparams,
            name=f"ffn_inproj_{l}",
        )(*acts, *ffn1, l1g, l1b, wtm, wcm, wgt, wgut, bgc, *ffn2_f32)
        h1, u_tm, k_tm, q_t, la_t, v_t, r_t, tile_flags = outs[:N_INPROJ_OUT]
        ffn2 = outs[N_INPROJ_OUT:]
        sc_flags = tile_flags.reshape(n_main + 1, FLAG_ROWS, LANES)[:, :tc, 0].reshape(-1)[:n_sc]

        gn = jnp.broadcast_to(gla_norm_g[l].reshape(GLA_WIDTH, 1), (GLA_WIDTH, SC_ROWS))
        mix_in = (u_tm, k_tm, q_t, la_t, v_t, r_t)
        mix_in_specs = [mix_rows(POOL_WIDTH), mix_rows(GLA_KEY_WIDTH), mix_cols(GLA_KEY_WIDTH),
                        mix_cols(GLA_KEY_WIDTH), mix_cols(GLA_WIDTH), mix_cols(GLA_WIDTH)]
        mix_const = (ubd, uprev, slot, dmask, causal, eye, gn, cnt, wpool_all, pscale_all)
        mix_const_specs = [_resident(ubd.shape), _resident(uprev.shape), _resident(slot.shape),
                           _resident(dmask.shape), _resident(causal.shape), _resident(eye.shape), _resident(gn.shape),
                           pl.BlockSpec((None, SC_ROWS, POOL_WIDTH), lambda i, f: (jnp.where(i == 0, 1, 0), 0, 0)),
                           _layer_resident(wpool_all.shape, l), _layer_resident(pscale_all.shape, l)]
        assert len(mix_in) == N_MIX_IN and len(mix_const) == N_MIX_CONST
        y_cat = pl.pallas_call(
            functools.partial(_mixer_kernel, tcm),
            grid_spec=pltpu.PrefetchScalarGridSpec(
                num_scalar_prefetch=1,
                grid=(n_mix + 1,),
                in_specs=mix_in_specs + mix_const_specs,
                out_specs=mix_rows(d),
                scratch_shapes=[pltpu.VMEM((GLA_HEADS, GLA_DV, SC_ROWS), F32),
                                pltpu.VMEM((BATCH, SUB + tcm * GLA_CHUNK, POOL_WIDTH), F32),
                                pltpu.VMEM((tcm, SC_ROWS, GLA_KEY_WIDTH), F32),
                                pltpu.VMEM((GLA_HEADS * SUB, SC_ROWS), F32),
                                pltpu.VMEM((SC_ROWS, GLA_HEADS * SC_ROWS), F32)]),
            out_shape=jax.ShapeDtypeStruct((n_rows, d), BF16),
            compiler_params=cparams,
            name=f"mixer_{l}",
        )(sc_flags, *mix_in, *mix_const)

        to_out = l == depth - 1
        post_specs = [row_tile(d), row_tile(d), _layer_resident(wo_all.shape, l),
                      _layer_resident(l2g.shape, l), _layer_resident(l2b.shape, l)] + ffn_specs + [
                      _layer_resident(l3g.shape, l), _layer_resident(l3b.shape, l)]
        post_args = (h1, y_cat, wo_all, l2g, l2b, *ffn2, l3g, l3b)
        if to_out:
            h = pl.pallas_call(
                functools.partial(_outproj_ffn_kernel, alpha, True, None),
                grid=(n_main,),
                in_specs=post_specs,
                out_specs=pl.BlockSpec((batch, tc, GLA_CHUNK, d), lambda i: (0, i, 0, 0)),
                out_shape=jax.ShapeDtypeStruct(x4.shape, F32),
                compiler_params=cparams,
                name=f"outproj_ffn_{l}",
            )(*post_args)
        else:
            ffn1_f32 = (ffn1_w_gate, ffn1_w_up, ffn1_w_down)
            plan, cast_in, cast_out, cast_shapes = cast_job(n_main + 1, l + 1, ffn1_f32)
            h, *ffn1 = pl.pallas_call(
                functools.partial(_outproj_ffn_kernel, alpha, False, plan),
                grid=(n_main + 1,),
                in_specs=post_specs + cast_in,
                out_specs=[row_tile(d)] + cast_out,
                out_shape=[jax.ShapeDtypeStruct((n_rows, d), F32)] + cast_shapes,
                compiler_params=cparams,
                name=f"outproj_ffn_{l}",
            )(*post_args, *ffn1_f32)

    return h.reshape(batch, seq, d)
```

```python
import functools

import numpy as np
import jax
import jax.numpy as jnp
from jax import lax
from jax.experimental import pallas as pl
from jax.experimental.pallas import tpu as pltpu

D_MODEL = 1024
N_META = 16
POOL_WIDTH = 512
POOL_GROUPS = 4
POOL_GROUP_DIM = 128
POOL_WINDOWS = (2, 4, 8, 16)
GLA_WIDTH = 512
GLA_HEADS = 4
GLA_KEY_WIDTH = 256
GLA_DK = 64
GLA_DV = 128
GLA_GATE_RANK = 16
GLA_GATE_TEMP = 16.0
GLA_CHUNK = 64
D_FF = 2816
LN_EPS = 1e-5
RMS_EPS = 1e-6

BATCH = 2
SC_ROWS = BATCH * GLA_CHUNK
SUB = 16
N_SUB = GLA_CHUNK // SUB
LANES = 128
SUBLANES = 8
PART_ROWS = 256
DENSE_TILE_CHUNKS = 4
MIXER_TILE_CHUNKS = 8
STAGGER_LAG = 1
VMEM_LIMIT_BYTES = 60 * 1024 * 1024
FAST_PATH_MAX_DECAY = 40.0
FLAG_ROWS = SUBLANES
N_FFN_WEIGHTS = 3
N_INPROJ_OUT = 8
WEIGHT_SLAB = 256
N_MIX_IN = 5
N_MIX_CONST = 7

F32 = jnp.float32
BF16 = jnp.bfloat16

assert POOL_WINDOWS == tuple(2 << g for g in range(POOL_GROUPS)) and POOL_WINDOWS[-1] == SUB


def _dot(a, b):
    return jnp.dot(a, b, preferred_element_type=F32)


def _dot_nt(a, b):
    return lax.dot_general(a, b, (((1,), (1,)), ((), ())), preferred_element_type=F32)


def _layer_norm(z, g, b):
    mu = jnp.mean(z, axis=-1, keepdims=True)
    zc = z - mu
    var = jnp.mean(zc * zc, axis=-1, keepdims=True)
    return zc * lax.rsqrt(var + LN_EPS) * g + b


def _silu(x):
    return x * jax.nn.sigmoid(x)


def _log_sigmoid(x):
    return jnp.minimum(x, 0.0) - jnp.log1p(jnp.exp(-jnp.abs(x)))


def _swiglu_half_step(x, wg_ref, wu_ref, wd_ref, alpha):
    xb = x.astype(BF16)
    g = _dot(xb, wg_ref[...])
    u = _dot(xb, wu_ref[...])
    yield
    act = (_silu(g) * u).astype(BF16)
    yield
    y = _dot(act, wd_ref[...])
    yield
    return alpha * x + 0.5 * y


def _staggered(gens):
    gens = list(gens)
    results = [None] * len(gens)
    done = [False] * len(gens)
    t = 0
    while not all(done):
        for k, gen in enumerate(gens):
            if done[k] or t < k * STAGGER_LAG:
                continue
            try:
                next(gen)
            except StopIteration as stop:
                results[k], done[k] = stop.value, True
        t += 1
    return results


def _ffn_inproj_rows(alpha, x, r0, is_meta, wg_ref, wu_ref, wd_ref, lng_ref, lnb_ref,
                     wtm_ref, wcm_ref, wgt_ref, wgut_ref, bgc_ref,
                     h_ref, u_ref, k_ref, qt_ref, lat_ref, vt_ref, rt_ref, flag_ref):
    n = x.shape[0]
    rows = slice(r0, r0 + n)
    z = yield from _swiglu_half_step(x, wg_ref, wu_ref, wd_ref, alpha)
    h = _layer_norm(z, lng_ref[...], lnb_ref[...])
    h_ref[rows, :] = h
    hb = h.astype(BF16)
    yield

    if is_meta:
        pad = GLA_CHUNK - N_META
        keep_r = (lax.broadcasted_iota(jnp.int32, (n, 1), 0) % GLA_CHUNK) >= pad
        keep_l = (lax.broadcasted_iota(jnp.int32, (1, n), 1) % GLA_CHUNK) >= pad
        mask_r = lambda v: jnp.where(keep_r, v, 0.0)
        mask_l = lambda v: jnp.where(keep_l, v, 0.0)
    else:
        mask_r = mask_l = lambda v: v

    qv = GLA_KEY_WIDTH + GLA_WIDTH
    gt = _dot_nt(wgt_ref[...], hb).astype(BF16)
    zt = _dot_nt(wcm_ref[:qv, :], hb)
    qt_ref[:, rows] = mask_l(zt[:GLA_KEY_WIDTH])
    vt_ref[:, rows] = mask_l(zt[GLA_KEY_WIDTH:]).astype(BF16)
    lat = mask_l(_log_sigmoid(_dot(wgut_ref[...], gt) + bgc_ref[...]) * (1.0 / GLA_GATE_TEMP))
    lat_ref[:, rows] = lat
    ztm = _dot(hb, wtm_ref[...])
    u_ref[rows, :] = mask_r(ztm[:, :POOL_WIDTH])
    k_ref[rows, :] = mask_r(ztm[:, POOL_WIDTH:])
    flags = []
    lane_lo = lax.broadcasted_iota(jnp.int32, (GLA_KEY_WIDTH, SC_ROWS), 1) < GLA_CHUNK
    for s in range(n // SC_ROWS):
        seg = lat[:, SC_ROWS * s:SC_ROWS * (s + 1)]
        tot = jnp.minimum(jnp.sum(jnp.where(lane_lo, seg, 0.0), axis=1, keepdims=True),
                          jnp.sum(jnp.where(lane_lo, 0.0, seg), axis=1, keepdims=True))
        flags.append((jnp.min(tot, axis=0, keepdims=True) < -FAST_PATH_MAX_DECAY).astype(jnp.int32))
    yield

    rt_ref[:, rows] = _silu(_dot_nt(wcm_ref[qv:, :], hb))
    return flags


def _cast_weight_slabs(plan, in_refs, out_refs):
    steps, slabs = plan
    step = pl.program_id(0)
    for m, (src, dst) in enumerate(zip(in_refs, out_refs)):
        @pl.when((step >= m * steps) & (step < m * steps + slabs))
        def _():
            dst[...] = src[...].astype(BF16)


def _ffn_inproj_kernel(alpha, from_x, tile_rows, cast_plan, x_ref, *refs):
    if from_x:
        meta_ref, *refs = refs
    if cast_plan is not None:
        n_par = len(refs) - 2 * N_FFN_WEIGHTS - N_INPROJ_OUT
        _cast_weight_slabs(cast_plan, refs[n_par:n_par + N_FFN_WEIGHTS], refs[-N_FFN_WEIGHTS:])
        refs = refs[:n_par] + refs[n_par + N_FFN_WEIGHTS:-N_FFN_WEIGHTS]
    flag_ref = refs[-1]
    last = pl.num_programs(0) - 1

    def x_rows(r0, n):
        if not from_x:
            return x_ref[r0:r0 + n, :]
        pieces = [divmod(r0 // GLA_CHUNK + i, BATCH) for i in range(n // GLA_CHUNK)]
        return jnp.concatenate([x_ref[b, s] for s, b in pieces], axis=0)

    def run(parts, is_meta):
        gens = [_ffn_inproj_rows(alpha, x, r0, is_meta, *refs) for x, r0 in parts]
        sc_flags = [f for fl in _staggered(gens) for f in fl]
        frow = lax.broadcasted_iota(jnp.int32, (FLAG_ROWS, LANES), 0)
        flags = jnp.zeros((FLAG_ROWS, LANES), jnp.int32)
        for s, f in enumerate(sc_flags):
            flags = jnp.where(frow == s, f, flags)
        flag_ref[...] = flags

    @pl.when(pl.program_id(0) < last)
    def _():
        pr = min(PART_ROWS, tile_rows)
        run([(x_rows(r0, pr), r0) for r0 in range(0, tile_rows, pr)], False)

    @pl.when(pl.program_id(0) == last)
    def _():
        run([(meta_ref[...] if from_x else x_ref[0:SC_ROWS, :], 0)], True)


def _split3(x):
    hi = x.astype(BF16)
    r1 = x - hi.astype(F32)
    mid = r1.astype(BF16)
    lo = (r1 - mid.astype(F32)).astype(BF16)
    return hi, mid, lo


def _transpose_cm(xt):
    return jnp.concatenate([xt[:SC_ROWS].T, xt[SC_ROWS:].T], axis=1)


def _dot3_right(x, c01):
    parts = _split3(x)
    return _dot(parts[0], c01) + _dot(parts[1], c01) + _dot(parts[2], c01)


def _stage_pool_rows(n, u_ref, ext_ref):
    C = GLA_CHUNK
    for c in range(n):
        for b in range(BATCH):
            ext_ref[b, SUB + C * c:SUB + C * (c + 1), :] = u_ref[SC_ROWS * c + C * b:SC_ROWS * c + C * (b + 1), :]


def _pool_chunk(c, u_ref, ext_ref, cnt, wpool_ref, pscale_ref):
    C = GLA_CHUNK
    u = u_ref[SC_ROWS * c:SC_ROWS * (c + 1), :]
    win = []
    for b in range(BATCH):
        level = ext_ref[b, C * c:C * c + SUB + C, :]
        for g in range(POOL_GROUPS):
            level = level[:, POOL_GROUP_DIM if g else 0:]
            level = level + pltpu.roll(level, 1 << g, axis=0)
            win.append(level[SUB:, :POOL_GROUP_DIM])
    parts = []
    for g in range(POOL_GROUPS):
        cols = slice(POOL_GROUP_DIM * g, POOL_GROUP_DIM * (g + 1))
        s = jnp.concatenate([win[b * POOL_GROUPS + g] for b in range(BATCH)], axis=0)
        p = s / cnt[:, cols] - u[:, cols]
        parts.append(_dot(p.astype(BF16), wpool_ref[g]))
    return jnp.concatenate(parts, axis=1) * pscale_ref[...]


def _mix_chunks(n, robust, in_refs, const_refs, y_ref, st_ref, b_scr, sd_scr, at_scr):
    k_ref, qt_ref, lat_ref, vt_ref, rt_ref = in_refs
    ubd_ref, uprev_ref, slot_ref, dmask_ref, causal_ref, eye_ref, gn_ref = const_refs
    C = GLA_CHUNK
    lane = lax.broadcasted_iota(jnp.int32, (SC_ROWS, SC_ROWS), 1)
    rowi = lax.broadcasted_iota(jnp.int32, (SC_ROWS, SC_ROWS), 0)
    lane64 = lax.broadcasted_iota(jnp.int32, (C, SC_ROWS), 1)
    lane8 = lax.broadcasted_iota(jnp.int32, (SUBLANES, SC_ROWS), 1)
    lane_lo = lane < C
    row_lo = rowi < C
    row_head = lax.broadcasted_iota(jnp.int32, (GLA_KEY_WIDTH, SC_ROWS), 0) // GLA_DK

    def heads_on_lanes(xt):
        return jnp.concatenate([jnp.where(row_head == h, xt, 0.0) for h in range(GLA_HEADS)], axis=1).astype(BF16)

    states = [st_ref[h] for h in range(GLA_HEADS)]
    ubd = ubd_ref[...]
    eye = eye_ref[...]
    gn = gn_ref[...]

    def chunk(c):
        rows = slice(SC_ROWS * c, SC_ROWS * (c + 1))
        bT = _dot3_right(lat_ref[:, rows], ubd)
        yield
        b_tm = _transpose_cm(bT)
        b_scr[c] = b_tm
        ends = [[b_scr[c, pl.ds(C * b + SUB * j + SUB - 1, 1), :] for j in range(N_SUB)] for b in range(BATCH)]

        def per_block(fn):
            return jnp.concatenate(
                [jnp.broadcast_to(fn(b, j), (SUB, GLA_KEY_WIDTH)) for b in range(BATCH) for j in range(N_SUB)], axis=0)

        k_tm = k_ref[rows, :]
        qT = qt_ref[:, rows]
        e_last = per_block(lambda b, j: ends[b][N_SUB - 1])
        ktil = k_tm * jnp.exp(e_last - b_tm)
        qeT = qT * jnp.exp(bT)

        if not robust:
            kneg = (k_tm * jnp.exp(-b_tm)).astype(BF16)
            at_all = jnp.where(causal_ref[...] != 0, _dot(kneg, heads_on_lanes(qeT)), 0.0)
        else:
            cprevT = _dot3_right(lat_ref[:, rows], uprev_ref[...])
            e_own = per_block(lambda b, j: ends[b][j])
            f2 = per_block(lambda b, j: jnp.exp(ends[b][min(j + 1, N_SUB - 1)] - ends[b][j]))
            f3 = per_block(lambda b, j: jnp.exp(ends[b][min(j + 2, N_SUB - 1)] - ends[b][j]))
            khat = k_tm * jnp.exp(e_own - b_tm)
            kslots = jnp.concatenate([khat, khat * f2, khat * f3], axis=0).astype(BF16)
            r_all = _dot(kslots, heads_on_lanes(qT * jnp.exp(bT - cprevT)))
            slot = slot_ref[...]
            at_off = jnp.where(slot == 1, r_all[0:SC_ROWS],
                               jnp.where(slot == 2, r_all[SC_ROWS:2 * SC_ROWS],
                                         jnp.where(slot == 3, r_all[2 * SC_ROWS:], 0.0)))
            kT = jnp.concatenate([k_tm[:, :LANES].T, k_tm[:, LANES:].T], axis=0)
            for dist in range(SUB):
                if dist == 0:
                    qs, bs = qT, bT
                else:
                    qs = pltpu.roll(qT, SC_ROWS - dist, axis=1)
                    bs = pltpu.roll(bT, SC_ROWS - dist, axis=1)
                prod = qs * kT * jnp.exp(jnp.minimum(bs - bT, 0.0))
                for h in range(GLA_HEADS):
                    sd_scr[pl.ds(SUB * h + dist, 1), :] = jnp.sum(
                        prod[GLA_DK * h:GLA_DK * (h + 1)], axis=0, keepdims=True)
            sd = jnp.concatenate([sd_scr[...], jnp.zeros((SC_ROWS - GLA_HEADS * SUB, SC_ROWS), F32)], axis=0)
            sdt = sd.T
            dmask = dmask_ref[...]
            for h in range(GLA_HEADS):
                xh = jnp.where((lane >= SUB * h) & (lane < SUB * (h + 1)), sdt, 0.0)
                skew = pltpu.roll(xh, (SC_ROWS - SUB * h) % SC_ROWS, axis=1, stride=1, stride_axis=0)
                at_scr[:, SC_ROWS * h:SC_ROWS * (h + 1)] = (
                    jnp.where(dmask != 0, skew, 0.0) + at_off[:, SC_ROWS * h:SC_ROWS * (h + 1)])
            at_all = at_scr[...]
        yield

        vT = vt_ref[:, rows]
        rT = rt_ref[:, rows]
        results = []
        for h in range(GLA_HEADS):
            at_h = at_all[:, SC_ROWS * h:SC_ROWS * (h + 1)]

            half = (h % 2) * C
            kt_cols = ktil[:, LANES * (h // 2):LANES * (h // 2 + 1)]
            kt_roll = pltpu.roll(kt_cols, C, axis=1)
            lo_src, hi_src = (kt_cols, kt_roll) if half == 0 else (kt_roll, kt_cols)
            kbd = jnp.where(row_lo & lane_lo, lo_src, jnp.where((~row_lo) & (~lane_lo), hi_src, 0.0))
            qe_h = qeT[GLA_DK * h:GLA_DK * (h + 1)]
            qebd = jnp.concatenate([jnp.where(lane64 < C, qe_h, 0.0), jnp.where(lane64 >= C, qe_h, 0.0)], axis=0)

            v_h = vT[GLA_DV * h:GLA_DV * (h + 1)]
            res = _dot(v_h, jnp.concatenate([at_h, kbd], axis=1).astype(BF16))
            results.append((res, _dot(states[h].astype(BF16), qebd.astype(BF16))))
        yield

        y_heads = []
        for h in range(GLA_HEADS):
            half = (h % 2) * C
            res, o_inter = results[h]
            st = states[h]
            o_t = res[:, :SC_ROWS] + o_inter

            e0 = jnp.broadcast_to(ends[0][N_SUB - 1][:, LANES * (h // 2):LANES * (h // 2 + 1)], (SUBLANES, LANES))
            e1 = jnp.broadcast_to(ends[1][N_SUB - 1][:, LANES * (h // 2):LANES * (h // 2 + 1)], (SUBLANES, LANES))
            if half == 0:
                e1 = pltpu.roll(e1, C, axis=1)
            else:
                e0 = pltpu.roll(e0, C, axis=1)
            dec = jnp.exp(jnp.where(lane8 < C, e0, e1))
            states[h] = st * jnp.broadcast_to(dec[0:1], (GLA_DV, SC_ROWS)) + res[:, SC_ROWS:]

            ms = jnp.mean(o_t * o_t, axis=0, keepdims=True)
            gsl = slice(GLA_DV * h, GLA_DV * (h + 1))
            y_heads.append(o_t * lax.rsqrt(ms + RMS_EPS) * gn[gsl] * rT[gsl])
        y_t = jnp.concatenate(y_heads, axis=0).astype(BF16)
        y_gla = _dot_nt(eye, y_t)
        yield
        y_ref[rows, :] = y_gla.astype(BF16)

    def in_turn():
        for c in range(n):
            yield from chunk(c)

    _staggered([in_turn()] if robust else [chunk(c) for c in range(n)])
    for h in range(GLA_HEADS):
        st_ref[h] = states[h]


def _outproj_ffn_rows(alpha, h, mix_fn, store, wo_ref, l2g_ref, l2b_ref, wg_ref, wu_ref, wd_ref, l3g_ref, l3b_ref):
    z = alpha * h + _dot(mix_fn(), wo_ref[...])
    yield
    h2 = _layer_norm(z, l2g_ref[...], l2b_ref[...])
    yield
    z2 = yield from _swiglu_half_step(h2, wg_ref, wu_ref, wd_ref, alpha)
    store(_layer_norm(z2, l3g_ref[...], l3b_ref[...]))


def _mixer_kernel(tc, flags_ref, *refs):
    in_refs, const_refs, y_ref = refs[:N_MIX_IN], refs[N_MIX_IN:N_MIX_IN + N_MIX_CONST], refs[N_MIX_IN + N_MIX_CONST]
    st_ref, b_scr, sd_scr, at_scr = refs[N_MIX_IN + N_MIX_CONST + 1:]
    step = pl.program_id(0)

    def run(n, robust):
        _mix_chunks(n, robust, in_refs, const_refs, y_ref, st_ref, b_scr, sd_scr, at_scr)

    @pl.when(step == 0)
    def _():
        st_ref[...] = jnp.zeros(st_ref.shape, F32)
        run(1, True)

    @pl.when(step > 0)
    def _():
        base = (step - 1) * tc
        slow = flags_ref[base]
        for c in range(1, tc):
            slow = jnp.maximum(slow, flags_ref[base + c])

        @pl.when(slow == 0)
        def _():
            run(tc, False)

        @pl.when(slow != 0)
        def _():
            run(tc, True)


def _outproj_ffn_kernel(alpha, to_out, cast_plan, h_ref, y_ref, u_ref, umeta_ref, cnt_ref, wpool_ref, pscale_ref,
                        *refs):
    *refs, ext_ref = refs
    if cast_plan is not None:
        n_in = len(refs) - 2 * N_FFN_WEIGHTS - 1
        _cast_weight_slabs(cast_plan, refs[n_in:n_in + N_FFN_WEIGHTS], refs[-N_FFN_WEIGHTS:])
        refs = refs[:n_in] + refs[n_in + N_FFN_WEIGHTS:-N_FFN_WEIGHTS]
    *w_refs, o_ref = refs
    tile_rows = h_ref.shape[0]
    C = GLA_CHUNK
    pad = C - N_META

    def store_rows(r0):
        def store(o):
            o_ref[r0:r0 + o.shape[0], :] = o
        return store

    def store_chunks(r0):
        def store(o):
            for sb in range(o.shape[0] // GLA_CHUNK):
                s, b = divmod(r0 // GLA_CHUNK + sb, BATCH)
                o_ref[b, s] = o[sb * GLA_CHUNK:(sb + 1) * GLA_CHUNK]
        return store

    def mixed(r0, r1, cnt):
        def fn():
            pool = [_pool_chunk(c, u_ref, ext_ref, cnt, wpool_ref, pscale_ref)
                    for c in range(r0 // SC_ROWS, r1 // SC_ROWS)]
            return jnp.concatenate([jnp.concatenate(pool, axis=0).astype(BF16), y_ref[r0:r1, :]], axis=1)
        return fn

    def parts(bounds, make_store, cnt):
        _staggered([_outproj_ffn_rows(alpha, h_ref[r0:r1, :], mixed(r0, r1, cnt), make_store(r0), *w_refs)
                    for r0, r1 in zip(bounds[:-1], bounds[1:])])

    def main_tile(make_store):
        n = tile_rows // SC_ROWS

        @pl.when(pl.program_id(0) == 0)
        def _():
            for b in range(BATCH):
                ext_ref[b, 0:SUB, :] = umeta_ref[C * b + pad:C * (b + 1), :]
        _stage_pool_rows(n, u_ref, ext_ref)
        parts(list(range(0, tile_rows, min(PART_ROWS, tile_rows))) + [tile_rows], make_store, cnt_ref[0])
        for b in range(BATCH):
            ext_ref[b, 0:SUB, :] = ext_ref[b, C * n:C * n + SUB, :]

    if to_out:
        main_tile(store_chunks)
        return
    last = pl.num_programs(0) - 1

    @pl.when(pl.program_id(0) < last)
    def _():
        main_tile(store_rows)

    @pl.when(pl.program_id(0) == last)
    def _():
        ext_ref[:, 0:SUB, :] = jnp.zeros((BATCH, SUB, POOL_WIDTH), F32)
        _stage_pool_rows(1, u_ref, ext_ref)
        parts([0, SC_ROWS], store_rows, cnt_ref[1])


def _resident(shape):
    nd = len(shape)
    return pl.BlockSpec(shape, lambda *_: (0,) * nd, pipeline_mode=pl.Buffered(1))


def _layer_resident(shape, layer):
    nd = len(shape) - 1
    return pl.BlockSpec((None,) + tuple(shape[1:]), lambda *_: (layer,) + (0,) * nd, pipeline_mode=pl.Buffered(1))


def _mixer_constants():
    r = np.arange(SC_ROWS)
    b, t = r // GLA_CHUNK, r % GLA_CHUNK
    same_b = b[:, None] == b[None, :]
    ubd = same_b & (t[:, None] <= t[None, :])
    blk = t // SUB
    uprev = same_b & (blk[:, None] < blk[None, :])
    dist = np.where(same_b, blk[None, :] - blk[:, None], 0)
    slot = np.where((dist >= 1) & (dist < N_SUB), dist, 0).astype(np.int32)
    slot = np.tile(slot, (1, GLA_HEADS))
    dmask = (same_b & (blk[:, None] == blk[None, :]) & (t[None, :] >= t[:, None])).astype(np.int32)
    causal = np.tile((same_b & (t[None, :] >= t[:, None])).astype(np.int32), (1, GLA_HEADS))
    eye = np.eye(SC_ROWS)
    w_lane = np.repeat(np.array(POOL_WINDOWS, np.float32), POOL_GROUP_DIM)[None, :]
    t_meta = np.maximum(t - (GLA_CHUNK - N_META), 0).astype(np.float32)[:, None]
    cnt = np.stack([np.broadcast_to(w_lane, (SC_ROWS, POOL_WIDTH)), np.minimum(t_meta + 1.0, w_lane)])
    return (jnp.asarray(ubd, BF16), jnp.asarray(uprev, BF16), jnp.asarray(slot),
            jnp.asarray(dmask), jnp.asarray(causal), jnp.asarray(eye, BF16), jnp.asarray(cnt, F32))


def _tile_chunks(n_ch, most):
    return next(k for k in (8, 4, 2, 1) if k <= most and n_ch % k == 0)


def kernel(x, meta_tokens, ffn1_w_gate, ffn1_w_up, ffn1_w_down, ln1_g, ln1_b, w_in, w_gate_up, b_gate, w_pool, pool_scale, gla_norm_g, w_out, ln2_g, ln2_b, ffn2_w_gate, ffn2_w_up, ffn2_w_down, ln3_g, ln3_b):
    batch, seq, d = x.shape
    depth = w_in.shape[0]
    assert batch == BATCH and d == D_MODEL and seq % GLA_CHUNK == 0
    n_ch = seq // GLA_CHUNK
    n_sc = n_ch + 1
    n_rows = n_sc * SC_ROWS
    tc = _tile_chunks(n_ch, DENSE_TILE_CHUNKS)
    tr = tc * SC_ROWS
    n_main = n_ch // tc
    tcm = _tile_chunks(n_ch, MIXER_TILE_CHUNKS)
    n_mix = n_ch // tcm
    alpha = (2.0 * depth) ** 0.25

    x4 = x.reshape(batch, n_ch, GLA_CHUNK, d)
    meta_half = jnp.concatenate([jnp.zeros((GLA_CHUNK - N_META, d), x.dtype), meta_tokens.astype(x.dtype)], axis=0)
    meta_sc = jnp.concatenate([meta_half, meta_half], axis=0)
    h = None

    ubd, uprev, slot, dmask, causal, eye, cnt = _mixer_constants()
    s0 = POOL_WIDTH
    s1 = s0 + GLA_KEY_WIDTH
    s2 = s1 + GLA_KEY_WIDTH
    s3 = s2 + GLA_WIDTH
    s4 = s3 + GLA_WIDTH
    cparams = pltpu.CompilerParams(dimension_semantics=("arbitrary",), vmem_limit_bytes=VMEM_LIMIT_BYTES)
    row_tile = lambda cols: pl.BlockSpec((tr, cols), lambda i: (i, 0))
    col_tile = lambda rows: pl.BlockSpec((rows, tr), lambda i: (0, i))
    x4_tile = pl.BlockSpec((batch, tc, GLA_CHUNK, d), lambda i: (0, jnp.minimum(i, n_main - 1), 0, 0))
    mix_tile = lambda i: lax.rem(i + n_mix, n_mix + 1)
    mix_rows = lambda cols: pl.BlockSpec((tcm * SC_ROWS, cols), lambda i, f: (mix_tile(i), 0))
    mix_cols = lambda rows: pl.BlockSpec((rows, tcm * SC_ROWS), lambda i, f: (0, mix_tile(i)))

    ffn1 = tuple(w[0].astype(BF16) for w in (ffn1_w_gate, ffn1_w_up, ffn1_w_down))
    ffn_specs = [_resident(w.shape) for w in ffn1]
    wo_all, wpool_all = w_out.astype(BF16), w_pool.astype(BF16)
    row3 = lambda p: p.reshape(depth, 1, p.shape[-1])
    l1g, l1b, l2g, l2b, l3g, l3b = (row3(p) for p in (ln1_g, ln1_b, ln2_g, ln2_b, ln3_g, ln3_b))
    pscale_all = row3(pool_scale)

    def cast_job(n_steps, layer, stacked):
        steps = n_steps // N_FFN_WEIGHTS
        assert steps >= 1, "weight-cast side job needs at least three grid steps"
        width, slabs = (WEIGHT_SLAB, D_FF // WEIGHT_SLAB) if steps >= D_FF // WEIGHT_SLAB else (D_FF, 1)
        slab = lambda m: (lambda i: jnp.clip(i - m * steps, 0, slabs - 1))
        in_specs = [pl.BlockSpec((None, d, width), lambda i, j=slab(0): (layer, 0, j(i))),
                    pl.BlockSpec((None, d, width), lambda i, j=slab(1): (layer, 0, j(i))),
                    pl.BlockSpec((None, width, d), lambda i, j=slab(2): (layer, j(i), 0))]
        out_specs = [pl.BlockSpec((d, width), lambda i, j=slab(0): (0, j(i))),
                     pl.BlockSpec((d, width), lambda i, j=slab(1): (0, j(i))),
                     pl.BlockSpec((width, d), lambda i, j=slab(2): (j(i), 0))]
        out_shapes = [jax.ShapeDtypeStruct(w.shape[1:], BF16) for w in stacked]
        return (steps, slabs), in_specs, out_specs, out_shapes

    for l in range(depth):
        wl = w_in[l]
        glr_pad = jnp.zeros((d, LANES - GLA_GATE_RANK), F32)
        wtm = jnp.concatenate([wl[:, :s0], wl[:, s1:s2]], axis=1).astype(BF16)
        wcm = jnp.concatenate([wl[:, s0:s1] * (GLA_DK ** -0.5), wl[:, s2:s3], wl[:, s3:s4]], axis=1).T.astype(BF16)
        wgt = jnp.concatenate([wl[:, s4:], glr_pad], axis=1).T.astype(BF16)
        wgut = jnp.concatenate([w_gate_up[l], jnp.zeros((LANES - GLA_GATE_RANK, GLA_KEY_WIDTH), F32)], axis=0).T.astype(BF16)
        bgc = b_gate[l].reshape(GLA_KEY_WIDTH, 1)

        from_x = l == 0
        acts = (x4, meta_sc) if from_x else (h,)
        act_specs = [x4_tile, _resident(meta_sc.shape)] if from_x else [row_tile(d)]
        ffn2_f32 = (ffn2_w_gate, ffn2_w_up, ffn2_w_down)
        plan, cast_in, cast_out, cast_shapes = cast_job(n_main + 1, l, ffn2_f32)
        outs = pl.pallas_call(
            functools.partial(_ffn_inproj_kernel, alpha, from_x, tr, plan),
            grid=(n_main + 1,),
            in_specs=act_specs + ffn_specs + [_layer_resident(l1g.shape, l), _layer_resident(l1b.shape, l),
                                              _resident(wtm.shape), _resident(wcm.shape), _resident(wgt.shape),
                                              _resident(wgut.shape), _resident(bgc.shape)] + cast_in,
            out_specs=[row_tile(d), row_tile(POOL_WIDTH), row_tile(GLA_KEY_WIDTH),
                       col_tile(GLA_KEY_WIDTH), col_tile(GLA_KEY_WIDTH), col_tile(GLA_WIDTH), col_tile(GLA_WIDTH),
                       pl.BlockSpec((FLAG_ROWS, LANES), lambda i: (i, 0))] + cast_out,
            out_shape=[jax.ShapeDtypeStruct((n_rows, d), F32),
                       jax.ShapeDtypeStruct((n_rows, POOL_WIDTH), F32),
                       jax.ShapeDtypeStruct((n_rows, GLA_KEY_WIDTH), F32),
                       jax.ShapeDtypeStruct((GLA_KEY_WIDTH, n_rows), F32),
                       jax.ShapeDtypeStruct((GLA_KEY_WIDTH, n_rows), F32),
                       jax.ShapeDtypeStruct((GLA_WIDTH, n_rows), BF16),
                       jax.ShapeDtypeStruct((GLA_WIDTH, n_rows), F32),
                       jax.ShapeDtypeStruct(((n_main + 1) * FLAG_ROWS, LANES), jnp.int32)] + cast_shapes,
            compiler_params=cparams,
            name=f"ffn_inproj_{l}",
        )(*acts, *ffn1, l1g, l1b, wtm, wcm, wgt, wgut, bgc, *ffn2_f32)
        h1, u_tm, k_tm, q_t, la_t, v_t, r_t, tile_flags = outs[:N_INPROJ_OUT]
        ffn2 = outs[N_INPROJ_OUT:]
        sc_flags = tile_flags.reshape(n_main + 1, FLAG_ROWS, LANES)[:, :tc, 0].reshape(-1)[:n_sc]

        gn = jnp.broadcast_to(gla_norm_g[l].reshape(GLA_WIDTH, 1), (GLA_WIDTH, SC_ROWS))
        mix_in = (k_tm, q_t, la_t, v_t, r_t)
        mix_in_specs = [mix_rows(GLA_KEY_WIDTH), mix_cols(GLA_KEY_WIDTH),
                        mix_cols(GLA_KEY_WIDTH), mix_cols(GLA_WIDTH), mix_cols(GLA_WIDTH)]
        mix_const = (ubd, uprev, slot, dmask, causal, eye, gn)
        mix_const_specs = [_resident(c.shape) for c in mix_const]
        assert len(mix_in) == N_MIX_IN and len(mix_const) == N_MIX_CONST
        y_gla = pl.pallas_call(
            functools.partial(_mixer_kernel, tcm),
            grid_spec=pltpu.PrefetchScalarGridSpec(
                num_scalar_prefetch=1,
                grid=(n_mix + 1,),
                in_specs=mix_in_specs + mix_const_specs,
                out_specs=mix_rows(GLA_WIDTH),
                scratch_shapes=[pltpu.VMEM((GLA_HEADS, GLA_DV, SC_ROWS), F32),
                                pltpu.VMEM((tcm, SC_ROWS, GLA_KEY_WIDTH), F32),
                                pltpu.VMEM((GLA_HEADS * SUB, SC_ROWS), F32),
                                pltpu.VMEM((SC_ROWS, GLA_HEADS * SC_ROWS), F32)]),
            out_shape=jax.ShapeDtypeStruct((n_rows, GLA_WIDTH), BF16),
            compiler_params=cparams,
            name=f"mixer_{l}",
        )(sc_flags, *mix_in, *mix_const)

        to_out = l == depth - 1
        post_specs = [row_tile(d), row_tile(GLA_WIDTH), row_tile(POOL_WIDTH),
                      pl.BlockSpec((tr, POOL_WIDTH), lambda i: (n_main, 0), pipeline_mode=pl.Buffered(1)),
                      _resident(cnt.shape), _layer_resident(wpool_all.shape, l), _layer_resident(pscale_all.shape, l),
                      _layer_resident(wo_all.shape, l),
                      _layer_resident(l2g.shape, l), _layer_resident(l2b.shape, l)] + ffn_specs + [
                      _layer_resident(l3g.shape, l), _layer_resident(l3b.shape, l)]
        post_args = (h1, y_gla, u_tm, u_tm, cnt, wpool_all, pscale_all, wo_all, l2g, l2b, *ffn2, l3g, l3b)
        pool_scratch = [pltpu.VMEM((BATCH, SUB + tc * GLA_CHUNK, POOL_WIDTH), F32)]
        if to_out:
            h = pl.pallas_call(
                functools.partial(_outproj_ffn_kernel, alpha, True, None),
                grid=(n_main,),
                in_specs=post_specs,
                out_specs=pl.BlockSpec((batch, tc, GLA_CHUNK, d), lambda i: (0, i, 0, 0)),
                out_shape=jax.ShapeDtypeStruct(x4.shape, F32),
                scratch_shapes=pool_scratch,
                compiler_params=cparams,
                name=f"outproj_ffn_{l}",
            )(*post_args)
        else:
            ffn1_f32 = (ffn1_w_gate, ffn1_w_up, ffn1_w_down)
            plan, cast_in, cast_out, cast_shapes = cast_job(n_main + 1, l + 1, ffn1_f32)
            h, *ffn1 = pl.pallas_call(
                functools.partial(_outproj_ffn_kernel, alpha, False, plan),
                grid=(n_main + 1,),
                in_specs=post_specs + cast_in,
                out_specs=[row_tile(d)] + cast_out,
                out_shape=[jax.ShapeDtypeStruct((n_rows, d), F32)] + cast_shapes,
                scratch_shapes=pool_scratch,
                compiler_params=cparams,
                name=f"outproj_ffn_{l}",
            )(*post_args, *ffn1_f32)

    return h.reshape(batch, seq, d)
```

```python
import functools

import numpy as np
import jax
import jax.numpy as jnp
from jax import lax
from jax.experimental import pallas as pl
from jax.experimental.pallas import tpu as pltpu

D_MODEL = 1024
N_META = 16
POOL_WIDTH = 512
POOL_GROUPS = 4
POOL_GROUP_DIM = 128
POOL_WINDOWS = (2, 4, 8, 16)
GLA_WIDTH = 512
GLA_HEADS = 4
GLA_KEY_WIDTH = 256
GLA_DK = 64
GLA_DV = 128
GLA_GATE_RANK = 16
GLA_GATE_TEMP = 16.0
GLA_CHUNK = 64
D_FF = 2816
LN_EPS = 1e-5
RMS_EPS = 1e-6

BATCH = 2
SC_ROWS = BATCH * GLA_CHUNK
SUB = 16
N_SUB = GLA_CHUNK // SUB
LANES = 128
SUBLANES = 8
PART_ROWS = 256
DENSE_TILE_CHUNKS = 4
MIXER_TILE_CHUNKS = 8
STAGGER_LAG = 1
VMEM_LIMIT_BYTES = 60 * 1024 * 1024
FAST_PATH_MAX_DECAY = 40.0
FLAG_ROWS = SUBLANES
N_FFN_WEIGHTS = 3
N_INPROJ_OUT = 8
WEIGHT_SLAB = 256
N_MIX_IN = 6
N_MIX_CONST = 9

F32 = jnp.float32
BF16 = jnp.bfloat16

assert POOL_WINDOWS == tuple(2 << g for g in range(POOL_GROUPS)) and POOL_WINDOWS[-1] == SUB


def _dot(a, b):
    return jnp.dot(a, b, preferred_element_type=F32)


def _dot_nt(a, b):
    return lax.dot_general(a, b, (((1,), (1,)), ((), ())), preferred_element_type=F32)


def _layer_norm(z, g, b):
    mu = jnp.mean(z, axis=-1, keepdims=True)
    zc = z - mu
    var = jnp.mean(zc * zc, axis=-1, keepdims=True)
    return zc * lax.rsqrt(var + LN_EPS) * g + b


def _silu(x):
    return x * jax.nn.sigmoid(x)


def _log_sigmoid(x):
    return jnp.minimum(x, 0.0) - jnp.log1p(jnp.exp(-jnp.abs(x)))


def _swiglu_half_step(x, wg_ref, wu_ref, wd_ref, alpha):
    xb = x.astype(BF16)
    g = _dot(xb, wg_ref[...])
    u = _dot(xb, wu_ref[...])
    yield
    act = (_silu(g) * u).astype(BF16)
    yield
    y = _dot(act, wd_ref[...])
    yield
    return alpha * x + 0.5 * y


def _staggered(gens):
    gens = list(gens)
    results = [None] * len(gens)
    done = [False] * len(gens)
    t = 0
    while not all(done):
        for k, gen in enumerate(gens):
            if done[k] or t < k * STAGGER_LAG:
                continue
            try:
                next(gen)
            except StopIteration as stop:
                results[k], done[k] = stop.value, True
        t += 1
    return results


def _ffn_inproj_rows(alpha, x, r0, is_meta, wg_ref, wu_ref, wd_ref, lng_ref, lnb_ref,
                     wtm_ref, wcm_ref, wgt_ref, wgut_ref, bgc_ref,
                     h_ref, u_ref, k_ref, qt_ref, lat_ref, vt_ref, rt_ref, flag_ref):
    n = x.shape[0]
    rows = slice(r0, r0 + n)
    z = yield from _swiglu_half_step(x, wg_ref, wu_ref, wd_ref, alpha)
    h = _layer_norm(z, lng_ref[...], lnb_ref[...])
    h_ref[rows, :] = h
    hb = h.astype(BF16)
    yield

    if is_meta:
        pad = GLA_CHUNK - N_META
        keep_r = (lax.broadcasted_iota(jnp.int32, (n, 1), 0) % GLA_CHUNK) >= pad
        keep_l = (lax.broadcasted_iota(jnp.int32, (1, n), 1) % GLA_CHUNK) >= pad
        mask_r = lambda v: jnp.where(keep_r, v, 0.0)
        mask_l = lambda v: jnp.where(keep_l, v, 0.0)
    else:
        mask_r = mask_l = lambda v: v

    qv = GLA_KEY_WIDTH + GLA_WIDTH
    gt = _dot_nt(wgt_ref[...], hb).astype(BF16)
    zt = _dot_nt(wcm_ref[:qv, :], hb)
    qt_ref[:, rows] = mask_l(zt[:GLA_KEY_WIDTH])
    vt_ref[:, rows] = mask_l(zt[GLA_KEY_WIDTH:]).astype(BF16)
    lat = mask_l(_log_sigmoid(_dot(wgut_ref[...], gt) + bgc_ref[...]) * (1.0 / GLA_GATE_TEMP))
    lat_ref[:, rows] = lat
    ztm = _dot(hb, wtm_ref[...])
    u_ref[rows, :] = mask_r(ztm[:, :POOL_WIDTH])
    k_ref[rows, :] = mask_r(ztm[:, POOL_WIDTH:])
    flags = []
    lane_lo = lax.broadcasted_iota(jnp.int32, (GLA_KEY_WIDTH, SC_ROWS), 1) < GLA_CHUNK
    for s in range(n // SC_ROWS):
        seg = lat[:, SC_ROWS * s:SC_ROWS * (s + 1)]
        tot = jnp.minimum(jnp.sum(jnp.where(lane_lo, seg, 0.0), axis=1, keepdims=True),
                          jnp.sum(jnp.where(lane_lo, 0.0, seg), axis=1, keepdims=True))
        flags.append((jnp.min(tot, axis=0, keepdims=True) < -FAST_PATH_MAX_DECAY).astype(jnp.int32))
    yield

    rt_ref[:, rows] = _silu(_dot_nt(wcm_ref[qv:, :], hb))
    return flags


def _cast_weight_slabs(plan, in_refs, out_refs):
    steps, slabs = plan
    step = pl.program_id(0)
    for m, (src, dst) in enumerate(zip(in_refs, out_refs)):
        @pl.when((step >= m * steps) & (step < m * steps + slabs))
        def _():
            dst[...] = src[...].astype(BF16)


def _ffn_inproj_kernel(alpha, from_x, tile_rows, cast_plan, x_ref, *refs):
    if from_x:
        meta_ref, *refs = refs
    if cast_plan is not None:
        n_par = len(refs) - 2 * N_FFN_WEIGHTS - N_INPROJ_OUT
        _cast_weight_slabs(cast_plan, refs[n_par:n_par + N_FFN_WEIGHTS], refs[-N_FFN_WEIGHTS:])
        refs = refs[:n_par] + refs[n_par + N_FFN_WEIGHTS:-N_FFN_WEIGHTS]
    flag_ref = refs[-1]
    last = pl.num_programs(0) - 1

    def x_rows(r0, n):
        if not from_x:
            return x_ref[r0:r0 + n, :]
        pieces = [divmod(r0 // GLA_CHUNK + i, BATCH) for i in range(n // GLA_CHUNK)]
        return jnp.concatenate([x_ref[b, s] for s, b in pieces], axis=0)

    def run(parts, is_meta):
        gens = [_ffn_inproj_rows(alpha, x, r0, is_meta, *refs) for x, r0 in parts]
        sc_flags = [f for fl in _staggered(gens) for f in fl]
        frow = lax.broadcasted_iota(jnp.int32, (FLAG_ROWS, LANES), 0)
        flags = jnp.zeros((FLAG_ROWS, LANES), jnp.int32)
        for s, f in enumerate(sc_flags):
            flags = jnp.where(frow == s, f, flags)
        flag_ref[...] = flags

    @pl.when(pl.program_id(0) < last)
    def _():
        pr = min(PART_ROWS, tile_rows)
        run([(x_rows(r0, pr), r0) for r0 in range(0, tile_rows, pr)], False)

    @pl.when(pl.program_id(0) == last)
    def _():
        run([(meta_ref[...] if from_x else x_ref[0:SC_ROWS, :], 0)], True)


def _split3(x):
    hi = x.astype(BF16)
    r1 = x - hi.astype(F32)
    mid = r1.astype(BF16)
    lo = (r1 - mid.astype(F32)).astype(BF16)
    return hi, mid, lo


def _transpose_cm(xt):
    return jnp.concatenate([xt[:SC_ROWS].T, xt[SC_ROWS:].T], axis=1)


def _dot3_right(x, c01):
    parts = _split3(x)
    return _dot(parts[0], c01) + _dot(parts[1], c01) + _dot(parts[2], c01)


def _mix_chunks(n, robust, in_refs, const_refs, y_ref, st_ref, ext_ref, b_scr, sd_scr, at_scr):
    u_ref, k_ref, qt_ref, lat_ref, vt_ref, rt_ref = in_refs
    (ubd_ref, uprev_ref, slot_ref, dmask_ref, causal_ref, gn_ref, cnt_ref,
     wpool_ref, pscale_ref) = const_refs
    C = GLA_CHUNK
    lane = lax.broadcasted_iota(jnp.int32, (SC_ROWS, SC_ROWS), 1)
    rowi = lax.broadcasted_iota(jnp.int32, (SC_ROWS, SC_ROWS), 0)
    lane64 = lax.broadcasted_iota(jnp.int32, (C, SC_ROWS), 1)
    lane8 = lax.broadcasted_iota(jnp.int32, (SUBLANES, SC_ROWS), 1)
    lane_lo = lane < C
    row_lo = rowi < C
    row_head = lax.broadcasted_iota(jnp.int32, (GLA_KEY_WIDTH, SC_ROWS), 0) // GLA_DK

    def heads_on_lanes(xt):
        return jnp.concatenate([jnp.where(row_head == h, xt, 0.0) for h in range(GLA_HEADS)], axis=1).astype(BF16)

    for c in range(n):
        for b in range(BATCH):
            ext_ref[b, SUB + C * c:SUB + C * (c + 1), :] = u_ref[SC_ROWS * c + C * b:SC_ROWS * c + C * (b + 1), :]
    states = [st_ref[h] for h in range(GLA_HEADS)]
    cnt = cnt_ref[...]
    ubd = ubd_ref[...]
    gn = gn_ref[...]

    def chunk(c):
        rows = slice(SC_ROWS * c, SC_ROWS * (c + 1))
        bT = _dot3_right(lat_ref[:, rows], ubd)

        u = u_ref[rows, :]
        win = []
        for b in range(BATCH):
            level = ext_ref[b, C * c:C * c + SUB + C, :]
            for g in range(POOL_GROUPS):
                level = level[:, POOL_GROUP_DIM if g else 0:]
                level = level + pltpu.roll(level, 1 << g, axis=0)
                win.append(level[SUB:, :POOL_GROUP_DIM])
        parts = []
        for g in range(POOL_GROUPS):
            cols = slice(POOL_GROUP_DIM * g, POOL_GROUP_DIM * (g + 1))
            s = jnp.concatenate([win[b * POOL_GROUPS + g] for b in range(BATCH)], axis=0)
            p = s / cnt[:, cols] - u[:, cols]
            parts.append(_dot(p.astype(BF16), wpool_ref[g]))
        y_pool = jnp.concatenate(parts, axis=1) * pscale_ref[...]
        yield

        b_tm = _transpose_cm(bT)
        b_scr[c] = b_tm
        ends = [[b_scr[c, pl.ds(C * b + SUB * j + SUB - 1, 1), :] for j in range(N_SUB)] for b in range(BATCH)]

        def per_block(fn):
            return jnp.concatenate(
                [jnp.broadcast_to(fn(b, j), (SUB, GLA_KEY_WIDTH)) for b in range(BATCH) for j in range(N_SUB)], axis=0)

        k_tm = k_ref[rows, :]
        qT = qt_ref[:, rows]
        e_last = per_block(lambda b, j: ends[b][N_SUB - 1])
        ktil = k_tm * jnp.exp(e_last - b_tm)
        qeT = qT * jnp.exp(bT)

        if not robust:
            kneg = (k_tm * jnp.exp(-b_tm)).astype(BF16)
            at_all = jnp.where(causal_ref[...] != 0, _dot(kneg, heads_on_lanes(qeT)), 0.0)
        else:
            cprevT = _dot3_right(lat_ref[:, rows], uprev_ref[...])
            e_own = per_block(lambda b, j: ends[b][j])
            f2 = per_block(lambda b, j: jnp.exp(ends[b][min(j + 1, N_SUB - 1)] - ends[b][j]))
            f3 = per_block(lambda b, j: jnp.exp(ends[b][min(j + 2, N_SUB - 1)] - ends[b][j]))
            khat = k_tm * jnp.exp(e_own - b_tm)
            kslots = jnp.concatenate([khat, khat * f2, khat * f3], axis=0).astype(BF16)
            r_all = _dot(kslots, heads_on_lanes(qT * jnp.exp(bT - cprevT)))
            slot = slot_ref[...]
            at_off = jnp.where(slot == 1, r_all[0:SC_ROWS],
                               jnp.where(slot == 2, r_all[SC_ROWS:2 * SC_ROWS],
                                         jnp.where(slot == 3, r_all[2 * SC_ROWS:], 0.0)))
            kT = jnp.concatenate([k_tm[:, :LANES].T, k_tm[:, LANES:].T], axis=0)
            for dist in range(SUB):
                if dist == 0:
                    qs, bs = qT, bT
                else:
                    qs = pltpu.roll(qT, SC_ROWS - dist, axis=1)
                    bs = pltpu.roll(bT, SC_ROWS - dist, axis=1)
                prod = qs * kT * jnp.exp(jnp.minimum(bs - bT, 0.0))
                for h in range(GLA_HEADS):
                    sd_scr[pl.ds(SUB * h + dist, 1), :] = jnp.sum(
                        prod[GLA_DK * h:GLA_DK * (h + 1)], axis=0, keepdims=True)
            sd = jnp.concatenate([sd_scr[...], jnp.zeros((SC_ROWS - GLA_HEADS * SUB, SC_ROWS), F32)], axis=0)
            sdt = sd.T
            dmask = dmask_ref[...]
            for h in range(GLA_HEADS):
                xh = jnp.where((lane >= SUB * h) & (lane < SUB * (h + 1)), sdt, 0.0)
                skew = pltpu.roll(xh, (SC_ROWS - SUB * h) % SC_ROWS, axis=1, stride=1, stride_axis=0)
                at_scr[:, SC_ROWS * h:SC_ROWS * (h + 1)] = (
                    jnp.where(dmask != 0, skew, 0.0) + at_off[:, SC_ROWS * h:SC_ROWS * (h + 1)])
            at_all = at_scr[...]
        yield

        vT = vt_ref[:, rows]
        rT = rt_ref[:, rows]
        results = []
        for h in range(GLA_HEADS):
            at_h = at_all[:, SC_ROWS * h:SC_ROWS * (h + 1)]

            half = (h % 2) * C
            kt_cols = ktil[:, LANES * (h // 2):LANES * (h // 2 + 1)]
            kt_roll = pltpu.roll(kt_cols, C, axis=1)
            lo_src, hi_src = (kt_cols, kt_roll) if half == 0 else (kt_roll, kt_cols)
            kbd = jnp.where(row_lo & lane_lo, lo_src, jnp.where((~row_lo) & (~lane_lo), hi_src, 0.0))
            qe_h = qeT[GLA_DK * h:GLA_DK * (h + 1)]
            qebd = jnp.concatenate([jnp.where(lane64 < C, qe_h, 0.0), jnp.where(lane64 >= C, qe_h, 0.0)], axis=0)

            v_h = vT[GLA_DV * h:GLA_DV * (h + 1)]
            res = _dot(v_h, jnp.concatenate([at_h, kbd], axis=1).astype(BF16))
            results.append((res, _dot(states[h].astype(BF16), qebd.astype(BF16))))
        yield

        y_heads = []
        for h in range(GLA_HEADS):
            half = (h % 2) * C
            res, o_inter = results[h]
            st = states[h]
            o_t = res[:, :SC_ROWS] + o_inter

            e0 = jnp.broadcast_to(ends[0][N_SUB - 1][:, LANES * (h // 2):LANES * (h // 2 + 1)], (SUBLANES, LANES))
            e1 = jnp.broadcast_to(ends[1][N_SUB - 1][:, LANES * (h // 2):LANES * (h // 2 + 1)], (SUBLANES, LANES))
            if half == 0:
                e1 = pltpu.roll(e1, C, axis=1)
            else:
                e0 = pltpu.roll(e0, C, axis=1)
            dec = jnp.exp(jnp.where(lane8 < C, e0, e1))
            states[h] = st * jnp.broadcast_to(dec[0:1], (GLA_DV, SC_ROWS)) + res[:, SC_ROWS:]

            ms = jnp.mean(o_t * o_t, axis=0, keepdims=True)
            gsl = slice(GLA_DV * h, GLA_DV * (h + 1))
            y_heads.append(o_t * lax.rsqrt(ms + RMS_EPS) * gn[gsl] * rT[gsl])
        y_gla = jnp.concatenate([y.T for y in y_heads], axis=1)
        yield
        y_ref[rows, :] = jnp.concatenate([y_pool, y_gla], axis=1).astype(BF16)

    def in_turn():
        for c in range(n):
            yield from chunk(c)

    _staggered([in_turn()] if robust else [chunk(c) for c in range(n)])
    for h in range(GLA_HEADS):
        st_ref[h] = states[h]
    for b in range(BATCH):
        ext_ref[b, 0:SUB, :] = ext_ref[b, C * n:C * n + SUB, :]


def _outproj_ffn_rows(alpha, h, y, store, wo_ref, l2g_ref, l2b_ref, wg_ref, wu_ref, wd_ref, l3g_ref, l3b_ref):
    z = alpha * h + _dot(y, wo_ref[...])
    yield
    h2 = _layer_norm(z, l2g_ref[...], l2b_ref[...])
    yield
    z2 = yield from _swiglu_half_step(h2, wg_ref, wu_ref, wd_ref, alpha)
    store(_layer_norm(z2, l3g_ref[...], l3b_ref[...]))


def _mixer_kernel(tc, flags_ref, *refs):
    in_refs, const_refs, y_ref = refs[:N_MIX_IN], refs[N_MIX_IN:N_MIX_IN + N_MIX_CONST], refs[N_MIX_IN + N_MIX_CONST]
    st_ref, ext_ref, b_scr, sd_scr, at_scr = refs[N_MIX_IN + N_MIX_CONST + 1:]
    step = pl.program_id(0)

    def run(n, robust):
        _mix_chunks(n, robust, in_refs, const_refs, y_ref, st_ref, ext_ref, b_scr, sd_scr, at_scr)

    @pl.when(step == 0)
    def _():
        st_ref[...] = jnp.zeros(st_ref.shape, F32)
        ext_ref[:, 0:SUB, :] = jnp.zeros((BATCH, SUB, POOL_WIDTH), F32)
        run(1, True)

    @pl.when(step > 0)
    def _():
        base = (step - 1) * tc
        slow = flags_ref[base]
        for c in range(1, tc):
            slow = jnp.maximum(slow, flags_ref[base + c])

        @pl.when(slow == 0)
        def _():
            run(tc, False)

        @pl.when(slow != 0)
        def _():
            run(tc, True)


def _outproj_ffn_kernel(alpha, to_out, cast_plan, h_ref, y_ref, *refs):
    if cast_plan is not None:
        n_in = len(refs) - 2 * N_FFN_WEIGHTS - 1
        _cast_weight_slabs(cast_plan, refs[n_in:n_in + N_FFN_WEIGHTS], refs[-N_FFN_WEIGHTS:])
        refs = refs[:n_in] + refs[n_in + N_FFN_WEIGHTS:-N_FFN_WEIGHTS]
    *w_refs, o_ref = refs
    tile_rows = h_ref.shape[0]

    def store_rows(r0):
        def store(o):
            o_ref[r0:r0 + o.shape[0], :] = o
        return store

    def store_chunks(r0):
        def store(o):
            for sb in range(o.shape[0] // GLA_CHUNK):
                s, b = divmod(r0 // GLA_CHUNK + sb, BATCH)
                o_ref[b, s] = o[sb * GLA_CHUNK:(sb + 1) * GLA_CHUNK]
        return store

    def parts(bounds, make_store):
        _staggered([_outproj_ffn_rows(alpha, h_ref[r0:r1, :], y_ref[r0:r1, :], make_store(r0), *w_refs)
                    for r0, r1 in zip(bounds[:-1], bounds[1:])])

    bounds = list(range(0, tile_rows, min(PART_ROWS, tile_rows))) + [tile_rows]
    if to_out:
        parts(bounds, store_chunks)
        return
    last = pl.num_programs(0) - 1

    @pl.when(pl.program_id(0) < last)
    def _():
        parts(bounds, store_rows)

    @pl.when(pl.program_id(0) == last)
    def _():
        parts([0, SC_ROWS], store_rows)


def _resident(shape):
    nd = len(shape)
    return pl.BlockSpec(shape, lambda *_: (0,) * nd, pipeline_mode=pl.Buffered(1))


def _layer_resident(shape, layer):
    nd = len(shape) - 1
    return pl.BlockSpec((None,) + tuple(shape[1:]), lambda *_: (layer,) + (0,) * nd, pipeline_mode=pl.Buffered(1))


def _mixer_constants():
    r = np.arange(SC_ROWS)
    b, t = r // GLA_CHUNK, r % GLA_CHUNK
    same_b = b[:, None] == b[None, :]
    ubd = same_b & (t[:, None] <= t[None, :])
    blk = t // SUB
    uprev = same_b & (blk[:, None] < blk[None, :])
    dist = np.where(same_b, blk[None, :] - blk[:, None], 0)
    slot = np.where((dist >= 1) & (dist < N_SUB), dist, 0).astype(np.int32)
    slot = np.tile(slot, (1, GLA_HEADS))
    dmask = (same_b & (blk[:, None] == blk[None, :]) & (t[None, :] >= t[:, None])).astype(np.int32)
    causal = np.tile((same_b & (t[None, :] >= t[:, None])).astype(np.int32), (1, GLA_HEADS))
    w_lane = np.repeat(np.array(POOL_WINDOWS, np.float32), POOL_GROUP_DIM)[None, :]
    t_meta = np.maximum(t - (GLA_CHUNK - N_META), 0).astype(np.float32)[:, None]
    cnt = np.stack([np.broadcast_to(w_lane, (SC_ROWS, POOL_WIDTH)), np.minimum(t_meta + 1.0, w_lane)])
    return (jnp.asarray(ubd, BF16), jnp.asarray(uprev, BF16), jnp.asarray(slot),
            jnp.asarray(dmask), jnp.asarray(causal), jnp.asarray(cnt, F32))


def _tile_chunks(n_ch, most):
    return next(k for k in (8, 4, 2, 1) if k <= most and n_ch % k == 0)


def kernel(x, meta_tokens, ffn1_w_gate, ffn1_w_up, ffn1_w_down, ln1_g, ln1_b, w_in, w_gate_up, b_gate, w_pool, pool_scale, gla_norm_g, w_out, ln2_g, ln2_b, ffn2_w_gate, ffn2_w_up, ffn2_w_down, ln3_g, ln3_b):
    batch, seq, d = x.shape
    depth = w_in.shape[0]
    assert batch == BATCH and d == D_MODEL and seq % GLA_CHUNK == 0
    n_ch = seq // GLA_CHUNK
    n_sc = n_ch + 1
    n_rows = n_sc * SC_ROWS
    tc = _tile_chunks(n_ch, DENSE_TILE_CHUNKS)
    tr = tc * SC_ROWS
    n_main = n_ch // tc
    tcm = _tile_chunks(n_ch, MIXER_TILE_CHUNKS)
    n_mix = n_ch // tcm
    alpha = (2.0 * depth) ** 0.25

    x4 = x.reshape(batch, n_ch, GLA_CHUNK, d)
    meta_half = jnp.concatenate([jnp.zeros((GLA_CHUNK - N_META, d), x.dtype), meta_tokens.astype(x.dtype)], axis=0)
    meta_sc = jnp.concatenate([meta_half, meta_half], axis=0)
    h = None

    ubd, uprev, slot, dmask, causal, cnt = _mixer_constants()
    s0 = POOL_WIDTH
    s1 = s0 + GLA_KEY_WIDTH
    s2 = s1 + GLA_KEY_WIDTH
    s3 = s2 + GLA_WIDTH
    s4 = s3 + GLA_WIDTH
    cparams = pltpu.CompilerParams(dimension_semantics=("arbitrary",), vmem_limit_bytes=VMEM_LIMIT_BYTES)
    row_tile = lambda cols: pl.BlockSpec((tr, cols), lambda i: (i, 0))
    col_tile = lambda rows: pl.BlockSpec((rows, tr), lambda i: (0, i))
    x4_tile = pl.BlockSpec((batch, tc, GLA_CHUNK, d), lambda i: (0, jnp.minimum(i, n_main - 1), 0, 0))
    mix_tile = lambda i: lax.rem(i + n_mix, n_mix + 1)
    mix_rows = lambda cols: pl.BlockSpec((tcm * SC_ROWS, cols), lambda i, f: (mix_tile(i), 0))
    mix_cols = lambda rows: pl.BlockSpec((rows, tcm * SC_ROWS), lambda i, f: (0, mix_tile(i)))

    ffn1 = tuple(w[0].astype(BF16) for w in (ffn1_w_gate, ffn1_w_up, ffn1_w_down))
    ffn_specs = [_resident(w.shape) for w in ffn1]
    wo_all, wpool_all = w_out.astype(BF16), w_pool.astype(BF16)
    row3 = lambda p: p.reshape(depth, 1, p.shape[-1])
    l1g, l1b, l2g, l2b, l3g, l3b = (row3(p) for p in (ln1_g, ln1_b, ln2_g, ln2_b, ln3_g, ln3_b))
    pscale_all = row3(pool_scale)

    def cast_job(n_steps, layer, stacked):
        steps = n_steps // N_FFN_WEIGHTS
        assert steps >= 1, "weight-cast side job needs at least three grid steps"
        width, slabs = (WEIGHT_SLAB, D_FF // WEIGHT_SLAB) if steps >= D_FF // WEIGHT_SLAB else (D_FF, 1)
        slab = lambda m: (lambda i: jnp.clip(i - m * steps, 0, slabs - 1))
        in_specs = [pl.BlockSpec((None, d, width), lambda i, j=slab(0): (layer, 0, j(i))),
                    pl.BlockSpec((None, d, width), lambda i, j=slab(1): (layer, 0, j(i))),
                    pl.BlockSpec((None, width, d), lambda i, j=slab(2): (layer, j(i), 0))]
        out_specs = [pl.BlockSpec((d, width), lambda i, j=slab(0): (0, j(i))),
                     pl.BlockSpec((d, width), lambda i, j=slab(1): (0, j(i))),
                     pl.BlockSpec((width, d), lambda i, j=slab(2): (j(i), 0))]
        out_shapes = [jax.ShapeDtypeStruct(w.shape[1:], BF16) for w in stacked]
        return (steps, slabs), in_specs, out_specs, out_shapes

    for l in range(depth):
        wl = w_in[l]
        glr_pad = jnp.zeros((d, LANES - GLA_GATE_RANK), F32)
        wtm = jnp.concatenate([wl[:, :s0], wl[:, s1:s2]], axis=1).astype(BF16)
        wcm = jnp.concatenate([wl[:, s0:s1] * (GLA_DK ** -0.5), wl[:, s2:s3], wl[:, s3:s4]], axis=1).T.astype(BF16)
        wgt = jnp.concatenate([wl[:, s4:], glr_pad], axis=1).T.astype(BF16)
        wgut = jnp.concatenate([w_gate_up[l], jnp.zeros((LANES - GLA_GATE_RANK, GLA_KEY_WIDTH), F32)], axis=0).T.astype(BF16)
        bgc = b_gate[l].reshape(GLA_KEY_WIDTH, 1)

        from_x = l == 0
        acts = (x4, meta_sc) if from_x else (h,)
        act_specs = [x4_tile, _resident(meta_sc.shape)] if from_x else [row_tile(d)]
        ffn2_f32 = (ffn2_w_gate, ffn2_w_up, ffn2_w_down)
        plan, cast_in, cast_out, cast_shapes = cast_job(n_main + 1, l, ffn2_f32)
        outs = pl.pallas_call(
            functools.partial(_ffn_inproj_kernel, alpha, from_x, tr, plan),
            grid=(n_main + 1,),
            in_specs=act_specs + ffn_specs + [_layer_resident(l1g.shape, l), _layer_resident(l1b.shape, l),
                                              _resident(wtm.shape), _resident(wcm.shape), _resident(wgt.shape),
                                              _resident(wgut.shape), _resident(bgc.shape)] + cast_in,
            out_specs=[row_tile(d), row_tile(POOL_WIDTH), row_tile(GLA_KEY_WIDTH),
                       col_tile(GLA_KEY_WIDTH), col_tile(GLA_KEY_WIDTH), col_tile(GLA_WIDTH), col_tile(GLA_WIDTH),
                       pl.BlockSpec((FLAG_ROWS, LANES), lambda i: (i, 0))] + cast_out,
            out_shape=[jax.ShapeDtypeStruct((n_rows, d), F32),
                       jax.ShapeDtypeStruct((n_rows, POOL_WIDTH), F32),
                       jax.ShapeDtypeStruct((n_rows, GLA_KEY_WIDTH), F32),
                       jax.ShapeDtypeStruct((GLA_KEY_WIDTH, n_rows), F32),
                       jax.ShapeDtypeStruct((GLA_KEY_WIDTH, n_rows), F32),
                       jax.ShapeDtypeStruct((GLA_WIDTH, n_rows), BF16),
                       jax.ShapeDtypeStruct((GLA_WIDTH, n_rows), F32),
                       jax.ShapeDtypeStruct(((n_main + 1) * FLAG_ROWS, LANES), jnp.int32)] + cast_shapes,
            compiler_params=cparams,
            name=f"ffn_inproj_{l}",
        )(*acts, *ffn1, l1g, l1b, wtm, wcm, wgt, wgut, bgc, *ffn2_f32)
        h1, u_tm, k_tm, q_t, la_t, v_t, r_t, tile_flags = outs[:N_INPROJ_OUT]
        ffn2 = outs[N_INPROJ_OUT:]
        sc_flags = tile_flags.reshape(n_main + 1, FLAG_ROWS, LANES)[:, :tc, 0].reshape(-1)[:n_sc]

        gn = jnp.broadcast_to(gla_norm_g[l].reshape(GLA_WIDTH, 1), (GLA_WIDTH, SC_ROWS))
        mix_in = (u_tm, k_tm, q_t, la_t, v_t, r_t)
        mix_in_specs = [mix_rows(POOL_WIDTH), mix_rows(GLA_KEY_WIDTH), mix_cols(GLA_KEY_WIDTH),
                        mix_cols(GLA_KEY_WIDTH), mix_cols(GLA_WIDTH), mix_cols(GLA_WIDTH)]
        mix_const = (ubd, uprev, slot, dmask, causal, gn, cnt, wpool_all, pscale_all)
        mix_const_specs = [_resident(ubd.shape), _resident(uprev.shape), _resident(slot.shape),
                           _resident(dmask.shape), _resident(causal.shape), _resident(gn.shape),
                           pl.BlockSpec((None, SC_ROWS, POOL_WIDTH), lambda i, f: (jnp.where(i == 0, 1, 0), 0, 0)),
                           _layer_resident(wpool_all.shape, l), _layer_resident(pscale_all.shape, l)]
        assert len(mix_in) == N_MIX_IN and len(mix_const) == N_MIX_CONST
        y_cat = pl.pallas_call(
            functools.partial(_mixer_kernel, tcm),
            grid_spec=pltpu.PrefetchScalarGridSpec(
                num_scalar_prefetch=1,
                grid=(n_mix + 1,),
                in_specs=mix_in_specs + mix_const_specs,
                out_specs=mix_rows(d),
                scratch_shapes=[pltpu.VMEM((GLA_HEADS, GLA_DV, SC_ROWS), F32),
                                pltpu.VMEM((BATCH, SUB + tcm * GLA_CHUNK, POOL_WIDTH), F32),
                                pltpu.VMEM((tcm, SC_ROWS, GLA_KEY_WIDTH), F32),
                                pltpu.VMEM((GLA_HEADS * SUB, SC_ROWS), F32),
                                pltpu.VMEM((SC_ROWS, GLA_HEADS * SC_ROWS), F32)]),
            out_shape=jax.ShapeDtypeStruct((n_rows, d), BF16),
            compiler_params=cparams,
            name=f"mixer_{l}",
        )(sc_flags, *mix_in, *mix_const)

        to_out = l == depth - 1
        post_specs = [row_tile(d), row_tile(d), _layer_resident(wo_all.shape, l),
                      _layer_resident(l2g.shape, l), _layer_resident(l2b.shape, l)] + ffn_specs + [
                      _layer_resident(l3g.shape, l), _layer_resident(l3b.shape, l)]
        post_args = (h1, y_cat, wo_all, l2g, l2b, *ffn2, l3g, l3b)
        if to_out:
            h = pl.pallas_call(
                functools.partial(_outproj_ffn_kernel, alpha, True, None),
                grid=(n_main,),
                in_specs=post_specs,
                out_specs=pl.BlockSpec((batch, tc, GLA_CHUNK, d), lambda i: (0, i, 0, 0)),
                out_shape=jax.ShapeDtypeStruct(x4.shape, F32),
                compiler_params=cparams,
                name=f"outproj_ffn_{l}",
            )(*post_args)
        else:
            ffn1_f32 = (ffn1_w_gate, ffn1_w_up, ffn1_w_down)
            plan, cast_in, cast_out, cast_shapes = cast_job(n_main + 1, l + 1, ffn1_f32)
            h, *ffn1 = pl.pallas_call(
                functools.partial(_outproj_ffn_kernel, alpha, False, plan),
                grid=(n_main + 1,),
                in_specs=post_specs + cast_in,
                out_specs=[row_tile(d)] + cast_out,
                out_shape=[jax.ShapeDtypeStruct((n_rows, d), F32)] + cast_shapes,
                compiler_params=cparams,
                name=f"outproj_ffn_{l}",
            )(*post_args, *ffn1_f32)

    return h.reshape(batch, seq, d)
```
